```python
import math
import jax, jax.numpy as jnp
from jax import lax
import numpy as np


D_MODEL = 2048
BATCH = 2
SEQ = 4096
DEPTH = 4

HEAD_DIM = 64
RWKV_HEADS = 16
RWKV_DIM = RWKV_HEADS * HEAD_DIM
DECAY_LORA = 96
AAA_LORA = 96
GATE_LORA = 256
ATTN_GROUPS = ((128, 1), (512, 4), (2048, 16))
N_GROUPS = len(ATTN_GROUPS)
ATTN_SLOTS = 8
ATTN_HEADS = N_GROUPS * ATTN_SLOTS
ATTN_DIM = ATTN_HEADS * HEAD_DIM
ATTN_OUT_DIM = ATTN_SLOTS * HEAD_DIM
N_BUCKETS = 32
MAX_DISTANCE = 1024
D_FF = 4 * D_MODEL
N_BRANCHES = 2
RWKV_COLS = 3 * RWKV_DIM + GATE_LORA + 2 * DECAY_LORA + 2 * AAA_LORA
N_IN_COLS = RWKV_COLS + 3 * ATTN_DIM + N_BRANCHES * D_MODEL
RMS_EPS = 1e-6
GN_EPS = 64e-5
L2_EPS = 1e-12

kernel_name = 'hybrid_rwkv7_dilated_attn_encoder'


def rmsnorm(x, g):
    xf = x.astype(jnp.float32)
    y = xf * lax.rsqrt(jnp.mean(xf * xf, axis=-1, keepdims=True) + RMS_EPS)
    return (y * g.astype(jnp.float32)).astype(x.dtype)


def t5_bucket(rel):
    nb = N_BUCKETS // 2
    max_exact = nb // 2
    ret = jnp.where(rel > 0, nb, 0)
    n = jnp.abs(rel)
    nf = jnp.maximum(n, 1).astype(jnp.float32)
    large = max_exact + (jnp.log(nf / max_exact) / math.log(MAX_DISTANCE / max_exact)
                         * (nb - max_exact)).astype(jnp.int32)
    large = jnp.minimum(large, nb - 1)
    return ret + jnp.where(n < max_exact, n, large)


def dilated_window_attention(q, k, v, bias_table, dilation, half):
    B, S, H, Dh = q.shape
    L = S // dilation
    Q = half
    nb = -(-L // Q)
    Lp = nb * Q

    def to_sub(t):
        t = t.reshape(B, L, dilation, H, Dh).transpose(0, 2, 1, 3, 4)
        return jnp.pad(t, ((0, 0), (0, 0), (0, Lp - L), (0, 0), (0, 0)))

    def windows(t):
        t = jnp.pad(to_sub(t), ((0, 0), (0, 0), (Q, Q), (0, 0), (0, 0)))
        t = t.reshape(B, dilation, nb + 2, Q, H, Dh)
        return jnp.concatenate([t[:, :, :-2], t[:, :, 1:-1], t[:, :, 2:]], axis=3)

    qs = to_sub(q).reshape(B, dilation, nb, Q, H, Dh)
    kw = windows(k)
    vw = windows(v)
    rel = jnp.arange(3 * Q)[None, :] - Q - jnp.arange(Q)[:, None]
    band = jnp.abs(rel) <= half
    key_idx = jnp.arange(nb)[:, None] * Q - Q + jnp.arange(3 * Q)[None, :]
    valid = (key_idx >= 0) & (key_idx < L)
    mask = band[None] & valid[:, None, :]
    bias = bias_table[t5_bucket(rel * dilation)].transpose(2, 0, 1)
    logits = jnp.einsum('brnqhd,brnkhd->brnhqk', qs, kw) * (Dh ** -0.5) + bias
    logits = jnp.where(mask[None, None, :, None], logits, -jnp.inf)
    m = jnp.max(logits, axis=-1, keepdims=True)
    p = jnp.exp(logits - m)
    denom = jnp.sum(p, axis=-1, keepdims=True)
    o = jnp.einsum('brnhqk,brnkhd->brnqhd', p / denom, vw)
    lse = (m + jnp.log(denom))[..., 0].transpose(0, 1, 2, 4, 3)

    def from_sub(t):
        t = t.reshape((B, dilation, Lp) + t.shape[4:])[:, :, :L]
        t = jnp.moveaxis(t, 1, 2)
        return t.reshape((B, S) + t.shape[3:])

    return from_sub(o), from_sub(lse)


def attention_branch(q, k, v, rel_bias):
    B, S, _ = q.shape
    heads = lambda t: t.astype(jnp.float32).reshape(B, S, ATTN_HEADS, HEAD_DIM)
    q, k, v = heads(q), heads(k), heads(v)
    table = rel_bias.astype(jnp.float32)
    outs, lses = [], []
    for gi, (window, dilation) in enumerate(ATTN_GROUPS):
        hs = slice(gi * ATTN_SLOTS, (gi + 1) * ATTN_SLOTS)
        o, lse = dilated_window_attention(q[:, :, hs], k[:, :, hs], v[:, :, hs],
                                          table[:, hs], dilation, window // (2 * dilation))
        outs.append(o)
        lses.append(lse)
    wts = jax.nn.softmax(jnp.stack(lses, axis=0), axis=0)
    o = jnp.sum(wts[..., None] * jnp.stack(outs, axis=0), axis=0)
    return o.reshape(B, S, ATTN_OUT_DIM)


def rwkv7_scan(r, w, k, v, kk, a, reverse):
    B, S, H, N = r.shape

    def step(state, inp):
        r_t, w_t, k_t, v_t, kk_t, a_t = inp
        sk = jnp.einsum('bhij,bhj->bhi', state, kk_t)
        state = (state * w_t[:, :, None, :] - sk[..., :, None] * (kk_t * a_t)[:, :, None, :]
                 + v_t[..., :, None] * k_t[:, :, None, :])
        return state, jnp.einsum('bhij,bhj->bhi', state, r_t)

    xs = tuple(jnp.moveaxis(t, 1, 0) for t in (r, w, k, v, kk, a))
    s0 = jnp.zeros((B, H, N, N), jnp.float32)
    _, y = lax.scan(step, s0, xs, reverse=reverse)
    return jnp.moveaxis(y, 0, 1)


def rwkv_branch(slab, mu, w0, w_up, a0, a_up, g_up, k_k, k_a, r_k, gn_w, gn_b):
    f32 = lambda t: t.astype(jnp.float32)
    slab = f32(slab)
    B, S, _ = slab.shape
    prev = jnp.pad(slab[:, :-1], ((0, 0), (1, 0), (0, 0)))
    nxt = jnp.pad(slab[:, 1:], ((0, 0), (0, 1), (0, 0)))
    p = slab + f32(mu) * (0.5 * (prev + nxt) - slab)
    splits = np.cumsum([RWKV_DIM, RWKV_DIM, RWKV_DIM, GATE_LORA, DECAY_LORA, DECAY_LORA, AAA_LORA])
    r, k, v, gd, wdf, wdb, adf, adb = jnp.split(p, splits, axis=-1)
    heads = lambda t: t.reshape(B, S, RWKV_HEADS, HEAD_DIM)
    g = jnp.matmul(jax.nn.sigmoid(gd), f32(g_up))
    kk = heads(k * f32(k_k))
    kk = kk / jnp.maximum(jnp.sqrt(jnp.sum(kk * kk, axis=-1, keepdims=True)), L2_EPS)
    rh, vh = heads(r), heads(v)
    y = jnp.zeros_like(rh)
    bonus = jnp.zeros_like(rh)
    for direction, (wd, ad) in enumerate(((wdf, adf), (wdb, adb))):
        wl = -jax.nn.softplus(-(f32(w0[direction]) + jnp.matmul(jnp.tanh(wd), f32(w_up[direction])))) - 0.5
        decay = jnp.exp(-jnp.exp(wl))
        a = jax.nn.sigmoid(f32(a0[direction]) + jnp.matmul(ad, f32(a_up[direction])))
        kd = heads(k * (1.0 + (a - 1.0) * f32(k_a)))
        y = y + rwkv7_scan(rh, heads(decay), kd, vh, kk, heads(a), reverse=(direction == 1))
        bonus = bonus + jnp.sum(rh * kd * f32(r_k), axis=-1, keepdims=True) * vh
    mean = jnp.mean(y, axis=-1, keepdims=True)
    var = jnp.mean(jnp.square(y - mean), axis=-1, keepdims=True)
    yn = ((y - mean) * lax.rsqrt(var + GN_EPS)).reshape(B, S, RWKV_DIM) * f32(gn_w) + f32(gn_b)
    return (yn + bonus.reshape(B, S, RWKV_DIM)) * g


def setup_inputs(seed: int = 0) -> dict:
    key = jax.random.key(seed)
    ks = jax.random.split(key, 24)
    nrm = lambda kk_, shape, scale: scale * jax.random.normal(kk_, shape, jnp.float32)
    x = nrm(ks[0], (BATCH, SEQ, D_MODEL), 1.0)
    norm1_g = 1.0 + nrm(ks[1], (DEPTH, D_MODEL), 0.02)
    w_in = nrm(ks[2], (DEPTH, D_MODEL, N_IN_COLS), D_MODEL ** -0.5)
    tshift_mu = jax.random.uniform(ks[3], (DEPTH, RWKV_COLS), jnp.float32, 0.2, 0.8)
    w0 = jax.random.uniform(ks[4], (DEPTH, 2, RWKV_DIM), jnp.float32, -6.5, -0.5)
    w_lora_up = nrm(ks[5], (DEPTH, 2, DECAY_LORA, RWKV_DIM), 0.5 * DECAY_LORA ** -0.5)
    a0 = nrm(ks[6], (DEPTH, 2, RWKV_DIM), 0.5)
    a_lora_up = nrm(ks[7], (DEPTH, 2, AAA_LORA, RWKV_DIM), 0.5 * AAA_LORA ** -0.5)
    g_lora_up = nrm(ks[8], (DEPTH, GATE_LORA, RWKV_DIM), GATE_LORA ** -0.5)
    k_k = 0.85 + nrm(ks[9], (DEPTH, RWKV_DIM), 0.05)
    k_a = 1.0 + nrm(ks[10], (DEPTH, RWKV_DIM), 0.05)
    r_k = nrm(ks[11], (DEPTH, RWKV_HEADS, HEAD_DIM), 0.1)
    gn_w = 1.0 + nrm(ks[12], (DEPTH, RWKV_DIM), 0.02)
    gn_b = nrm(ks[13], (DEPTH, RWKV_DIM), 0.02)
    rel_bias = nrm(ks[14], (N_BUCKETS, ATTN_HEADS), 0.5)
    w_branch_rwkv = nrm(ks[15], (DEPTH, RWKV_DIM, D_MODEL), RWKV_DIM ** -0.5)
    w_branch_attn = nrm(ks[16], (DEPTH, ATTN_OUT_DIM, D_MODEL), ATTN_OUT_DIM ** -0.5)
    w_out = nrm(ks[17], (DEPTH, D_MODEL, D_MODEL), D_MODEL ** -0.5)
    norm2_g = 1.0 + nrm(ks[18], (DEPTH, D_MODEL), 0.02)
    w_mlp_in = nrm(ks[19], (DEPTH, D_MODEL, D_FF), D_MODEL ** -0.5)
    w_mlp_out = nrm(ks[20], (DEPTH, D_FF, D_MODEL), D_FF ** -0.5)
    final_g = 1.0 + nrm(ks[21], (D_MODEL,), 0.02)
    return {'x': x, 'norm1_g': norm1_g, 'w_in': w_in, 'tshift_mu': tshift_mu, 'w0': w0,
            'w_lora_up': w_lora_up, 'a0': a0, 'a_lora_up': a_lora_up, 'g_lora_up': g_lora_up,
            'k_k': k_k, 'k_a': k_a, 'r_k': r_k, 'gn_w': gn_w, 'gn_b': gn_b, 'rel_bias': rel_bias,
            'w_branch_rwkv': w_branch_rwkv, 'w_branch_attn': w_branch_attn, 'w_out': w_out,
            'norm2_g': norm2_g, 'w_mlp_in': w_mlp_in, 'w_mlp_out': w_mlp_out, 'final_g': final_g}


def reference(x, norm1_g, w_in, tshift_mu, w0, w_lora_up, a0, a_lora_up, g_lora_up, k_k, k_a,
              r_k, gn_w, gn_b, rel_bias, w_branch_rwkv, w_branch_attn, w_out, norm2_g,
              w_mlp_in, w_mlp_out, final_g):
    dtype = x.dtype
    cuts = [RWKV_COLS, RWKV_COLS + ATTN_DIM, RWKV_COLS + 2 * ATTN_DIM, RWKV_COLS + 3 * ATTN_DIM]
    h = x
    for l in range(DEPTH):
        u = rmsnorm(h, norm1_g[l])
        proj = jnp.matmul(u, w_in[l])
        slab, q, k, v, gates = jnp.split(proj, cuts, axis=-1)
        o_rwkv = rwkv_branch(slab, tshift_mu[l], w0[l], w_lora_up[l], a0[l], a_lora_up[l],
                             g_lora_up[l], k_k[l], k_a[l], r_k[l], gn_w[l], gn_b[l]).astype(dtype)
        o_attn = attention_branch(q, k, v, rel_bias).astype(dtype)
        gate = jax.nn.sigmoid(gates.astype(jnp.float32)).astype(dtype)
        g_rwkv, g_attn = jnp.split(gate, 2, axis=-1)
        merged = (g_rwkv * jnp.matmul(o_rwkv, w_branch_rwkv[l])
                  + g_attn * jnp.matmul(o_attn, w_branch_attn[l]))
        h = h + jnp.matmul(merged, w_out[l])
        u = rmsnorm(h, norm2_g[l])
        h = h + jnp.matmul(jnp.square(jax.nn.relu(jnp.matmul(u, w_mlp_in[l]))), w_mlp_out[l])
    return rmsnorm(h, final_g)
```

```python
import functools
import math

import numpy as np
import jax
import jax.numpy as jnp
from jax import lax
from jax.experimental import pallas as pl
from jax.experimental.pallas import tpu as pltpu

F32 = jnp.float32
BF16 = jnp.bfloat16

HEAD_DIM = 64
LANES = 128
RWKV_DIM = 1024
DECAY_LORA = 96
AAA_LORA = 96
GATE_LORA = 256
LORA_COLS = GATE_LORA + 2 * DECAY_LORA + 2 * AAA_LORA
LORA_TAIL = LORA_COLS - GATE_LORA
ATTN_GROUPS = ((128, 1), (512, 4), (2048, 16))
ATTN_SLOTS = 8
ATTN_DIM = 1536
ATTN_OUT_DIM = ATTN_SLOTS * HEAD_DIM
ATTN_HALF = 64
N_BUCKETS = 32
MAX_DISTANCE = 1024
RMS_EPS = 1e-6
GN_EPS = 64e-5
L2_EPS = 1e-12
NEG_BIG = -1e30
CHUNK = 64
VMEM_LIMIT = 56 * 1024 * 1024


def _cparams(*sem):
    return pltpu.CompilerParams(dimension_semantics=sem, vmem_limit_bytes=VMEM_LIMIT)


def _dot(a, b):
    return jnp.dot(a.astype(BF16), b.astype(BF16), preferred_element_type=F32)


def _dot_nt(a, b):
    return lax.dot_general(a.astype(BF16), b.astype(BF16), (((1,), (1,)), ((), ())),
                           preferred_element_type=F32)


def _dot_tn(a, b):
    return lax.dot_general(a.astype(BF16), b.astype(BF16), (((0,), (0,)), ((), ())),
                           preferred_element_type=F32)


def _dot_hi(a, b):
    a0 = a.astype(BF16)
    a1 = (a - a0.astype(F32)).astype(BF16)
    b0 = b.astype(BF16)
    b1 = (b - b0.astype(F32)).astype(BF16)
    return (jnp.dot(a0, b0, preferred_element_type=F32) + jnp.dot(a0, b1, preferred_element_type=F32)
            + jnp.dot(a1, b0, preferred_element_type=F32))


def _split3(x):
    p0 = x.astype(BF16)
    r1 = x - p0.astype(F32)
    p1 = r1.astype(BF16)
    p2 = (r1 - p1.astype(F32)).astype(BF16)
    return p0, p1, p2


def _dot_exact_rhs(a_bf16, x):
    return sum(jnp.dot(a_bf16, p, preferred_element_type=F32) for p in _split3(x))


def _head_sum(x, ones_bd):
    return sum(jnp.dot(p, ones_bd, preferred_element_type=F32) for p in _split3(x))


def _head_block_ones(width):
    r = lax.broadcasted_iota(jnp.int32, (width, width), 0) // HEAD_DIM
    c = lax.broadcasted_iota(jnp.int32, (width, width), 1) // HEAD_DIM
    return (r == c).astype(BF16)


def _norm_mm_kernel(x_ref, g_ref, w_ref, o_ref, u_ref, *, act):
    @pl.when(pl.program_id(1) == 0)
    def _():
        x = x_ref[...]
        ms = jnp.mean(x * x, axis=-1, keepdims=True)
        u_ref[...] = (x * lax.rsqrt(ms + RMS_EPS) * g_ref[...]).astype(BF16)

    acc = jnp.dot(u_ref[...], w_ref[...], preferred_element_type=F32)
    if act == "relu2":
        acc = jnp.square(jnp.maximum(acc, 0.0))
    elif act == "sigmoid":
        acc = jax.nn.sigmoid(acc)
    o_ref[...] = acc.astype(o_ref.dtype)


def norm_matmul(x, g, w, *, tm, tn, act=None, out_dtype=F32):
    M, K = x.shape
    N = w.shape[1]
    return pl.pallas_call(
        functools.partial(_norm_mm_kernel, act=act),
        grid=(M // tm, N // tn),
        in_specs=[pl.BlockSpec((tm, K), lambda i, j: (i, 0)),
                  pl.BlockSpec((1, K), lambda i, j: (0, 0)),
                  pl.BlockSpec((K, tn), lambda i, j: (0, j))],
        out_specs=pl.BlockSpec((tm, tn), lambda i, j: (i, j)),
        out_shape=jax.ShapeDtypeStruct((M, N), out_dtype),
        scratch_shapes=[pltpu.VMEM((tm, K), BF16)],
        compiler_params=_cparams("parallel", "arbitrary"),
        name="norm_matmul_" + (act or "id"),
    )(x, g.reshape(1, K), w)


def _mm_res_kernel(x_ref, w_ref, r_ref, o_ref, acc_ref):
    k = pl.program_id(2)

    @pl.when(k == 0)
    def _():
        acc_ref[...] = jnp.zeros_like(acc_ref)

    acc_ref[...] += jnp.dot(x_ref[...], w_ref[...], preferred_element_type=F32)

    @pl.when(k == pl.num_programs(2) - 1)
    def _():
        o_ref[...] = r_ref[...] + acc_ref[...]


def matmul_residual(x, w, res, *, tm, tn, tk):
    M, K = x.shape
    N = w.shape[1]
    return pl.pallas_call(
        _mm_res_kernel,
        grid=(M // tm, N // tn, K // tk),
        in_specs=[pl.BlockSpec((tm, tk), lambda i, j, k: (i, k)),
                  pl.BlockSpec((tk, tn), lambda i, j, k: (k, j)),
                  pl.BlockSpec((tm, tn), lambda i, j, k: (i, j))],
        out_specs=pl.BlockSpec((tm, tn), lambda i, j, k: (i, j)),
        out_shape=jax.ShapeDtypeStruct((M, N), F32),
        scratch_shapes=[pltpu.VMEM((tm, tn), F32)],
        compiler_params=_cparams("parallel", "parallel", "arbitrary"),
        name="matmul_residual",
    )(x, w, res)


def _rmsnorm_kernel(x_ref, g_ref, o_ref):
    x = x_ref[...]
    ms = jnp.mean(x * x, axis=-1, keepdims=True)
    o_ref[...] = x * lax.rsqrt(ms + RMS_EPS) * g_ref[...]


def rmsnorm_rows(x, g, *, tm):
    M, K = x.shape
    return pl.pallas_call(
        _rmsnorm_kernel,
        grid=(M // tm,),
        in_specs=[pl.BlockSpec((tm, K), lambda i: (i, 0)),
                  pl.BlockSpec((1, K), lambda i: (0, 0))],
        out_specs=pl.BlockSpec((tm, K), lambda i: (i, 0)),
        out_shape=jax.ShapeDtypeStruct((M, K), F32),
        compiler_params=_cparams("parallel"),
        name="final_rmsnorm",
    )(x, g.reshape(1, K))


def _token_shift(main, prev_blk, next_blk, mu, first, last):
    rows = main.shape[0]
    prev_row = jnp.where(first, 0.0, prev_blk[7:8, :])
    next_row = jnp.where(last, 0.0, next_blk[0:1, :])
    rid = lax.broadcasted_iota(jnp.int32, main.shape, 0)
    up = jnp.where(rid == 0, prev_row, pltpu.roll(main, 1, 0))
    dn = jnp.where(rid == rows - 1, next_row, pltpu.roll(main, rows - 1, 0))
    return main + mu * (0.5 * (up + dn) - main)


def _rwkv_prep_kernel(r_m, r_p, r_n, k_m, k_p, k_n, v_m, v_p, v_n, l_m, l_p, l_n,
                      mu_r, mu_k, mu_v, mu_l, w0_ref, a0_ref, lora_w_ref, g_up_ref,
                      kk_ref, ka_ref, rk_ref,
                      r_o, v_o, kk_o, lwf_o, lwb_o, kaf_o, kab_o, kdf_o, kdb_o, g_o, bonus_o):
    i = pl.program_id(1)
    first = i == 0
    last = i == pl.num_programs(1) - 1
    r = _token_shift(r_m[0], r_p[0], r_n[0], mu_r[...], first, last)
    k = _token_shift(k_m[0], k_p[0], k_n[0], mu_k[...], first, last)
    v = _token_shift(v_m[0], v_p[0], v_n[0], mu_v[...], first, last)
    lo = _token_shift(l_m[0], l_p[0], l_n[0], mu_l[...], first, last)

    g = _dot(jax.nn.sigmoid(lo[:, :GATE_LORA]), g_up_ref[...])
    tail = lo[:, GATE_LORA:]
    col = lax.broadcasted_iota(jnp.int32, tail.shape, 1)
    tail = jnp.where(col < 2 * DECAY_LORA, jnp.tanh(tail), tail)
    up = _dot(tail, lora_w_ref[...])

    ones_bd = _head_block_ones(2 * LANES)

    def head_sum(x):
        return jnp.concatenate(
            [_head_sum(x[:, c:c + 2 * LANES], ones_bd) for c in range(0, RWKV_DIM, 2 * LANES)], axis=1)

    kk = k * kk_ref[...]
    kk = kk / jnp.maximum(jnp.sqrt(head_sum(kk * kk)), L2_EPS)

    r_o[0] = r
    v_o[0] = v
    kk_o[0] = kk
    g_o[0] = g
    kd_sum = jnp.zeros_like(k)
    for d, (lw_o, ka_o, kd_o) in enumerate(((lwf_o, kaf_o, kdf_o), (lwb_o, kab_o, kdb_o))):
        z = w0_ref[d:d + 1, :] + up[:, d * RWKV_DIM:(d + 1) * RWKV_DIM]
        nz = -z
        softplus = jnp.maximum(nz, 0.0) + jnp.log1p(jnp.exp(-jnp.abs(nz)))
        lw_o[0] = -jnp.exp(-softplus - 0.5)
        a = jax.nn.sigmoid(a0_ref[d:d + 1, :] + up[:, (2 + d) * RWKV_DIM:(3 + d) * RWKV_DIM])
        kd = k * (1.0 + (a - 1.0) * ka_ref[...])
        ka_o[0] = kk * a
        kd_o[0] = kd
        kd_sum = kd_sum + kd
    bonus_o[0] = head_sum(r * kd_sum * rk_ref[...]) * v


def rwkv_prep(rkv, lora, mu, w0, a0, lora_w, g_up, k_k, k_a, r_k, *, ts):
    B, S, _ = rkv.shape
    D = RWKV_DIM
    nblk8 = S // 8

    def main_spec(width, cblk):
        return pl.BlockSpec((1, ts, width), lambda b, i: (b, i, cblk))

    def prev_spec(width, cblk):
        return pl.BlockSpec((1, 8, width), lambda b, i: (b, jnp.maximum(i * (ts // 8) - 1, 0), cblk))

    def next_spec(width, cblk):
        return pl.BlockSpec((1, 8, width), lambda b, i: (b, jnp.minimum((i + 1) * (ts // 8), nblk8 - 1), cblk))

    def row_spec(width):
        return pl.BlockSpec((1, width), lambda b, i: (0, 0))

    def full_spec(shape):
        return pl.BlockSpec(shape, lambda b, i: (0,) * len(shape))

    in_specs = []
    args = []
    for cblk in range(3):
        in_specs += [main_spec(D, cblk), prev_spec(D, cblk), next_spec(D, cblk)]
        args += [rkv, rkv, rkv]
    in_specs += [main_spec(LORA_COLS, 0), prev_spec(LORA_COLS, 0), next_spec(LORA_COLS, 0)]
    args += [lora, lora, lora]
    mu_r, mu_k, mu_v, mu_l = (mu[None, 0:D], mu[None, D:2 * D], mu[None, 2 * D:3 * D], mu[None, 3 * D:])
    in_specs += [row_spec(D), row_spec(D), row_spec(D), row_spec(LORA_COLS),
                 full_spec((2, D)), full_spec((2, D)), full_spec((LORA_TAIL, 4 * D)),
                 full_spec((GATE_LORA, D)), row_spec(D), row_spec(D), row_spec(D)]
    args += [mu_r, mu_k, mu_v, mu_l, w0, a0, lora_w, g_up, k_k[None], k_a[None], r_k.reshape(1, D)]
    out_spec = pl.BlockSpec((1, ts, D), lambda b, i: (b, i, 0))
    n_out = 11
    return pl.pallas_call(
        _rwkv_prep_kernel,
        grid=(B, S // ts),
        in_specs=in_specs,
        out_specs=[out_spec] * n_out,
        out_shape=[jax.ShapeDtypeStruct((B, S, D), F32)] * n_out,
        compiler_params=_cparams("parallel", "arbitrary"),
        name="rwkv_prep",
    )(*args)


def _bd_stack(x, m0):
    zero = jnp.zeros_like(x)
    return jnp.concatenate([jnp.where(m0, x, zero), jnp.where(m0, zero, x)], axis=0)


def _scan_chunk(r, v, kk, lw, ka, kd, s_bd, consts, reverse):
    tri_cum, strict2, incl2, eye2, m0_1, m0_2, bd_mask, blk_masks = consts[reverse]
    C = r.shape[0]
    cum = _dot_exact_rhs(tri_cum, lw)
    cum_prev = cum - lw
    e_cum = jnp.exp(cum)
    e_prev = jnp.exp(cum_prev)
    e_neg = jnp.exp(-cum)
    total = cum[0:1, :] if reverse else cum[C - 1:C, :]
    e_rest = jnp.exp(total - cum)
    r0 = r * e_cum
    at0 = -kk * e_prev
    bt = ka * e_neg
    kt = kd * e_neg
    bh = ka * e_rest
    kh = kd * e_rest

    gram = _dot_nt(jnp.concatenate([at0, r0], axis=0),
                   jnp.concatenate([_bd_stack(bt, m0_1), _bd_stack(kt, m0_1)], axis=0))
    zero = jnp.zeros((C, 2 * LANES), F32)
    a_top = jnp.where(strict2, gram[:C], zero)
    a_bot = jnp.where(incl2, gram[C:], zero)
    a_ab = a_top[:, :LANES]
    a_ak = a_top[:, LANES:]

    zero1 = jnp.zeros((C, LANES), F32)
    x = jnp.where(blk_masks[0], a_ab, zero1)
    t = eye2 + x
    x = _dot_hi(x, _bd_stack(x, m0_1))
    xt = _dot_hi(jnp.concatenate([x, t], axis=0), _bd_stack(x, m0_1))
    t = t + xt[C:]
    t = t + _dot_hi(t, _bd_stack(xt[:C], m0_1))
    for off_mask in blk_masks[1:]:
        z = _dot_hi(t, _bd_stack(jnp.where(off_mask, a_ab, zero1), m0_1))
        t = t + _dot_hi(z, _bd_stack(t, m0_1))

    av = _dot(a_ak, _bd_stack(v, m0_1))
    wu = _dot(t, _bd_stack(jnp.concatenate([at0, av], axis=1), m0_2))
    w_a = wu[:, :LANES]
    u_v = wu[:, LANES:]

    wr = _dot_nt(jnp.concatenate([w_a, r0], axis=0), s_bd)
    u = wr[:C] + u_v
    y = wr[C:] + _dot(a_bot, jnp.concatenate([_bd_stack(u, m0_1), _bd_stack(v, m0_1)], axis=0))
    upd = _dot_tn(jnp.concatenate([u, v], axis=0), jnp.concatenate([bh, kh], axis=0))
    s_new = s_bd * jnp.exp(total) + jnp.where(bd_mask, upd, jnp.zeros_like(upd))
    return y, s_new


def _scan_consts(C):
    row = lax.broadcasted_iota(jnp.int32, (C, C), 0)
    colc = lax.broadcasted_iota(jnp.int32, (C, C), 1)
    row2 = lax.broadcasted_iota(jnp.int32, (C, 2 * LANES), 0)
    col2 = lax.broadcasted_iota(jnp.int32, (C, 2 * LANES), 1) % HEAD_DIM
    lane1 = lax.broadcasted_iota(jnp.int32, (C, LANES), 1)
    lane2 = lax.broadcasted_iota(jnp.int32, (C, 2 * LANES), 1) % LANES
    m0_1 = lane1 < HEAD_DIM
    m0_2 = lane2 < HEAD_DIM
    rb = lax.broadcasted_iota(jnp.int32, (LANES, LANES), 0) // HEAD_DIM
    cb = lax.broadcasted_iota(jnp.int32, (LANES, LANES), 1) // HEAD_DIM
    bd_mask = rb == cb
    row1 = lax.broadcasted_iota(jnp.int32, (C, LANES), 0)
    col1 = lane1 % HEAD_DIM
    eye2 = (row1 == col1).astype(F32)
    blk_masks = [row1 // 8 == col1 // 8]
    for b in (8, 16, 32):
        blk_masks.append((row1 // (2 * b) == col1 // (2 * b)) & (row1 // b != col1 // b))
    out = {}
    for reverse in (False, True):
        if reverse:
            tri = (colc >= row).astype(BF16)
            strict2 = col2 > row2
            incl2 = col2 >= row2
        else:
            tri = (colc <= row).astype(BF16)
            strict2 = col2 < row2
            incl2 = col2 <= row2
        out[reverse] = (tri, strict2, incl2, eye2, m0_1, m0_2, bd_mask, blk_masks)
    return out


def _rwkv_scan_kernel(rf, vf, kkf, lwf, kaf, kdf, rb, vb, kkb, lwb, kab, kdb,
                      yf_o, yb_o, state_ref):
    c = pl.program_id(1)

    @pl.when(c == 0)
    def _():
        state_ref[...] = jnp.zeros_like(state_ref)

    C = rf.shape[1]
    consts = _scan_consts(C)
    n_pairs = rf.shape[2] // LANES
    for d, (refs, y_o) in enumerate((((rf, vf, kkf, lwf, kaf, kdf), yf_o),
                                     ((rb, vb, kkb, lwb, kab, kdb), yb_o))):
        for p in range(n_pairs):
            sl = slice(p * LANES, (p + 1) * LANES)
            r, v, kk, lw, ka, kd = (ref[0, :, sl] for ref in refs)
            y, s_new = _scan_chunk(r, v, kk, lw, ka, kd, state_ref[d, p], consts, reverse=(d == 1))
            y_o[0, :, sl] = y
            state_ref[d, p] = s_new


def rwkv_scan(r, v, kk, lwf, lwb, kaf, kab, kdf, kdb):
    B, S, D = r.shape
    C = CHUNK
    nc = S // C
    fwd = pl.BlockSpec((1, C, D), lambda b, c: (b, c, 0))
    bwd = pl.BlockSpec((1, C, D), lambda b, c: (b, nc - 1 - c, 0))
    return pl.pallas_call(
        _rwkv_scan_kernel,
        grid=(B, nc),
        in_specs=[fwd] * 6 + [bwd] * 6,
        out_specs=[fwd, bwd],
        out_shape=[jax.ShapeDtypeStruct((B, S, D), F32)] * 2,
        scratch_shapes=[pltpu.VMEM((2, D // LANES, LANES, LANES), F32)],
        compiler_params=_cparams("parallel", "arbitrary"),
        name="rwkv_scan",
    )(r, v, kk, lwf, kaf, kdf, r, v, kk, lwb, kab, kdb)


def _attn_kernel(table_ref, bucket_ref, q_ref, kp_ref, km_ref, kn_ref, vp_ref, vm_ref, vn_ref,
                 o_ref, lse_ref, bias_ref, *, nblk):
    i = pl.program_id(2)
    TQ = q_ref.shape[1]
    TK = TQ + 2 * ATTN_HALF

    @pl.when((pl.program_id(0) == 0) & (pl.program_id(1) == 0) & (i == 0))
    def _():
        bucket = bucket_ref[...]
        row = lax.broadcasted_iota(jnp.int32, (TQ, TK), 0)
        colk = lax.broadcasted_iota(jnp.int32, (TQ, TK), 1)
        band = jnp.abs(colk - ATTN_HALF - row) <= ATTN_HALF
        for h in range(ATTN_SLOTS):
            acc = jnp.zeros((TQ, TK), F32)
            for b in range(N_BUCKETS):
                acc = jnp.where(bucket == b, table_ref[h * N_BUCKETS + b], acc)
            bias_ref[h] = jnp.where(band, acc, NEG_BIG)

    col = lax.broadcasted_iota(jnp.int32, (TQ, TK), 1)
    edge_ok = ((col >= ATTN_HALF) | (i > 0)) & ((col < TQ + ATTN_HALF) | (i < nblk - 1))
    m0 = lax.broadcasted_iota(jnp.int32, (TQ, LANES), 1) < HEAD_DIM
    for p in range(ATTN_OUT_DIM // LANES):
        sl = slice(p * LANES, (p + 1) * LANES)
        q2 = q_ref[0, :, sl]
        kwin = jnp.concatenate([kp_ref[0, :, sl], km_ref[0, :, sl], kn_ref[0, :, sl]], axis=0)
        vwin = jnp.concatenate([vp_ref[0, :, sl], vm_ref[0, :, sl], vn_ref[0, :, sl]], axis=0)
        outs, lses = [], []
        for hh in range(2):
            qm = jnp.where(m0 if hh == 0 else ~m0, q2, jnp.zeros_like(q2))
            s = lax.dot_general(qm, kwin, (((1,), (1,)), ((), ())), preferred_element_type=F32)
            s = s * (HEAD_DIM ** -0.5) + bias_ref[2 * p + hh]
            s = jnp.where(edge_ok, s, NEG_BIG)
            m = jnp.max(s, axis=-1, keepdims=True)
            e = jnp.exp(s - m)
            den = jnp.sum(e, axis=-1, keepdims=True)
            pv = jnp.dot(e.astype(BF16), vwin, preferred_element_type=F32)
            outs.append(pv / den)
            lses.append(jnp.broadcast_to(m + jnp.log(den), (TQ, LANES)))
        o_ref[0, :, sl] = jnp.where(m0, outs[0], outs[1])
        lse_ref[0, :, sl] = jnp.where(m0, lses[0], lses[1])


def _t5_bucket(rel):
    nb = N_BUCKETS // 2
    max_exact = nb // 2
    ret = jnp.where(rel > 0, nb, 0)
    n = jnp.abs(rel)
    nf = jnp.maximum(n, 1).astype(jnp.float32)
    large = max_exact + (jnp.log(nf / max_exact) / math.log(MAX_DISTANCE / max_exact)
                         * (nb - max_exact)).astype(jnp.int32)
    large = jnp.minimum(large, nb - 1)
    return ret + jnp.where(n < max_exact, n, large)


def dilated_attention(qkv, table, gi, dilation, *, tq):
    B, S, W = qkv.shape
    L = S // dilation
    nblk = L // tq
    nslabs = W // ATTN_OUT_DIM
    view = qkv.reshape(B, L, dilation * W)
    hb = tq // ATTN_HALF
    nhalf = L // ATTN_HALF

    def main(off):
        return pl.BlockSpec((1, tq, ATTN_OUT_DIM), lambda b, r, i: (b, i, r * nslabs + off))

    def prev(off):
        return pl.BlockSpec((1, ATTN_HALF, ATTN_OUT_DIM),
                            lambda b, r, i: (b, jnp.maximum(i * hb - 1, 0), r * nslabs + off))

    def nxt(off):
        return pl.BlockSpec((1, ATTN_HALF, ATTN_OUT_DIM),
                            lambda b, r, i: (b, jnp.minimum((i + 1) * hb, nhalf - 1), r * nslabs + off))

    tk = tq + 2 * ATTN_HALF
    rel = jnp.arange(tk)[None, :] - ATTN_HALF - jnp.arange(tq)[:, None]
    bucket = _t5_bucket(rel * dilation).astype(jnp.int32)
    tbl = table[:, gi * ATTN_SLOTS:(gi + 1) * ATTN_SLOTS].astype(F32).T.reshape(-1)
    ko, vo = 3 + gi, 6 + gi
    out_spec = pl.BlockSpec((1, tq, ATTN_OUT_DIM), lambda b, r, i: (b, i, r))
    o, lse = pl.pallas_call(
        functools.partial(_attn_kernel, nblk=nblk),
        grid=(B, dilation, nblk),
        in_specs=[pl.BlockSpec(memory_space=pltpu.SMEM),
                  pl.BlockSpec((tq, tk), lambda b, r, i: (0, 0)),
                  main(gi), prev(ko), main(ko), nxt(ko), prev(vo), main(vo), nxt(vo)],
        out_specs=[out_spec, out_spec],
        out_shape=[jax.ShapeDtypeStruct((B, L, dilation * ATTN_OUT_DIM), F32)] * 2,
        scratch_shapes=[pltpu.VMEM((ATTN_SLOTS, tq, tk), F32)],
        compiler_params=_cparams("arbitrary", "arbitrary", "arbitrary"),
        name=f"dilated_attn_g{gi}",
    )(tbl, bucket, view, view, view, view, view, view, view)
    return o.reshape(B, S, ATTN_OUT_DIM), lse.reshape(B, S, ATTN_OUT_DIM)


def _branch_post_kernel(yf_ref, yb_ref, bonus_ref, g_ref, gnw_ref, gnb_ref,
                        o1, o2, o3, l1, l2, l3, orw_ref, oat_ref):
    y = yf_ref[...] + yb_ref[...]
    ones_bd = _head_block_ones(2 * LANES)
    width = y.shape[1]

    def head_mean(x):
        return jnp.concatenate(
            [_head_sum(x[:, c:c + 2 * LANES], ones_bd) for c in range(0, width, 2 * LANES)],
            axis=1) * (1.0 / HEAD_DIM)

    mean = head_mean(y)
    yc = y - mean
    var = head_mean(yc * yc)
    yn = yc * lax.rsqrt(var + GN_EPS) * gnw_ref[...] + gnb_ref[...]
    orw_ref[...] = ((yn + bonus_ref[...]) * g_ref[...]).astype(BF16)

    la, lb, lc = l1[...], l2[...], l3[...]
    m = jnp.maximum(jnp.maximum(la, lb), lc)
    ea, eb, ec = jnp.exp(la - m), jnp.exp(lb - m), jnp.exp(lc - m)
    den = ea + eb + ec
    oat_ref[...] = ((ea * o1[...] + eb * o2[...] + ec * o3[...]) / den).astype(BF16)


def branch_post(yf, yb, bonus, g, gn_w, gn_b, outs, lses, *, tm):
    M, D = yf.shape
    A = ATTN_OUT_DIM
    big = pl.BlockSpec((tm, D), lambda i: (i, 0))
    small = pl.BlockSpec((tm, A), lambda i: (i, 0))
    rowp = pl.BlockSpec((1, D), lambda i: (0, 0))
    return pl.pallas_call(
        _branch_post_kernel,
        grid=(M // tm,),
        in_specs=[big] * 4 + [rowp, rowp] + [small] * 6,
        out_specs=[big, small],
        out_shape=[jax.ShapeDtypeStruct((M, D), BF16), jax.ShapeDtypeStruct((M, A), BF16)],
        compiler_params=_cparams("parallel"),
        name="branch_post",
    )(yf, yb, bonus, g, gn_w[None], gn_b[None], *outs, *lses)


def _merge_kernel(orw_ref, oat_ref, wr_ref, wa_ref, gr_ref, ga_ref, o_ref):
    a = jnp.dot(orw_ref[...], wr_ref[...], preferred_element_type=F32)
    b = jnp.dot(oat_ref[...], wa_ref[...], preferred_element_type=F32)
    o_ref[...] = (gr_ref[...] * a + ga_ref[...] * b).astype(o_ref.dtype)


def branch_merge(orw, oat, w_r, w_a, gates, *, tm, tn):
    M = orw.shape[0]
    N = w_r.shape[1]
    nj = N // tn
    return pl.pallas_call(
        _merge_kernel,
        grid=(M // tm, nj),
        in_specs=[pl.BlockSpec((tm, orw.shape[1]), lambda i, j: (i, 0)),
                  pl.BlockSpec((tm, oat.shape[1]), lambda i, j: (i, 0)),
                  pl.BlockSpec((w_r.shape[0], tn), lambda i, j: (0, j)),
                  pl.BlockSpec((w_a.shape[0], tn), lambda i, j: (0, j)),
                  pl.BlockSpec((tm, tn), lambda i, j: (i, j)),
                  pl.BlockSpec((tm, tn), lambda i, j: (i, j + nj))],
        out_specs=pl.BlockSpec((tm, tn), lambda i, j: (i, j)),
        out_shape=jax.ShapeDtypeStruct((M, N), BF16),
        compiler_params=_cparams("parallel", "arbitrary"),
        name="branch_merge",
    )(orw, oat, w_r, w_a, gates, gates)


def _lora_weights(w_up, a_up):
    z = jnp.zeros((DECAY_LORA, RWKV_DIM), F32)
    rows = [jnp.concatenate([w_up[0], z, z, z], axis=1),
            jnp.concatenate([z, w_up[1], z, z], axis=1),
            jnp.concatenate([z, z, a_up[0], z], axis=1),
            jnp.concatenate([z, z, z, a_up[1]], axis=1)]
    return jnp.concatenate(rows, axis=0).astype(BF16)


def kernel(x, norm1_g, w_in, tshift_mu, w0, w_lora_up, a0, a_lora_up, g_lora_up, k_k, k_a, r_k,
           gn_w, gn_b, rel_bias, w_branch_rwkv, w_branch_attn, w_out, norm2_g, w_mlp_in, w_mlp_out,
           final_g):
    B, S, D = x.shape
    M = B * S
    depth = w_in.shape[0]
    c_rkv = 3 * RWKV_DIM
    c_slab = c_rkv + LORA_COLS
    c_attn = c_slab + 3 * ATTN_DIM
    h = x.reshape(M, D)
    for l in range(depth):
        wl = w_in[l]
        g1 = norm1_g[l]
        rkv = norm_matmul(h, g1, wl[:, :c_rkv].astype(BF16), tm=1024, tn=1024)
        lora = norm_matmul(h, g1, wl[:, c_rkv:c_slab].astype(BF16), tm=1024, tn=LORA_COLS)
        qkv = norm_matmul(h, g1, wl[:, c_slab:c_attn].astype(BF16), tm=1024, tn=ATTN_DIM, out_dtype=BF16)
        gates = norm_matmul(h, g1, wl[:, c_attn:].astype(BF16), tm=1024, tn=1024, act="sigmoid")

        (r, v, kk, lwf, lwb, kaf, kab, kdf, kdb, g, bonus) = rwkv_prep(
            rkv.reshape(B, S, c_rkv), lora.reshape(B, S, LORA_COLS), tshift_mu[l], w0[l], a0[l],
            _lora_weights(w_lora_up[l], a_lora_up[l]), g_lora_up[l].astype(BF16),
            k_k[l], k_a[l], r_k[l], ts=256)
        yf, yb = rwkv_scan(r, v, kk, lwf, lwb, kaf, kab, kdf, kdb)

        qkv3 = qkv.reshape(B, S, 3 * ATTN_DIM)
        outs, lses = [], []
        for gi, (window, dilation) in enumerate(ATTN_GROUPS):
            assert window // (2 * dilation) == ATTN_HALF
            o, lse = dilated_attention(qkv3, rel_bias, gi, dilation, tq=128)
            outs.append(o.reshape(M, ATTN_OUT_DIM))
            lses.append(lse.reshape(M, ATTN_OUT_DIM))

        orw, oat = branch_post(yf.reshape(M, RWKV_DIM), yb.reshape(M, RWKV_DIM),
                               bonus.reshape(M, RWKV_DIM), g.reshape(M, RWKV_DIM),
                               gn_w[l], gn_b[l], outs, lses, tm=512)
        merged = branch_merge(orw, oat, w_branch_rwkv[l].astype(BF16), w_branch_attn[l].astype(BF16),
                              gates, tm=1024, tn=1024)
        h = matmul_residual(merged, w_out[l].astype(BF16), h, tm=1024, tn=1024, tk=D)

        act = norm_matmul(h, norm2_g[l], w_mlp_in[l].astype(BF16), tm=1024, tn=1024, act="relu2",
                          out_dtype=BF16)
        h = matmul_residual(act, w_mlp_out[l].astype(BF16), h, tm=1024, tn=1024, tk=2048)
    out = rmsnorm_rows(h, final_g, tm=512)
    return out.reshape(B, S, D)
```

```python
import functools
import math

import numpy as np
import jax
import jax.numpy as jnp
from jax import lax
from jax.experimental import pallas as pl
from jax.experimental.pallas import tpu as pltpu

F32 = jnp.float32
BF16 = jnp.bfloat16

HEAD_DIM = 64
LANES = 128
RWKV_DIM = 1024
DECAY_LORA = 96
AAA_LORA = 96
GATE_LORA = 256
LORA_COLS = GATE_LORA + 2 * DECAY_LORA + 2 * AAA_LORA
LORA_TAIL = LORA_COLS - GATE_LORA
ATTN_GROUPS = ((128, 1), (512, 4), (2048, 16))
ATTN_SLOTS = 8
ATTN_DIM = 1536
ATTN_OUT_DIM = ATTN_SLOTS * HEAD_DIM
ATTN_HALF = 64
N_BUCKETS = 32
MAX_DISTANCE = 1024
RMS_EPS = 1e-6
GN_EPS = 64e-5
L2_EPS = 1e-12
NEG_BIG = -1e30
CHUNK = 64
VMEM_LIMIT = 56 * 1024 * 1024


def _cparams(*sem):
    return pltpu.CompilerParams(dimension_semantics=sem, vmem_limit_bytes=VMEM_LIMIT)


def _dot(a, b):
    return jnp.dot(a.astype(BF16), b.astype(BF16), preferred_element_type=F32)


def _dot_nt(a, b):
    return lax.dot_general(a.astype(BF16), b.astype(BF16), (((1,), (1,)), ((), ())),
                           preferred_element_type=F32)


def _dot_tn(a, b):
    return lax.dot_general(a.astype(BF16), b.astype(BF16), (((0,), (0,)), ((), ())),
                           preferred_element_type=F32)


def _dot_hi(a, b):
    a0 = a.astype(BF16)
    a1 = (a - a0.astype(F32)).astype(BF16)
    b0 = b.astype(BF16)
    b1 = (b - b0.astype(F32)).astype(BF16)
    return (jnp.dot(a0, b0, preferred_element_type=F32) + jnp.dot(a0, b1, preferred_element_type=F32)
            + jnp.dot(a1, b0, preferred_element_type=F32))


def _split3(x):
    p0 = x.astype(BF16)
    r1 = x - p0.astype(F32)
    p1 = r1.astype(BF16)
    p2 = (r1 - p1.astype(F32)).astype(BF16)
    return p0, p1, p2


def _dot_exact_rhs(a_bf16, x):
    return sum(jnp.dot(a_bf16, p, preferred_element_type=F32) for p in _split3(x))


def _head_sum(x, ones_bd):
    return sum(jnp.dot(p, ones_bd, preferred_element_type=F32) for p in _split3(x))


def _head_block_ones(width):
    r = lax.broadcasted_iota(jnp.int32, (width, width), 0) // HEAD_DIM
    c = lax.broadcasted_iota(jnp.int32, (width, width), 1) // HEAD_DIM
    return (r == c).astype(BF16)


def _norm_mm_kernel(x_ref, g_ref, w_ref, o_ref, u_ref, *, act):
    @pl.when(pl.program_id(1) == 0)
    def _():
        x = x_ref[...]
        ms = jnp.mean(x * x, axis=-1, keepdims=True)
        u_ref[...] = (x * lax.rsqrt(ms + RMS_EPS) * g_ref[...]).astype(BF16)

    acc = jnp.dot(u_ref[...], w_ref[...], preferred_element_type=F32)
    if act == "relu2":
        acc = jnp.square(jnp.maximum(acc, 0.0))
    elif act == "sigmoid":
        acc = jax.nn.sigmoid(acc)
    o_ref[...] = acc.astype(o_ref.dtype)


def norm_matmul(x, g, w, *, tm, tn, act=None, out_dtype=F32):
    M, K = x.shape
    N = w.shape[1]
    return pl.pallas_call(
        functools.partial(_norm_mm_kernel, act=act),
        grid=(M // tm, N // tn),
        in_specs=[pl.BlockSpec((tm, K), lambda i, j: (i, 0)),
                  pl.BlockSpec((1, K), lambda i, j: (0, 0)),
                  pl.BlockSpec((K, tn), lambda i, j: (0, j))],
        out_specs=pl.BlockSpec((tm, tn), lambda i, j: (i, j)),
        out_shape=jax.ShapeDtypeStruct((M, N), out_dtype),
        scratch_shapes=[pltpu.VMEM((tm, K), BF16)],
        compiler_params=_cparams("parallel", "arbitrary"),
        name="norm_matmul_" + (act or "id"),
    )(x, g.reshape(1, K), w)


def _mm_res_kernel(x_ref, w_ref, r_ref, o_ref, acc_ref):
    k = pl.program_id(2)

    @pl.when(k == 0)
    def _():
        acc_ref[...] = jnp.zeros_like(acc_ref)

    acc_ref[...] += jnp.dot(x_ref[...], w_ref[...], preferred_element_type=F32)

    @pl.when(k == pl.num_programs(2) - 1)
    def _():
        o_ref[...] = r_ref[...] + acc_ref[...]


def matmul_residual(x, w, res, *, tm, tn, tk):
    M, K = x.shape
    N = w.shape[1]
    return pl.pallas_call(
        _mm_res_kernel,
        grid=(M // tm, N // tn, K // tk),
        in_specs=[pl.BlockSpec((tm, tk), lambda i, j, k: (i, k)),
                  pl.BlockSpec((tk, tn), lambda i, j, k: (k, j)),
                  pl.BlockSpec((tm, tn), lambda i, j, k: (i, j))],
        out_specs=pl.BlockSpec((tm, tn), lambda i, j, k: (i, j)),
        out_shape=jax.ShapeDtypeStruct((M, N), F32),
        scratch_shapes=[pltpu.VMEM((tm, tn), F32)],
        compiler_params=_cparams("parallel", "parallel", "arbitrary"),
        name="matmul_residual",
    )(x, w, res)


def _rmsnorm_kernel(x_ref, g_ref, o_ref):
    x = x_ref[...]
    ms = jnp.mean(x * x, axis=-1, keepdims=True)
    o_ref[...] = x * lax.rsqrt(ms + RMS_EPS) * g_ref[...]


def rmsnorm_rows(x, g, *, tm):
    M, K = x.shape
    return pl.pallas_call(
        _rmsnorm_kernel,
        grid=(M // tm,),
        in_specs=[pl.BlockSpec((tm, K), lambda i: (i, 0)),
                  pl.BlockSpec((1, K), lambda i: (0, 0))],
        out_specs=pl.BlockSpec((tm, K), lambda i: (i, 0)),
        out_shape=jax.ShapeDtypeStruct((M, K), F32),
        compiler_params=_cparams("parallel"),
        name="final_rmsnorm",
    )(x, g.reshape(1, K))


def _token_shift(main, prev_blk, next_blk, mu, first, last):
    rows = main.shape[0]
    prev_row = jnp.where(first, 0.0, prev_blk[7:8, :])
    next_row = jnp.where(last, 0.0, next_blk[0:1, :])
    rid = lax.broadcasted_iota(jnp.int32, main.shape, 0)
    up = jnp.where(rid == 0, prev_row, pltpu.roll(main, 1, 0))
    dn = jnp.where(rid == rows - 1, next_row, pltpu.roll(main, rows - 1, 0))
    return main + mu * (0.5 * (up + dn) - main)


def _rwkv_prep_kernel(r_m, r_p, r_n, k_m, k_p, k_n, v_m, v_p, v_n, l_m, l_p, l_n,
                      mu_r, mu_k, mu_v, mu_l, w0_ref, a0_ref, lora_w_ref, g_up_ref,
                      kk_ref, ka_ref, rk_ref,
                      r_o, v_o, kk_o, lwf_o, lwb_o, kaf_o, kab_o, kdf_o, kdb_o, g_o, bonus_o):
    i = pl.program_id(1)
    first = i == 0
    last = i == pl.num_programs(1) - 1
    r = _token_shift(r_m[0], r_p[0], r_n[0], mu_r[...], first, last)
    k = _token_shift(k_m[0], k_p[0], k_n[0], mu_k[...], first, last)
    v = _token_shift(v_m[0], v_p[0], v_n[0], mu_v[...], first, last)
    lo = _token_shift(l_m[0], l_p[0], l_n[0], mu_l[...], first, last)

    g = _dot(jax.nn.sigmoid(lo[:, :GATE_LORA]), g_up_ref[...])
    tail = lo[:, GATE_LORA:]
    col = lax.broadcasted_iota(jnp.int32, tail.shape, 1)
    tail = jnp.where(col < 2 * DECAY_LORA, jnp.tanh(tail), tail)
    up = _dot(tail, lora_w_ref[...])

    ones_bd = _head_block_ones(2 * LANES)

    def head_sum(x):
        return jnp.concatenate(
            [_head_sum(x[:, c:c + 2 * LANES], ones_bd) for c in range(0, RWKV_DIM, 2 * LANES)], axis=1)

    kk = k * kk_ref[...]
    kk = kk / jnp.maximum(jnp.sqrt(head_sum(kk * kk)), L2_EPS)

    r_o[0] = r
    v_o[0] = v
    kk_o[0] = kk
    g_o[0] = g
    kd_sum = jnp.zeros_like(k)
    for d, (lw_o, ka_o, kd_o) in enumerate(((lwf_o, kaf_o, kdf_o), (lwb_o, kab_o, kdb_o))):
        z = w0_ref[d:d + 1, :] + up[:, d * RWKV_DIM:(d + 1) * RWKV_DIM]
        nz = -z
        softplus = jnp.maximum(nz, 0.0) + jnp.log1p(jnp.exp(-jnp.abs(nz)))
        lw_o[0] = -jnp.exp(-softplus - 0.5)
        a = jax.nn.sigmoid(a0_ref[d:d + 1, :] + up[:, (2 + d) * RWKV_DIM:(3 + d) * RWKV_DIM])
        kd = k * (1.0 + (a - 1.0) * ka_ref[...])
        ka_o[0] = kk * a
        kd_o[0] = kd
        kd_sum = kd_sum + kd
    bonus_o[0] = head_sum(r * kd_sum * rk_ref[...]) * v


def rwkv_prep(rkv, lora, mu, w0, a0, lora_w, g_up, k_k, k_a, r_k, *, ts):
    B, S, _ = rkv.shape
    D = RWKV_DIM
    nblk8 = S // 8

    def main_spec(width, cblk):
        return pl.BlockSpec((1, ts, width), lambda b, i: (b, i, cblk))

    def prev_spec(width, cblk):
        return pl.BlockSpec((1, 8, width), lambda b, i: (b, jnp.maximum(i * (ts // 8) - 1, 0), cblk))

    def next_spec(width, cblk):
        return pl.BlockSpec((1, 8, width), lambda b, i: (b, jnp.minimum((i + 1) * (ts // 8), nblk8 - 1), cblk))

    def row_spec(width):
        return pl.BlockSpec((1, width), lambda b, i: (0, 0))

    def full_spec(shape):
        return pl.BlockSpec(shape, lambda b, i: (0,) * len(shape))

    in_specs = []
    args = []
    for cblk in range(3):
        in_specs += [main_spec(D, cblk), prev_spec(D, cblk), next_spec(D, cblk)]
        args += [rkv, rkv, rkv]
    in_specs += [main_spec(LORA_COLS, 0), prev_spec(LORA_COLS, 0), next_spec(LORA_COLS, 0)]
    args += [lora, lora, lora]
    mu_r, mu_k, mu_v, mu_l = (mu[None, 0:D], mu[None, D:2 * D], mu[None, 2 * D:3 * D], mu[None, 3 * D:])
    in_specs += [row_spec(D), row_spec(D), row_spec(D), row_spec(LORA_COLS),
                 full_spec((2, D)), full_spec((2, D)), full_spec((LORA_TAIL, 4 * D)),
                 full_spec((GATE_LORA, D)), row_spec(D), row_spec(D), row_spec(D)]
    args += [mu_r, mu_k, mu_v, mu_l, w0, a0, lora_w, g_up, k_k[None], k_a[None], r_k.reshape(1, D)]
    out_spec = pl.BlockSpec((1, ts, D), lambda b, i: (b, i, 0))
    n_out = 11
    return pl.pallas_call(
        _rwkv_prep_kernel,
        grid=(B, S // ts),
        in_specs=in_specs,
        out_specs=[out_spec] * n_out,
        out_shape=[jax.ShapeDtypeStruct((B, S, D), F32)] * n_out,
        compiler_params=_cparams("parallel", "arbitrary"),
        name="rwkv_prep",
    )(*args)


INV_BASE = 8
INV_BASE_PASSES = 3
INV_MERGE_PASSES = 1

_BMM = (((2,), (1,)), ((0,), (0,)))
_BMM_NT = (((2,), (2,)), ((0,), (0,)))
_BMM_TN = (((1,), (1,)), ((0,), (0,)))


def _bdot(a, b, dims=_BMM, passes=1):
    a0 = a.astype(BF16)
    b0 = b.astype(BF16)
    out = lax.dot_general(a0, b0, dims, preferred_element_type=F32)
    if passes == 3:
        a1 = (a - a0.astype(F32)).astype(BF16)
        b1 = (b - b0.astype(F32)).astype(BF16)
        out = (out + lax.dot_general(a0, b1, dims, preferred_element_type=F32)
               + lax.dot_general(a1, b0, dims, preferred_element_type=F32))
    return out


def _bd_stack(x, m0):
    zero = jnp.zeros_like(x)
    return jnp.concatenate([jnp.where(m0, x, zero), jnp.where(m0, zero, x)], axis=-2)


def _scan_chunk(r, v, kk, lw, ka, kd, s_bd, consts):
    tri, is_rev, strict2, incl2, eye2, m0_1, m0_2, bd_mask, blk_masks = consts
    C = r.shape[1]
    cum = sum(lax.dot_general(tri, p, _BMM, preferred_element_type=F32) for p in _split3(lw))
    cum_prev = cum - lw
    e_cum = jnp.exp(cum)
    e_prev = jnp.exp(cum_prev)
    e_neg = jnp.exp(-cum)
    total = jnp.where(is_rev, cum[:, 0:1, :], cum[:, C - 1:C, :])
    e_rest = jnp.exp(total - cum)
    r0 = r * e_cum
    at0 = -kk * e_prev
    bt = ka * e_neg
    kt = kd * e_neg
    bh = ka * e_rest
    kh = kd * e_rest

    gram = _bdot(jnp.concatenate([at0, r0], axis=1),
                 jnp.concatenate([_bd_stack(bt, m0_1), _bd_stack(kt, m0_1)], axis=1), _BMM_NT)
    zero = jnp.zeros_like(gram[:, :C])
    a_top = jnp.where(strict2, gram[:, :C], zero)
    a_bot = jnp.where(incl2, gram[:, C:], zero)
    a_ab = a_top[:, :, :LANES]
    a_ak = a_top[:, :, LANES:]

    zero1 = jnp.zeros_like(a_ab)
    x = jnp.where(blk_masks[0], a_ab, zero1)
    t = eye2 + x
    size = 2
    while size < INV_BASE:
        x = _bdot(x, _bd_stack(x, m0_1), passes=INV_BASE_PASSES)
        t = t + _bdot(t, _bd_stack(x, m0_1), passes=INV_BASE_PASSES)
        size *= 2
    for off_mask in blk_masks[1:]:
        z = _bdot(t, _bd_stack(jnp.where(off_mask, a_ab, zero1), m0_1), passes=INV_MERGE_PASSES)
        t = t + _bdot(z, _bd_stack(t, m0_1), passes=INV_MERGE_PASSES)

    av = _bdot(a_ak, _bd_stack(v, m0_1))
    wu = _bdot(t, _bd_stack(jnp.concatenate([at0, av], axis=2), m0_2))
    w_a = wu[:, :, :LANES]
    u_v = wu[:, :, LANES:]

    wr = _bdot(jnp.concatenate([w_a, r0], axis=1), s_bd, _BMM_NT)
    u = wr[:, :C] + u_v
    y = wr[:, C:] + _bdot(a_bot, jnp.concatenate([_bd_stack(u, m0_1), _bd_stack(v, m0_1)], axis=1))
    upd = _bdot(jnp.concatenate([u, v], axis=1), jnp.concatenate([bh, kh], axis=1), _BMM_TN)
    s_new = s_bd * jnp.exp(total) + jnp.where(bd_mask, upd, jnp.zeros_like(upd))
    return y, s_new


def _scan_consts(C, n_fwd, n_bwd):
    G = n_fwd + n_bwd
    is_rev = lax.broadcasted_iota(jnp.int32, (G, 1, 1), 0) >= n_fwd
    row = lax.broadcasted_iota(jnp.int32, (G, C, C), 1)
    colc = lax.broadcasted_iota(jnp.int32, (G, C, C), 2)
    sign = jnp.where(is_rev, -1, 1)
    tri = ((colc - row) * sign <= 0).astype(F32).astype(BF16)
    row2 = lax.broadcasted_iota(jnp.int32, (G, C, 2 * LANES), 1)
    col2 = lax.broadcasted_iota(jnp.int32, (G, C, 2 * LANES), 2) % HEAD_DIM
    strict2 = (col2 - row2) * sign < 0
    incl2 = (col2 - row2) * sign <= 0
    lane1 = lax.broadcasted_iota(jnp.int32, (1, C, LANES), 2)
    lane2 = lax.broadcasted_iota(jnp.int32, (1, C, 2 * LANES), 2) % LANES
    m0_1 = lane1 < HEAD_DIM
    m0_2 = lane2 < HEAD_DIM
    rb = lax.broadcasted_iota(jnp.int32, (1, LANES, LANES), 1) // HEAD_DIM
    cb = lax.broadcasted_iota(jnp.int32, (1, LANES, LANES), 2) // HEAD_DIM
    bd_mask = rb == cb
    row1 = lax.broadcasted_iota(jnp.int32, (1, C, LANES), 1)
    col1 = lane1 % HEAD_DIM
    eye2 = (row1 == col1).astype(F32)
    blk_masks = [row1 // INV_BASE == col1 // INV_BASE]
    b = INV_BASE
    while b < C:
        blk_masks.append((row1 // (2 * b) == col1 // (2 * b)) & (row1 // b != col1 // b))
        b *= 2
    return (tri, is_rev, strict2, incl2, eye2, m0_1, m0_2, bd_mask, blk_masks)


def _rwkv_scan_kernel(rf, vf, kkf, lwf, kaf, kdf, rb, vb, kkb, lwb, kab, kdb,
                      yf_o, yb_o, state_ref):
    c = pl.program_id(1)

    @pl.when(c == 0)
    def _():
        state_ref[...] = jnp.zeros_like(state_ref)

    C = rf.shape[1]
    n_pairs = rf.shape[2] // LANES
    consts = _scan_consts(C, n_pairs, n_pairs)

    def gather(ref_f, ref_b):
        return jnp.stack([ref[0, :, p * LANES:(p + 1) * LANES]
                          for ref in (ref_f, ref_b) for p in range(n_pairs)], axis=0)

    r, v, kk, lw, ka, kd = (gather(f, b) for f, b in
                            ((rf, rb), (vf, vb), (kkf, kkb), (lwf, lwb), (kaf, kab), (kdf, kdb)))
    y, s_new = _scan_chunk(r, v, kk, lw, ka, kd, state_ref[...], consts)
    state_ref[...] = s_new
    for d, y_o in enumerate((yf_o, yb_o)):
        for p in range(n_pairs):
            y_o[0, :, p * LANES:(p + 1) * LANES] = y[d * n_pairs + p]


def rwkv_scan(r, v, kk, lwf, lwb, kaf, kab, kdf, kdb):
    B, S, D = r.shape
    C = CHUNK
    nc = S // C
    fwd = pl.BlockSpec((1, C, D), lambda b, c: (b, c, 0))
    bwd = pl.BlockSpec((1, C, D), lambda b, c: (b, nc - 1 - c, 0))
    return pl.pallas_call(
        _rwkv_scan_kernel,
        grid=(B, nc),
        in_specs=[fwd] * 6 + [bwd] * 6,
        out_specs=[fwd, bwd],
        out_shape=[jax.ShapeDtypeStruct((B, S, D), F32)] * 2,
        scratch_shapes=[pltpu.VMEM((2 * (D // LANES), LANES, LANES), F32)],
        compiler_params=_cparams("parallel", "arbitrary"),
        name="rwkv_scan",
    )(r, v, kk, lwf, kaf, kdf, r, v, kk, lwb, kab, kdb)


def _attn_kernel(table_ref, bucket_ref, q_ref, kp_ref, km_ref, kn_ref, vp_ref, vm_ref, vn_ref,
                 o_ref, lse_ref, bias_ref, *, nblk):
    i = pl.program_id(2)
    TQ = q_ref.shape[1]
    TK = TQ + 2 * ATTN_HALF

    @pl.when((pl.program_id(0) == 0) & (pl.program_id(1) == 0) & (i == 0))
    def _():
        bucket = bucket_ref[...]
        row = lax.broadcasted_iota(jnp.int32, (TQ, TK), 0)
        colk = lax.broadcasted_iota(jnp.int32, (TQ, TK), 1)
        band = jnp.abs(colk - ATTN_HALF - row) <= ATTN_HALF
        for h in range(ATTN_SLOTS):
            acc = jnp.zeros((TQ, TK), F32)
            for b in range(N_BUCKETS):
                acc = jnp.where(bucket == b, table_ref[h * N_BUCKETS + b], acc)
            bias_ref[h] = jnp.where(band, acc, NEG_BIG)

    col = lax.broadcasted_iota(jnp.int32, (TQ, TK), 1)
    edge_ok = ((col >= ATTN_HALF) | (i > 0)) & ((col < TQ + ATTN_HALF) | (i < nblk - 1))
    m0 = lax.broadcasted_iota(jnp.int32, (TQ, LANES), 1) < HEAD_DIM
    for p in range(ATTN_OUT_DIM // LANES):
        sl = slice(p * LANES, (p + 1) * LANES)
        q2 = q_ref[0, :, sl]
        kwin = jnp.concatenate([kp_ref[0, :, sl], km_ref[0, :, sl], kn_ref[0, :, sl]], axis=0)
        vwin = jnp.concatenate([vp_ref[0, :, sl], vm_ref[0, :, sl], vn_ref[0, :, sl]], axis=0)
        outs, lses = [], []
        for hh in range(2):
            qm = jnp.where(m0 if hh == 0 else ~m0, q2, jnp.zeros_like(q2))
            s = lax.dot_general(qm, kwin, (((1,), (1,)), ((), ())), preferred_element_type=F32)
            s = s * (HEAD_DIM ** -0.5) + bias_ref[2 * p + hh]
            s = jnp.where(edge_ok, s, NEG_BIG)
            m = jnp.max(s, axis=-1, keepdims=True)
            e = jnp.exp(s - m)
            den = jnp.sum(e, axis=-1, keepdims=True)
            pv = jnp.dot(e.astype(BF16), vwin, preferred_element_type=F32)
            outs.append(pv / den)
            lses.append(jnp.broadcast_to(m + jnp.log(den), (TQ, LANES)))
        o_ref[0, :, sl] = jnp.where(m0, outs[0], outs[1])
        lse_ref[0, :, sl] = jnp.where(m0, lses[0], lses[1])


def _t5_bucket(rel):
    nb = N_BUCKETS // 2
    max_exact = nb // 2
    ret = jnp.where(rel > 0, nb, 0)
    n = jnp.abs(rel)
    nf = jnp.maximum(n, 1).astype(jnp.float32)
    large = max_exact + (jnp.log(nf / max_exact) / math.log(MAX_DISTANCE / max_exact)
                         * (nb - max_exact)).astype(jnp.int32)
    large = jnp.minimum(large, nb - 1)
    return ret + jnp.where(n < max_exact, n, large)


def dilated_attention(qkv, table, gi, dilation, *, tq):
    B, S, W = qkv.shape
    L = S // dilation
    nblk = L // tq
    nslabs = W // ATTN_OUT_DIM
    view = qkv.reshape(B, L, dilation * W)
    hb = tq // ATTN_HALF
    nhalf = L // ATTN_HALF

    def main(off):
        return pl.BlockSpec((1, tq, ATTN_OUT_DIM), lambda b, r, i: (b, i, r * nslabs + off))

    def prev(off):
        return pl.BlockSpec((1, ATTN_HALF, ATTN_OUT_DIM),
                            lambda b, r, i: (b, jnp.maximum(i * hb - 1, 0), r * nslabs + off))

    def nxt(off):
        return pl.BlockSpec((1, ATTN_HALF, ATTN_OUT_DIM),
                            lambda b, r, i: (b, jnp.minimum((i + 1) * hb, nhalf - 1), r * nslabs + off))

    tk = tq + 2 * ATTN_HALF
    rel = jnp.arange(tk)[None, :] - ATTN_HALF - jnp.arange(tq)[:, None]
    bucket = _t5_bucket(rel * dilation).astype(jnp.int32)
    tbl = table[:, gi * ATTN_SLOTS:(gi + 1) * ATTN_SLOTS].astype(F32).T.reshape(-1)
    ko, vo = 3 + gi, 6 + gi
    out_spec = pl.BlockSpec((1, tq, ATTN_OUT_DIM), lambda b, r, i: (b, i, r))
    o, lse = pl.pallas_call(
        functools.partial(_attn_kernel, nblk=nblk),
        grid=(B, dilation, nblk),
        in_specs=[pl.BlockSpec(memory_space=pltpu.SMEM),
                  pl.BlockSpec((tq, tk), lambda b, r, i: (0, 0)),
                  main(gi), prev(ko), main(ko), nxt(ko), prev(vo), main(vo), nxt(vo)],
        out_specs=[out_spec, out_spec],
        out_shape=[jax.ShapeDtypeStruct((B, L, dilation * ATTN_OUT_DIM), F32)] * 2,
        scratch_shapes=[pltpu.VMEM((ATTN_SLOTS, tq, tk), F32)],
        compiler_params=_cparams("arbitrary", "arbitrary", "arbitrary"),
        name=f"dilated_attn_g{gi}",
    )(tbl, bucket, view, view, view, view, view, view, view)
    return o.reshape(B, S, ATTN_OUT_DIM), lse.reshape(B, S, ATTN_OUT_DIM)


def _branch_post_kernel(yf_ref, yb_ref, bonus_ref, g_ref, gnw_ref, gnb_ref,
                        o1, o2, o3, l1, l2, l3, orw_ref, oat_ref):
    y = yf_ref[...] + yb_ref[...]
    ones_bd = _head_block_ones(2 * LANES)
    width = y.shape[1]

    def head_mean(x):
        return jnp.concatenate(
            [_head_sum(x[:, c:c + 2 * LANES], ones_bd) for c in range(0, width, 2 * LANES)],
            axis=1) * (1.0 / HEAD_DIM)

    mean = head_mean(y)
    yc = y - mean
    var = head_mean(yc * yc)
    yn = yc * lax.rsqrt(var + GN_EPS) * gnw_ref[...] + gnb_ref[...]
    orw_ref[...] = ((yn + bonus_ref[...]) * g_ref[...]).astype(BF16)

    la, lb, lc = l1[...], l2[...], l3[...]
    m = jnp.maximum(jnp.maximum(la, lb), lc)
    ea, eb, ec = jnp.exp(la - m), jnp.exp(lb - m), jnp.exp(lc - m)
    den = ea + eb + ec
    oat_ref[...] = ((ea * o1[...] + eb * o2[...] + ec * o3[...]) / den).astype(BF16)


def branch_post(yf, yb, bonus, g, gn_w, gn_b, outs, lses, *, tm):
    M, D = yf.shape
    A = ATTN_OUT_DIM
    big = pl.BlockSpec((tm, D), lambda i: (i, 0))
    small = pl.BlockSpec((tm, A), lambda i: (i, 0))
    rowp = pl.BlockSpec((1, D), lambda i: (0, 0))
    return pl.pallas_call(
        _branch_post_kernel,
        grid=(M // tm,),
        in_specs=[big] * 4 + [rowp, rowp] + [small] * 6,
        out_specs=[big, small],
        out_shape=[jax.ShapeDtypeStruct((M, D), BF16), jax.ShapeDtypeStruct((M, A), BF16)],
        compiler_params=_cparams("parallel"),
        name="branch_post",
    )(yf, yb, bonus, g, gn_w[None], gn_b[None], *outs, *lses)


def _merge_kernel(orw_ref, oat_ref, wr_ref, wa_ref, gr_ref, ga_ref, o_ref):
    a = jnp.dot(orw_ref[...], wr_ref[...], preferred_element_type=F32)
    b = jnp.dot(oat_ref[...], wa_ref[...], preferred_element_type=F32)
    o_ref[...] = (gr_ref[...] * a + ga_ref[...] * b).astype(o_ref.dtype)


def branch_merge(orw, oat, w_r, w_a, gates, *, tm, tn):
    M = orw.shape[0]
    N = w_r.shape[1]
    nj = N // tn
    return pl.pallas_call(
        _merge_kernel,
        grid=(M // tm, nj),
        in_specs=[pl.BlockSpec((tm, orw.shape[1]), lambda i, j: (i, 0)),
                  pl.BlockSpec((tm, oat.shape[1]), lambda i, j: (i, 0)),
                  pl.BlockSpec((w_r.shape[0], tn), lambda i, j: (0, j)),
                  pl.BlockSpec((w_a.shape[0], tn), lambda i, j: (0, j)),
                  pl.BlockSpec((tm, tn), lambda i, j: (i, j)),
                  pl.BlockSpec((tm, tn), lambda i, j: (i, j + nj))],
        out_specs=pl.BlockSpec((tm, tn), lambda i, j: (i, j)),
        out_shape=jax.ShapeDtypeStruct((M, N), BF16),
        compiler_params=_cparams("parallel", "arbitrary"),
        name="branch_merge",
    )(orw, oat, w_r, w_a, gates, gates)


def _lora_weights(w_up, a_up):
    z = jnp.zeros((DECAY_LORA, RWKV_DIM), F32)
    rows = [jnp.concatenate([w_up[0], z, z, z], axis=1),
            jnp.concatenate([z, w_up[1], z, z], axis=1),
            jnp.concatenate([z, z, a_up[0], z], axis=1),
            jnp.concatenate([z, z, z, a_up[1]], axis=1)]
    return jnp.concatenate(rows, axis=0).astype(BF16)


def kernel(x, norm1_g, w_in, tshift_mu, w0, w_lora_up, a0, a_lora_up, g_lora_up, k_k, k_a, r_k,
           gn_w, gn_b, rel_bias, w_branch_rwkv, w_branch_attn, w_out, norm2_g, w_mlp_in, w_mlp_out,
           final_g):
    B, S, D = x.shape
    M = B * S
    depth = w_in.shape[0]
    c_rkv = 3 * RWKV_DIM
    c_slab = c_rkv + LORA_COLS
    c_attn = c_slab + 3 * ATTN_DIM
    h = x.reshape(M, D)
    for l in range(depth):
        wl = w_in[l]
        g1 = norm1_g[l]
        rkv = norm_matmul(h, g1, wl[:, :c_rkv].astype(BF16), tm=1024, tn=1024)
        lora = norm_matmul(h, g1, wl[:, c_rkv:c_slab].astype(BF16), tm=1024, tn=LORA_COLS)
        qkv = norm_matmul(h, g1, wl[:, c_slab:c_attn].astype(BF16), tm=1024, tn=ATTN_DIM, out_dtype=BF16)
        gates = norm_matmul(h, g1, wl[:, c_attn:].astype(BF16), tm=1024, tn=1024, act="sigmoid")

        (r, v, kk, lwf, lwb, kaf, kab, kdf, kdb, g, bonus) = rwkv_prep(
            rkv.reshape(B, S, c_rkv), lora.reshape(B, S, LORA_COLS), tshift_mu[l], w0[l], a0[l],
            _lora_weights(w_lora_up[l], a_lora_up[l]), g_lora_up[l].astype(BF16),
            k_k[l], k_a[l], r_k[l], ts=256)
        yf, yb = rwkv_scan(r, v, kk, lwf, lwb, kaf, kab, kdf, kdb)

        qkv3 = qkv.reshape(B, S, 3 * ATTN_DIM)
        outs, lses = [], []
        for gi, (window, dilation) in enumerate(ATTN_GROUPS):
            assert window // (2 * dilation) == ATTN_HALF
            o, lse = dilated_attention(qkv3, rel_bias, gi, dilation, tq=128)
            outs.append(o.reshape(M, ATTN_OUT_DIM))
            lses.append(lse.reshape(M, ATTN_OUT_DIM))

        orw, oat = branch_post(yf.reshape(M, RWKV_DIM), yb.reshape(M, RWKV_DIM),
                               bonus.reshape(M, RWKV_DIM), g.reshape(M, RWKV_DIM),
                               gn_w[l], gn_b[l], outs, lses, tm=512)
        merged = branch_merge(orw, oat, w_branch_rwkv[l].astype(BF16), w_branch_attn[l].astype(BF16),
                              gates, tm=1024, tn=1024)
        h = matmul_residual(merged, w_out[l].astype(BF16), h, tm=1024, tn=1024, tk=D)

        act = norm_matmul(h, norm2_g[l], w_mlp_in[l].astype(BF16), tm=1024, tn=1024, act="relu2",
                          out_dtype=BF16)
        h = matmul_residual(act, w_mlp_out[l].astype(BF16), h, tm=1024, tn=1024, tk=2048)
    out = rmsnorm_rows(h, final_g, tm=512)
    return out.reshape(B, S, D)
```

```python
import functools
import math

import jax
import jax.numpy as jnp
from jax import lax
from jax.experimental import pallas as pl
from jax.experimental.pallas import tpu as pltpu

F32 = jnp.float32
BF16 = jnp.bfloat16

HEAD_DIM = 64
LANES = 128
RWKV_DIM = 1024
DECAY_LORA = 96
AAA_LORA = 96
GATE_LORA = 256
LORA_COLS = GATE_LORA + 2 * DECAY_LORA + 2 * AAA_LORA
LORA_TAIL = LORA_COLS - GATE_LORA
ATTN_GROUPS = ((128, 1), (512, 4), (2048, 16))
ATTN_SLOTS = 8
ATTN_DIM = 1536
ATTN_OUT_DIM = ATTN_SLOTS * HEAD_DIM
ATTN_HALF = 64
N_BUCKETS = 32
MAX_DISTANCE = 1024
RMS_EPS = 1e-6
GN_EPS = 64e-5
L2_EPS = 1e-12
NEG_BIG = -1e30
CHUNK = 64
VMEM_LIMIT = 56 * 1024 * 1024


def _cparams(*sem):
    return pltpu.CompilerParams(dimension_semantics=sem, vmem_limit_bytes=VMEM_LIMIT)


def _dot(a, b):
    return jnp.dot(a.astype(BF16), b.astype(BF16), preferred_element_type=F32)


def _split3(x):
    p0 = x.astype(BF16)
    r1 = x - p0.astype(F32)
    p1 = r1.astype(BF16)
    p2 = (r1 - p1.astype(F32)).astype(BF16)
    return p0, p1, p2


def _dot_exact_rhs(a_bf16, x):
    return sum(jnp.dot(a_bf16, p, preferred_element_type=F32) for p in _split3(x))


def _head_sum(x, ones_bd):
    return sum(jnp.dot(p, ones_bd, preferred_element_type=F32) for p in _split3(x))


def _head_block_ones(width):
    r = lax.broadcasted_iota(jnp.int32, (width, width), 0) // HEAD_DIM
    c = lax.broadcasted_iota(jnp.int32, (width, width), 1) // HEAD_DIM
    return (r == c).astype(BF16)


def _norm_mm_kernel(x_ref, g_ref, w_ref, o_ref, u_ref, *, act):
    @pl.when(pl.program_id(1) == 0)
    def _():
        x = x_ref[...]
        ms = jnp.mean(x * x, axis=-1, keepdims=True)
        u_ref[...] = (x * lax.rsqrt(ms + RMS_EPS) * g_ref[...]).astype(BF16)

    acc = jnp.dot(u_ref[...], w_ref[...], preferred_element_type=F32)
    if act == "relu2":
        acc = jnp.square(jnp.maximum(acc, 0.0))
    elif act == "sigmoid":
        acc = jax.nn.sigmoid(acc)
    o_ref[...] = acc.astype(o_ref.dtype)


def norm_matmul(x, g, w, *, tm, tn, act=None, out_dtype=F32):
    M, K = x.shape
    N = w.shape[1]
    return pl.pallas_call(
        functools.partial(_norm_mm_kernel, act=act),
        grid=(M // tm, N // tn),
        in_specs=[pl.BlockSpec((tm, K), lambda i, j: (i, 0)),
                  pl.BlockSpec((1, K), lambda i, j: (0, 0)),
                  pl.BlockSpec((K, tn), lambda i, j: (0, j))],
        out_specs=pl.BlockSpec((tm, tn), lambda i, j: (i, j)),
        out_shape=jax.ShapeDtypeStruct((M, N), out_dtype),
        scratch_shapes=[pltpu.VMEM((tm, K), BF16)],
        compiler_params=_cparams("parallel", "arbitrary"),
        name="norm_matmul_" + (act or "id"),
    )(x, g.reshape(1, K), w)


def _qkv_proj_kernel(x_ref, g_ref, w_ref, o0_ref, o1_ref, o2_ref, u_ref, acc_ref):
    @pl.when(pl.program_id(1) == 0)
    def _():
        x = x_ref[...]
        ms = jnp.mean(x * x, axis=-1, keepdims=True)
        u_ref[...] = (x * lax.rsqrt(ms + RMS_EPS) * g_ref[...]).astype(BF16)

    acc = jnp.dot(u_ref[...], w_ref[...], preferred_element_type=F32)
    tm = acc.shape[0]
    tiles_per_group = ATTN_OUT_DIM // LANES
    for gi, (o_ref, (_, d)) in enumerate(zip((o0_ref, o1_ref, o2_ref), ATTN_GROUPS)):
        if d == 1:
            o_ref[0, 0] = acc[:, gi * ATTN_OUT_DIM:(gi + 1) * ATTN_OUT_DIM].astype(BF16)
            continue
        for t in range(tiles_per_group):
            c = gi * tiles_per_group + t
            acc_ref[c] = acc[:, c * LANES:(c + 1) * LANES]
            for res in range(d):
                o_ref[0, res, :, t * LANES:(t + 1) * LANES] = (
                    acc_ref[c, pl.ds(res, tm // d, stride=d), :].astype(BF16))


def qkv_proj(x, g, w, *, batch, tm):
    M, K = x.shape
    S = M // batch
    nb = S // tm
    outs_shape, outs_spec = [], []
    for _, d in ATTN_GROUPS:
        outs_shape.append(jax.ShapeDtypeStruct((batch, d, S // d, 3 * ATTN_OUT_DIM), BF16))
        outs_spec.append(pl.BlockSpec((1, d, tm // d, ATTN_OUT_DIM),
                                      lambda i, j: (i // nb, 0, i % nb, j)))
    return pl.pallas_call(
        _qkv_proj_kernel,
        grid=(M // tm, 3),
        in_specs=[pl.BlockSpec((tm, K), lambda i, j: (i, 0)),
                  pl.BlockSpec((1, K), lambda i, j: (0, 0)),
                  pl.BlockSpec((K, ATTN_DIM), lambda i, j: (0, j))],
        out_specs=outs_spec,
        out_shape=outs_shape,
        scratch_shapes=[pltpu.VMEM((tm, K), BF16), pltpu.VMEM((ATTN_DIM // LANES, tm, LANES), F32)],
        compiler_params=_cparams("parallel", "arbitrary"),
        name="qkv_proj",
    )(x, g.reshape(1, K), w)


def _mm_res_kernel(x_ref, w_ref, r_ref, o_ref, acc_ref):
    k = pl.program_id(2)

    @pl.when(k == 0)
    def _():
        acc_ref[...] = jnp.zeros_like(acc_ref)

    acc_ref[...] += jnp.dot(x_ref[...], w_ref[...], preferred_element_type=F32)

    @pl.when(k == pl.num_programs(2) - 1)
    def _():
        o_ref[...] = r_ref[...] + acc_ref[...]


def matmul_residual(x, w, res, *, tm, tn, tk):
    M, K = x.shape
    N = w.shape[1]
    return pl.pallas_call(
        _mm_res_kernel,
        grid=(M // tm, N // tn, K // tk),
        in_specs=[pl.BlockSpec((tm, tk), lambda i, j, k: (i, k)),
                  pl.BlockSpec((tk, tn), lambda i, j, k: (k, j)),
                  pl.BlockSpec((tm, tn), lambda i, j, k: (i, j))],
        out_specs=pl.BlockSpec((tm, tn), lambda i, j, k: (i, j)),
        out_shape=jax.ShapeDtypeStruct((M, N), F32),
        scratch_shapes=[pltpu.VMEM((tm, tn), F32)],
        compiler_params=_cparams("parallel", "parallel", "arbitrary"),
        name="matmul_residual",
    )(x, w, res)


def _rmsnorm_kernel(x_ref, g_ref, o_ref):
    x = x_ref[...]
    ms = jnp.mean(x * x, axis=-1, keepdims=True)
    o_ref[...] = x * lax.rsqrt(ms + RMS_EPS) * g_ref[...]


def rmsnorm_rows(x, g, *, tm):
    M, K = x.shape
    return pl.pallas_call(
        _rmsnorm_kernel,
        grid=(M // tm,),
        in_specs=[pl.BlockSpec((tm, K), lambda i: (i, 0)),
                  pl.BlockSpec((1, K), lambda i: (0, 0))],
        out_specs=pl.BlockSpec((tm, K), lambda i: (i, 0)),
        out_shape=jax.ShapeDtypeStruct((M, K), F32),
        compiler_params=_cparams("parallel"),
        name="final_rmsnorm",
    )(x, g.reshape(1, K))


def _token_shift(main, prev_blk, next_blk, mu, first, last):
    rows = main.shape[0]
    prev_row = jnp.where(first, 0.0, prev_blk[7:8, :])
    next_row = jnp.where(last, 0.0, next_blk[0:1, :])
    rid = lax.broadcasted_iota(jnp.int32, main.shape, 0)
    up = jnp.where(rid == 0, prev_row, pltpu.roll(main, 1, 0))
    dn = jnp.where(rid == rows - 1, next_row, pltpu.roll(main, rows - 1, 0))
    return main + mu * (0.5 * (up + dn) - main)


def _rwkv_prep_kernel(r_m, r_p, r_n, k_m, k_p, k_n, v_m, v_p, v_n, l_m, l_p, l_n,
                      mu_r, mu_k, mu_v, mu_l, w0_ref, a0_ref, lora_w_ref, g_up_ref,
                      kk_ref, ka_ref, rk_ref,
                      r_o, v_o, kk_o, lwf_o, lwb_o, kaf_o, kab_o, kdf_o, kdb_o, g_o, bonus_o):
    i = pl.program_id(1)
    first = i == 0
    last = i == pl.num_programs(1) - 1
    r = _token_shift(r_m[0], r_p[0], r_n[0], mu_r[...], first, last)
    k = _token_shift(k_m[0], k_p[0], k_n[0], mu_k[...], first, last)
    v = _token_shift(v_m[0], v_p[0], v_n[0], mu_v[...], first, last)
    lo = _token_shift(l_m[0], l_p[0], l_n[0], mu_l[...], first, last)

    g = _dot(jax.nn.sigmoid(lo[:, :GATE_LORA]), g_up_ref[...])
    tail = lo[:, GATE_LORA:]
    col = lax.broadcasted_iota(jnp.int32, tail.shape, 1)
    tail = jnp.where(col < 2 * DECAY_LORA, jnp.tanh(tail), tail)
    up = _dot(tail, lora_w_ref[...])

    ones_bd = _head_block_ones(2 * LANES)

    def head_sum(x):
        return jnp.concatenate(
            [_head_sum(x[:, c:c + 2 * LANES], ones_bd) for c in range(0, RWKV_DIM, 2 * LANES)], axis=1)

    kk = k * kk_ref[...]
    kk = kk / jnp.maximum(jnp.sqrt(head_sum(kk * kk)), L2_EPS)

    r_o[0] = r
    v_o[0] = v
    kk_o[0] = kk
    g_o[0] = g
    kd_sum = jnp.zeros_like(k)
    for d, (lw_o, ka_o, kd_o) in enumerate(((lwf_o, kaf_o, kdf_o), (lwb_o, kab_o, kdb_o))):
        z = w0_ref[d:d + 1, :] + up[:, d * RWKV_DIM:(d + 1) * RWKV_DIM]
        nz = -z
        softplus = jnp.maximum(nz, 0.0) + jnp.log1p(jnp.exp(-jnp.abs(nz)))
        lw_o[0] = -jnp.exp(-softplus - 0.5)
        a = jax.nn.sigmoid(a0_ref[d:d + 1, :] + up[:, (2 + d) * RWKV_DIM:(3 + d) * RWKV_DIM])
        kd = k * (1.0 + (a - 1.0) * ka_ref[...])
        ka_o[0] = kk * a
        kd_o[0] = kd
        kd_sum = kd_sum + kd
    bonus_o[0] = head_sum(r * kd_sum * rk_ref[...]) * v


def rwkv_prep(rkv, lora, mu, w0, a0, lora_w, g_up, k_k, k_a, r_k, *, ts):
    B, S, _ = rkv.shape
    D = RWKV_DIM
    nblk8 = S // 8

    def main_spec(width, cblk):
        return pl.BlockSpec((1, ts, width), lambda b, i: (b, i, cblk))

    def prev_spec(width, cblk):
        return pl.BlockSpec((1, 8, width), lambda b, i: (b, jnp.maximum(i * (ts // 8) - 1, 0), cblk))

    def next_spec(width, cblk):
        return pl.BlockSpec((1, 8, width), lambda b, i: (b, jnp.minimum((i + 1) * (ts // 8), nblk8 - 1), cblk))

    def row_spec(width):
        return pl.BlockSpec((1, width), lambda b, i: (0, 0))

    def full_spec(shape):
        return pl.BlockSpec(shape, lambda b, i: (0,) * len(shape))

    in_specs = []
    args = []
    for cblk in range(3):
        in_specs += [main_spec(D, cblk), prev_spec(D, cblk), next_spec(D, cblk)]
        args += [rkv, rkv, rkv]
    in_specs += [main_spec(LORA_COLS, 0), prev_spec(LORA_COLS, 0), next_spec(LORA_COLS, 0)]
    args += [lora, lora, lora]
    mu_r, mu_k, mu_v, mu_l = (mu[None, 0:D], mu[None, D:2 * D], mu[None, 2 * D:3 * D], mu[None, 3 * D:])
    in_specs += [row_spec(D), row_spec(D), row_spec(D), row_spec(LORA_COLS),
                 full_spec((2, D)), full_spec((2, D)), full_spec((LORA_TAIL, 4 * D)),
                 full_spec((GATE_LORA, D)), row_spec(D), row_spec(D), row_spec(D)]
    args += [mu_r, mu_k, mu_v, mu_l, w0, a0, lora_w, g_up, k_k[None], k_a[None], r_k.reshape(1, D)]
    out_spec = pl.BlockSpec((1, ts, D), lambda b, i: (b, i, 0))
    n_out = 11
    return pl.pallas_call(
        _rwkv_prep_kernel,
        grid=(B, S // ts),
        in_specs=in_specs,
        out_specs=[out_spec] * n_out,
        out_shape=[jax.ShapeDtypeStruct((B, S, D), F32)] * n_out,
        compiler_params=_cparams("parallel", "arbitrary"),
        name="rwkv_prep",
    )(*args)


_BMM = (((2,), (1,)), ((0,), (0,)))
_BMM_NT = (((2,), (2,)), ((0,), (0,)))
_BMM_TN = (((1,), (1,)), ((0,), (0,)))


def _bdot(a, b, dims=_BMM):
    return lax.dot_general(a.astype(BF16), b.astype(BF16), dims, preferred_element_type=F32)


def _bd_stack(x, m0):
    zero = jnp.zeros_like(x)
    return jnp.concatenate([jnp.where(m0, x, zero), jnp.where(m0, zero, x)], axis=-2)


def _scan_chunk(r, v, kk, lw, cum, ka, kd, s_bd, consts):
    is_rev, strict2, incl2, eye2, m0_1, m0_2, bd_mask, off_masks = consts
    C = r.shape[1]
    cum_prev = cum - lw
    e_cum = jnp.exp(cum)
    e_prev = jnp.exp(cum_prev)
    e_neg = jnp.exp(-cum)
    total = jnp.where(is_rev, cum[:, 0:1, :], cum[:, C - 1:C, :])
    e_rest = jnp.exp(total - cum)
    r0 = r * e_cum
    at0 = -kk * e_prev
    bt = ka * e_neg
    kt = kd * e_neg
    bh = ka * e_rest
    kh = kd * e_rest

    gram = _bdot(jnp.concatenate([at0, r0], axis=1),
                 jnp.concatenate([_bd_stack(bt, m0_1), _bd_stack(kt, m0_1)], axis=1), _BMM_NT)
    zero = jnp.zeros_like(gram[:, :C])
    a_top = jnp.where(strict2, gram[:, :C], zero)
    a_bot = jnp.where(incl2, gram[:, C:], zero)
    a_ab = a_top[:, :, :LANES]
    a_ak = a_top[:, :, LANES:]

    zero1 = jnp.zeros_like(a_ab)
    t = eye2 + jnp.where(off_masks[0], a_ab, zero1)
    for off_mask in off_masks[1:]:
        z = _bdot(t, _bd_stack(jnp.where(off_mask, a_ab, zero1), m0_1))
        t = t + _bdot(z, _bd_stack(t, m0_1))

    av = _bdot(a_ak, _bd_stack(v, m0_1))
    wu = _bdot(t, _bd_stack(jnp.concatenate([at0, av], axis=2), m0_2))
    w_a = wu[:, :, :LANES]
    u_v = wu[:, :, LANES:]

    wr = _bdot(jnp.concatenate([w_a, r0], axis=1), s_bd, _BMM_NT)
    u = wr[:, :C] + u_v
    y = wr[:, C:] + _bdot(a_bot, jnp.concatenate([_bd_stack(u, m0_1), _bd_stack(v, m0_1)], axis=1))
    upd = _bdot(jnp.concatenate([u, v], axis=1), jnp.concatenate([bh, kh], axis=1), _BMM_TN)
    s_new = s_bd * jnp.exp(total) + jnp.where(bd_mask, upd, jnp.zeros_like(upd))
    return y, s_new


def _scan_consts(C, n_fwd, n_bwd):
    G = n_fwd + n_bwd
    is_rev = lax.broadcasted_iota(jnp.int32, (G, 1, 1), 0) >= n_fwd
    sign = jnp.where(is_rev, -1, 1)
    row2 = lax.broadcasted_iota(jnp.int32, (G, C, 2 * LANES), 1)
    col2 = lax.broadcasted_iota(jnp.int32, (G, C, 2 * LANES), 2) % HEAD_DIM
    strict2 = (col2 - row2) * sign < 0
    incl2 = (col2 - row2) * sign <= 0
    lane1 = lax.broadcasted_iota(jnp.int32, (1, C, LANES), 2)
    lane2 = lax.broadcasted_iota(jnp.int32, (1, C, 2 * LANES), 2) % LANES
    m0_1 = lane1 < HEAD_DIM
    m0_2 = lane2 < HEAD_DIM
    rb = lax.broadcasted_iota(jnp.int32, (1, LANES, LANES), 1) // HEAD_DIM
    cb = lax.broadcasted_iota(jnp.int32, (1, LANES, LANES), 2) // HEAD_DIM
    bd_mask = rb == cb
    row1 = lax.broadcasted_iota(jnp.int32, (1, C, LANES), 1)
    col1 = lane1 % HEAD_DIM
    eye2 = (row1 == col1).astype(F32)
    off_masks = [row1 // 2 == col1 // 2]
    b = 2
    while b < C:
        off_masks.append((row1 // (2 * b) == col1 // (2 * b)) & (row1 // b != col1 // b))
        b *= 2
    return (is_rev, strict2, incl2, eye2, m0_1, m0_2, bd_mask, off_masks)


def _rwkv_scan_kernel(rf, vf, kkf, lwf, kaf, kdf, rb, vb, kkb, lwb, kab, kdb,
                      yf_o, yb_o, state_ref):
    c = pl.program_id(1)

    @pl.when(c == 0)
    def _():
        state_ref[...] = jnp.zeros_like(state_ref)

    C = rf.shape[1]
    n_pairs = rf.shape[2] // LANES
    consts = _scan_consts(C, n_pairs, n_pairs)

    row = lax.broadcasted_iota(jnp.int32, (C, C), 0)
    col = lax.broadcasted_iota(jnp.int32, (C, C), 1)
    cum_f = _dot_exact_rhs((col <= row).astype(F32).astype(BF16), lwf[0])
    cum_b = _dot_exact_rhs((col >= row).astype(F32).astype(BF16), lwb[0])

    def pairs(x_f, x_b):
        return jnp.stack([x[:, p * LANES:(p + 1) * LANES] for x in (x_f, x_b) for p in range(n_pairs)],
                         axis=0)

    r, v, kk, lw, ka, kd = (pairs(f[0], b[0]) for f, b in
                            ((rf, rb), (vf, vb), (kkf, kkb), (lwf, lwb), (kaf, kab), (kdf, kdb)))
    y, s_new = _scan_chunk(r, v, kk, lw, pairs(cum_f, cum_b), ka, kd, state_ref[...], consts)
    state_ref[...] = s_new
    for d, y_o in enumerate((yf_o, yb_o)):
        for p in range(n_pairs):
            y_o[0, :, p * LANES:(p + 1) * LANES] = y[d * n_pairs + p]


def rwkv_scan(r, v, kk, lwf, lwb, kaf, kab, kdf, kdb):
    B, S, D = r.shape
    C = CHUNK
    nc = S // C
    fwd = pl.BlockSpec((1, C, D), lambda b, c: (b, c, 0))
    bwd = pl.BlockSpec((1, C, D), lambda b, c: (b, nc - 1 - c, 0))
    return pl.pallas_call(
        _rwkv_scan_kernel,
        grid=(B, nc),
        in_specs=[fwd] * 6 + [bwd] * 6,
        out_specs=[fwd, bwd],
        out_shape=[jax.ShapeDtypeStruct((B, S, D), F32)] * 2,
        scratch_shapes=[pltpu.VMEM((2 * (D // LANES), LANES, LANES), F32)],
        compiler_params=_cparams("parallel", "arbitrary"),
        name="rwkv_scan",
    )(r, v, kk, lwf, kaf, kdf, r, v, kk, lwb, kab, kdb)


def _attn_kernel(table_ref, bucket_ref, q_ref, kp_ref, km_ref, kn_ref, vp_ref, vm_ref, vn_ref,
                 o_ref, lse_ref, bias_ref, *, nblk):
    i = pl.program_id(2)
    TQ = q_ref.shape[0]
    TK = TQ + 2 * ATTN_HALF

    @pl.when((pl.program_id(0) == 0) & (pl.program_id(1) == 0) & (i == 0))
    def _():
        bucket = bucket_ref[...]
        row = lax.broadcasted_iota(jnp.int32, (TQ, TK), 0)
        colk = lax.broadcasted_iota(jnp.int32, (TQ, TK), 1)
        band = jnp.abs(colk - ATTN_HALF - row) <= ATTN_HALF
        for h in range(ATTN_SLOTS):
            acc = jnp.zeros((TQ, TK), F32)
            for b in range(N_BUCKETS):
                acc = jnp.where(bucket == b, table_ref[h * N_BUCKETS + b], acc)
            bias_ref[h] = jnp.where(band, acc, NEG_BIG)

    col = lax.broadcasted_iota(jnp.int32, (TQ, TK), 1)
    edge_ok = ((col >= ATTN_HALF) | (i > 0)) & ((col < TQ + ATTN_HALF) | (i < nblk - 1))
    m0 = lax.broadcasted_iota(jnp.int32, (TQ, LANES), 1) < HEAD_DIM
    for p in range(ATTN_OUT_DIM // LANES):
        sl = slice(p * LANES, (p + 1) * LANES)
        q2 = q_ref[:, sl]
        kwin = jnp.concatenate([kp_ref[:, sl], km_ref[:, sl], kn_ref[:, sl]], axis=0)
        vwin = jnp.concatenate([vp_ref[:, sl], vm_ref[:, sl], vn_ref[:, sl]], axis=0)
        outs, lses = [], []
        for hh in range(2):
            qm = jnp.where(m0 if hh == 0 else ~m0, q2, jnp.zeros_like(q2))
            s = lax.dot_general(qm, kwin, (((1,), (1,)), ((), ())), preferred_element_type=F32)
            s = s * (HEAD_DIM ** -0.5) + bias_ref[2 * p + hh]
            s = jnp.where(edge_ok, s, NEG_BIG)
            m = jnp.max(s, axis=-1, keepdims=True)
            e = jnp.exp(s - m)
            den = jnp.sum(e, axis=-1, keepdims=True)
            pv = jnp.dot(e.astype(BF16), vwin, preferred_element_type=F32)
            outs.append(pv / den)
            lses.append(jnp.broadcast_to(m + jnp.log(den), (TQ, LANES)))
        o_ref[:, sl] = jnp.where(m0, outs[0], outs[1])
        lse_ref[:, sl] = jnp.where(m0, lses[0], lses[1])


def _t5_bucket(rel):
    nb = N_BUCKETS // 2
    max_exact = nb // 2
    ret = jnp.where(rel > 0, nb, 0)
    n = jnp.abs(rel)
    nf = jnp.maximum(n, 1).astype(jnp.float32)
    large = max_exact + (jnp.log(nf / max_exact) / math.log(MAX_DISTANCE / max_exact)
                         * (nb - max_exact)).astype(jnp.int32)
    large = jnp.minimum(large, nb - 1)
    return ret + jnp.where(n < max_exact, n, large)


def dilated_attention(qkv, table, gi, *, tq):
    B, dilation, L, _ = qkv.shape
    nblk = L // tq
    hb = tq // ATTN_HALF
    nhalf = L // ATTN_HALF
    W = ATTN_OUT_DIM

    def main(off):
        return pl.BlockSpec((None, None, tq, W), lambda b, r, i: (b, r, i, off))

    def prev(off):
        return pl.BlockSpec((None, None, ATTN_HALF, W),
                            lambda b, r, i: (b, r, jnp.maximum(i * hb - 1, 0), off))

    def nxt(off):
        return pl.BlockSpec((None, None, ATTN_HALF, W),
                            lambda b, r, i: (b, r, jnp.minimum((i + 1) * hb, nhalf - 1), off))

    tk = tq + 2 * ATTN_HALF
    rel = jnp.arange(tk)[None, :] - ATTN_HALF - jnp.arange(tq)[:, None]
    bucket = _t5_bucket(rel * dilation).astype(jnp.int32)
    tbl = table[:, gi * ATTN_SLOTS:(gi + 1) * ATTN_SLOTS].astype(F32).T.reshape(-1)
    out_spec = pl.BlockSpec((None, None, tq, W), lambda b, r, i: (b, r, i, 0))
    return pl.pallas_call(
        functools.partial(_attn_kernel, nblk=nblk),
        grid=(B, dilation, nblk),
        in_specs=[pl.BlockSpec(memory_space=pltpu.SMEM),
                  pl.BlockSpec((tq, tk), lambda b, r, i: (0, 0)),
                  main(0), prev(1), main(1), nxt(1), prev(2), main(2), nxt(2)],
        out_specs=[out_spec, out_spec],
        out_shape=[jax.ShapeDtypeStruct((B, dilation, L, W), F32)] * 2,
        scratch_shapes=[pltpu.VMEM((ATTN_SLOTS, tq, tk), F32)],
        compiler_params=_cparams("arbitrary", "arbitrary", "arbitrary"),
        name=f"dilated_attn_g{gi}",
    )(tbl, bucket, qkv, qkv, qkv, qkv, qkv, qkv, qkv)


def _branch_post_kernel(yf_ref, yb_ref, bonus_ref, g_ref, gnw_ref, gnb_ref,
                        o0, o1, o2, l0, l1, l2, orw_ref, oat_ref, *scratch):
    y = yf_ref[...] + yb_ref[...]
    ones_bd = _head_block_ones(2 * LANES)
    tm, width = y.shape

    def head_mean(x):
        return jnp.concatenate(
            [_head_sum(x[:, c:c + 2 * LANES], ones_bd) for c in range(0, width, 2 * LANES)],
            axis=1) * (1.0 / HEAD_DIM)

    mean = head_mean(y)
    yc = y - mean
    var = head_mean(yc * yc)
    yn = yc * lax.rsqrt(var + GN_EPS) * gnw_ref[...] + gnb_ref[...]
    orw_ref[...] = ((yn + bonus_ref[...]) * g_ref[...]).astype(BF16)

    def natural_order(ref, scr):
        d = ref.shape[0]
        for t in range(ATTN_OUT_DIM // LANES):
            for res in range(d):
                scr[t, pl.ds(res, tm // d, stride=d), :] = ref[res, :, t * LANES:(t + 1) * LANES]
        return jnp.concatenate([scr[t] for t in range(ATTN_OUT_DIM // LANES)], axis=1)

    s_o1, s_o2, s_l1, s_l2 = scratch
    oa, ob, oc = o0[0], natural_order(o1, s_o1), natural_order(o2, s_o2)
    la, lb, lc = l0[0], natural_order(l1, s_l1), natural_order(l2, s_l2)
    m = jnp.maximum(jnp.maximum(la, lb), lc)
    ea, eb, ec = jnp.exp(la - m), jnp.exp(lb - m), jnp.exp(lc - m)
    den = ea + eb + ec
    oat_ref[...] = ((ea * oa + eb * ob + ec * oc) / den).astype(BF16)


def branch_post(yf, yb, bonus, g, gn_w, gn_b, outs, lses, *, tm):
    M, D = yf.shape
    A = ATTN_OUT_DIM
    batch = outs[0].shape[0]
    nb = M // batch // tm
    big = pl.BlockSpec((tm, D), lambda b, i: (b * nb + i, 0))
    rowp = pl.BlockSpec((1, D), lambda b, i: (0, 0))
    attn_specs = [pl.BlockSpec((None, o.shape[1], tm // o.shape[1], A), lambda b, i: (b, 0, i, 0))
                  for o in outs]
    return pl.pallas_call(
        _branch_post_kernel,
        grid=(batch, nb),
        in_specs=[big] * 4 + [rowp, rowp] + attn_specs + attn_specs,
        out_specs=[big, pl.BlockSpec((tm, A), lambda b, i: (b * nb + i, 0))],
        out_shape=[jax.ShapeDtypeStruct((M, D), BF16), jax.ShapeDtypeStruct((M, A), BF16)],
        scratch_shapes=[pltpu.VMEM((A // LANES, tm, LANES), F32)] * 4,
        compiler_params=_cparams("parallel", "arbitrary"),
        name="branch_post",
    )(yf, yb, bonus, g, gn_w[None], gn_b[None], *outs, *lses)


def _merge_kernel(orw_ref, oat_ref, wr_ref, wa_ref, gr_ref, ga_ref, o_ref):
    a = jnp.dot(orw_ref[...], wr_ref[...], preferred_element_type=F32)
    b = jnp.dot(oat_ref[...], wa_ref[...], preferred_element_type=F32)
    o_ref[...] = (gr_ref[...] * a + ga_ref[...] * b).astype(o_ref.dtype)


def branch_merge(orw, oat, w_r, w_a, gates, *, tm, tn):
    M = orw.shape[0]
    N = w_r.shape[1]
    nj = N // tn
    return pl.pallas_call(
        _merge_kernel,
        grid=(M // tm, nj),
        in_specs=[pl.BlockSpec((tm, orw.shape[1]), lambda i, j: (i, 0)),
                  pl.BlockSpec((tm, oat.shape[1]), lambda i, j: (i, 0)),
                  pl.BlockSpec((w_r.shape[0], tn), lambda i, j: (0, j)),
                  pl.BlockSpec((w_a.shape[0], tn), lambda i, j: (0, j)),
                  pl.BlockSpec((tm, tn), lambda i, j: (i, j)),
                  pl.BlockSpec((tm, tn), lambda i, j: (i, j + nj))],
        out_specs=pl.BlockSpec((tm, tn), lambda i, j: (i, j)),
        out_shape=jax.ShapeDtypeStruct((M, N), BF16),
        compiler_params=_cparams("parallel", "arbitrary"),
        name="branch_merge",
    )(orw, oat, w_r, w_a, gates, gates)


def _lora_weights(w_up, a_up):
    z = jnp.zeros((DECAY_LORA, RWKV_DIM), F32)
    rows = [jnp.concatenate([w_up[0], z, z, z], axis=1),
            jnp.concatenate([z, w_up[1], z, z], axis=1),
            jnp.concatenate([z, z, a_up[0], z], axis=1),
            jnp.concatenate([z, z, z, a_up[1]], axis=1)]
    return jnp.concatenate(rows, axis=0).astype(BF16)


def kernel(x, norm1_g, w_in, tshift_mu, w0, w_lora_up, a0, a_lora_up, g_lora_up, k_k, k_a, r_k,
           gn_w, gn_b, rel_bias, w_branch_rwkv, w_branch_attn, w_out, norm2_g, w_mlp_in, w_mlp_out,
           final_g):
    B, S, D = x.shape
    M = B * S
    depth = w_in.shape[0]
    c_rkv = 3 * RWKV_DIM
    c_slab = c_rkv + LORA_COLS
    c_attn = c_slab + 3 * ATTN_DIM
    h = x.reshape(M, D)
    for l in range(depth):
        wl = w_in[l]
        g1 = norm1_g[l]
        rkv = norm_matmul(h, g1, wl[:, :c_rkv].astype(BF16), tm=1024, tn=1024)
        lora = norm_matmul(h, g1, wl[:, c_rkv:c_slab].astype(BF16), tm=1024, tn=LORA_COLS)
        qkv_groups = qkv_proj(h, g1, wl[:, c_slab:c_attn].astype(BF16), batch=B, tm=1024)
        gates = norm_matmul(h, g1, wl[:, c_attn:].astype(BF16), tm=1024, tn=1024, act="sigmoid")

        (r, v, kk, lwf, lwb, kaf, kab, kdf, kdb, g, bonus) = rwkv_prep(
            rkv.reshape(B, S, c_rkv), lora.reshape(B, S, LORA_COLS), tshift_mu[l], w0[l], a0[l],
            _lora_weights(w_lora_up[l], a_lora_up[l]), g_lora_up[l].astype(BF16),
            k_k[l], k_a[l], r_k[l], ts=256)
        yf, yb = rwkv_scan(r, v, kk, lwf, lwb, kaf, kab, kdf, kdb)

        outs, lses = [], []
        for gi, (window, dilation) in enumerate(ATTN_GROUPS):
            assert window // (2 * dilation) == ATTN_HALF
            o, lse = dilated_attention(qkv_groups[gi], rel_bias, gi, tq=128)
            outs.append(o)
            lses.append(lse)

        orw, oat = branch_post(yf.reshape(M, RWKV_DIM), yb.reshape(M, RWKV_DIM),
                               bonus.reshape(M, RWKV_DIM), g.reshape(M, RWKV_DIM),
                               gn_w[l], gn_b[l], outs, lses, tm=512)
        merged = branch_merge(orw, oat, w_branch_rwkv[l].astype(BF16), w_branch_attn[l].astype(BF16),
                              gates, tm=1024, tn=1024)
        h = matmul_residual(merged, w_out[l].astype(BF16), h, tm=1024, tn=1024, tk=D)

        act = norm_matmul(h, norm2_g[l], w_mlp_in[l].astype(BF16), tm=1024, tn=1024, act="relu2",
                          out_dtype=BF16)
        h = matmul_residual(act, w_mlp_out[l].astype(BF16), h, tm=1024, tn=1024, tk=2048)
    out = rmsnorm_rows(h, final_g, tm=512)
    return out.reshape(B, S, D)
```

```python
import functools
import math

import jax
import jax.numpy as jnp
from jax import lax
from jax.experimental import pallas as pl
from jax.experimental.pallas import tpu as pltpu

F32 = jnp.float32
BF16 = jnp.bfloat16

HEAD_DIM = 64
LANES = 128
RWKV_DIM = 1024
DECAY_LORA = 96
AAA_LORA = 96
GATE_LORA = 256
LORA_COLS = GATE_LORA + 2 * DECAY_LORA + 2 * AAA_LORA
LORA_TAIL = LORA_COLS - GATE_LORA
ATTN_GROUPS = ((128, 1), (512, 4), (2048, 16))
ATTN_SLOTS = 8
ATTN_DIM = 1536
ATTN_OUT_DIM = ATTN_SLOTS * HEAD_DIM
ATTN_HALF = 64
N_BUCKETS = 32
MAX_DISTANCE = 1024
RMS_EPS = 1e-6
GN_EPS = 64e-5
L2_EPS = 1e-12
NEG_BIG = -1e30
CHUNK = 64
VMEM_LIMIT = 56 * 1024 * 1024


def _cparams(*sem):
    return pltpu.CompilerParams(dimension_semantics=sem, vmem_limit_bytes=VMEM_LIMIT)


def _dot(a, b):
    return jnp.dot(a.astype(BF16), b.astype(BF16), preferred_element_type=F32)


def _split3(x):
    p0 = x.astype(BF16)
    r1 = x - p0.astype(F32)
    p1 = r1.astype(BF16)
    p2 = (r1 - p1.astype(F32)).astype(BF16)
    return p0, p1, p2


def _dot_exact_rhs(a_bf16, x):
    return sum(jnp.dot(a_bf16, p, preferred_element_type=F32) for p in _split3(x))


def _head_sum(x, ones_bd):
    return sum(jnp.dot(p, ones_bd, preferred_element_type=F32) for p in _split3(x))


def _head_block_ones(width):
    r = lax.broadcasted_iota(jnp.int32, (width, width), 0) // HEAD_DIM
    c = lax.broadcasted_iota(jnp.int32, (width, width), 1) // HEAD_DIM
    return (r == c).astype(BF16)


def _norm_mm_kernel(x_ref, g_ref, w_ref, o_ref, u_ref, *, act):
    @pl.when(pl.program_id(1) == 0)
    def _():
        x = x_ref[...]
        ms = jnp.mean(x * x, axis=-1, keepdims=True)
        u_ref[...] = (x * lax.rsqrt(ms + RMS_EPS) * g_ref[...]).astype(BF16)

    acc = jnp.dot(u_ref[...], w_ref[...], preferred_element_type=F32)
    if act == "relu2":
        acc = jnp.square(jnp.maximum(acc, 0.0))
    elif act == "sigmoid":
        acc = jax.nn.sigmoid(acc)
    o_ref[...] = acc.astype(o_ref.dtype)


def _block_cols(w, tn):
    K, N = w.shape
    return w.astype(BF16).reshape(K, N // tn, tn).transpose(1, 0, 2)


def norm_matmul(x, g, wb, *, tm, act=None, out_dtype=F32):
    M, K = x.shape
    nj, _, tn = wb.shape
    return pl.pallas_call(
        functools.partial(_norm_mm_kernel, act=act),
        grid=(M // tm, nj),
        in_specs=[pl.BlockSpec((tm, K), lambda i, j: (i, 0)),
                  pl.BlockSpec((1, K), lambda i, j: (0, 0)),
                  pl.BlockSpec((None, K, tn), lambda i, j: (j, 0, 0))],
        out_specs=pl.BlockSpec((None, tm, tn), lambda i, j: (j, i, 0)),
        out_shape=jax.ShapeDtypeStruct((nj, M, tn), out_dtype),
        scratch_shapes=[pltpu.VMEM((tm, K), BF16)],
        compiler_params=_cparams("parallel", "arbitrary"),
        name="norm_matmul_" + (act or "id"),
    )(x, g.reshape(1, K), wb)


def _qkv_proj_kernel(x_ref, g_ref, w_ref, o0_ref, o1_ref, o2_ref, u_ref, acc_ref):
    @pl.when(pl.program_id(1) == 0)
    def _():
        x = x_ref[...]
        ms = jnp.mean(x * x, axis=-1, keepdims=True)
        u_ref[...] = (x * lax.rsqrt(ms + RMS_EPS) * g_ref[...]).astype(BF16)

    acc = jnp.dot(u_ref[...], w_ref[...], preferred_element_type=F32)
    tm = acc.shape[0]
    tiles_per_group = ATTN_OUT_DIM // LANES
    for gi, (o_ref, (_, d)) in enumerate(zip((o0_ref, o1_ref, o2_ref), ATTN_GROUPS)):
        if d == 1:
            o_ref[0] = acc[:, gi * ATTN_OUT_DIM:(gi + 1) * ATTN_OUT_DIM].astype(BF16)
            continue
        for t in range(tiles_per_group):
            c = gi * tiles_per_group + t
            acc_ref[c] = acc[:, c * LANES:(c + 1) * LANES]
            for res in range(d):
                o_ref[res, :, t * LANES:(t + 1) * LANES] = (
                    acc_ref[c, pl.ds(res, tm // d, stride=d), :].astype(BF16))


def qkv_proj(x, g, wb, *, batch, tm):
    M, K = x.shape
    S = M // batch
    nb = S // tm
    outs_shape, outs_spec = [], []
    for _, d in ATTN_GROUPS:
        outs_shape.append(jax.ShapeDtypeStruct((batch, d, 3, S // d, ATTN_OUT_DIM), BF16))
        outs_spec.append(pl.BlockSpec((None, d, None, tm // d, ATTN_OUT_DIM),
                                      lambda i, j: (i // nb, 0, j, i % nb, 0)))
    return pl.pallas_call(
        _qkv_proj_kernel,
        grid=(M // tm, 3),
        in_specs=[pl.BlockSpec((tm, K), lambda i, j: (i, 0)),
                  pl.BlockSpec((1, K), lambda i, j: (0, 0)),
                  pl.BlockSpec((None, K, ATTN_DIM), lambda i, j: (j, 0, 0))],
        out_specs=outs_spec,
        out_shape=outs_shape,
        scratch_shapes=[pltpu.VMEM((tm, K), BF16), pltpu.VMEM((ATTN_DIM // LANES, tm, LANES), F32)],
        compiler_params=_cparams("parallel", "arbitrary"),
        name="qkv_proj",
    )(x, g.reshape(1, K), wb)


def _mm_res_kernel(x_ref, w_ref, r_ref, o_ref, acc_ref):
    k = pl.program_id(1)

    @pl.when(k == 0)
    def _():
        acc_ref[...] = jnp.zeros_like(acc_ref)

    nsub, _, sub = x_ref.shape
    acc_ref[...] += sum(jnp.dot(x_ref[s], w_ref[s * sub:(s + 1) * sub, :], preferred_element_type=F32)
                        for s in range(nsub))

    @pl.when(k == pl.num_programs(1) - 1)
    def _():
        o_ref[...] = r_ref[...] + acc_ref[...]


def matmul_residual(xb, w, res, *, tm, tk):
    nblk, M, sub = xb.shape
    K, N = w.shape
    assert nblk * sub == K and tk % sub == 0
    return pl.pallas_call(
        _mm_res_kernel,
        grid=(M // tm, K // tk),
        in_specs=[pl.BlockSpec((tk // sub, tm, sub), lambda i, k: (k, i, 0)),
                  pl.BlockSpec((tk, N), lambda i, k: (k, 0)),
                  pl.BlockSpec((tm, N), lambda i, k: (i, 0))],
        out_specs=pl.BlockSpec((tm, N), lambda i, k: (i, 0)),
        out_shape=jax.ShapeDtypeStruct((M, N), F32),
        scratch_shapes=[pltpu.VMEM((tm, N), F32)],
        compiler_params=_cparams("parallel", "arbitrary"),
        name="matmul_residual",
    )(xb, w, res)


def _rmsnorm_kernel(x_ref, g_ref, o_ref):
    x = x_ref[...]
    ms = jnp.mean(x * x, axis=-1, keepdims=True)
    o_ref[...] = x * lax.rsqrt(ms + RMS_EPS) * g_ref[...]


def rmsnorm_rows(x, g, *, tm):
    M, K = x.shape
    return pl.pallas_call(
        _rmsnorm_kernel,
        grid=(M // tm,),
        in_specs=[pl.BlockSpec((tm, K), lambda i: (i, 0)),
                  pl.BlockSpec((1, K), lambda i: (0, 0))],
        out_specs=pl.BlockSpec((tm, K), lambda i: (i, 0)),
        out_shape=jax.ShapeDtypeStruct((M, K), F32),
        compiler_params=_cparams("parallel"),
        name="final_rmsnorm",
    )(x, g.reshape(1, K))


def _token_shift(main, prev_blk, next_blk, mu, first, last):
    rows = main.shape[0]
    prev_row = jnp.where(first, 0.0, prev_blk[7:8, :])
    next_row = jnp.where(last, 0.0, next_blk[0:1, :])
    rid = lax.broadcasted_iota(jnp.int32, main.shape, 0)
    up = jnp.where(rid == 0, prev_row, pltpu.roll(main, 1, 0))
    dn = jnp.where(rid == rows - 1, next_row, pltpu.roll(main, rows - 1, 0))
    return main + mu * (0.5 * (up + dn) - main)


def _rwkv_prep_kernel(r_m, r_p, r_n, k_m, k_p, k_n, v_m, v_p, v_n, l_m, l_p, l_n,
                      mu_r, mu_k, mu_v, mu_l, w0_ref, a0_ref, lora_w_ref, g_up_ref,
                      kk_ref, ka_ref, rk_ref,
                      r_o, v_o, kk_o, lwf_o, lwb_o, kaf_o, kab_o, kdf_o, kdb_o, g_o, bonus_o):
    i = pl.program_id(1)
    first = i == 0
    last = i == pl.num_programs(1) - 1
    r = _token_shift(r_m[0], r_p[0], r_n[0], mu_r[...], first, last)
    k = _token_shift(k_m[0], k_p[0], k_n[0], mu_k[...], first, last)
    v = _token_shift(v_m[0], v_p[0], v_n[0], mu_v[...], first, last)
    lo = _token_shift(l_m[0], l_p[0], l_n[0], mu_l[...], first, last)

    g = _dot(jax.nn.sigmoid(lo[:, :GATE_LORA]), g_up_ref[...])
    tail = lo[:, GATE_LORA:]
    col = lax.broadcasted_iota(jnp.int32, tail.shape, 1)
    tail = jnp.where(col < 2 * DECAY_LORA, jnp.tanh(tail), tail)
    up = _dot(tail, lora_w_ref[...])

    ones_bd = _head_block_ones(2 * LANES)

    def head_sum(x):
        return jnp.concatenate(
            [_head_sum(x[:, c:c + 2 * LANES], ones_bd) for c in range(0, RWKV_DIM, 2 * LANES)], axis=1)

    kk = k * kk_ref[...]
    kk = kk / jnp.maximum(jnp.sqrt(head_sum(kk * kk)), L2_EPS)

    r_o[0] = r
    v_o[0] = v
    kk_o[0] = kk
    g_o[0] = g
    kd_sum = jnp.zeros_like(k)
    for d, (lw_o, ka_o, kd_o) in enumerate(((lwf_o, kaf_o, kdf_o), (lwb_o, kab_o, kdb_o))):
        z = w0_ref[d:d + 1, :] + up[:, d * RWKV_DIM:(d + 1) * RWKV_DIM]
        nz = -z
        softplus = jnp.maximum(nz, 0.0) + jnp.log1p(jnp.exp(-jnp.abs(nz)))
        lw_o[0] = -jnp.exp(-softplus - 0.5)
        a = jax.nn.sigmoid(a0_ref[d:d + 1, :] + up[:, (2 + d) * RWKV_DIM:(3 + d) * RWKV_DIM])
        kd = k * (1.0 + (a - 1.0) * ka_ref[...])
        ka_o[0] = kk * a
        kd_o[0] = kd
        kd_sum = kd_sum + kd
    bonus_o[0] = head_sum(r * kd_sum * rk_ref[...]) * v


def rwkv_prep(rkv, lora, mu, w0, a0, lora_w, g_up, k_k, k_a, r_k, *, ts):
    _, B, S, _ = rkv.shape
    D = RWKV_DIM
    nblk8 = S // 8

    def main_spec(width, cblk):
        return pl.BlockSpec((None, 1, ts, width), lambda b, i: (cblk, b, i, 0))

    def prev_spec(width, cblk):
        return pl.BlockSpec((None, 1, 8, width),
                            lambda b, i: (cblk, b, jnp.maximum(i * (ts // 8) - 1, 0), 0))

    def next_spec(width, cblk):
        return pl.BlockSpec((None, 1, 8, width),
                            lambda b, i: (cblk, b, jnp.minimum((i + 1) * (ts // 8), nblk8 - 1), 0))

    def row_spec(width):
        return pl.BlockSpec((1, width), lambda b, i: (0, 0))

    def full_spec(shape):
        return pl.BlockSpec(shape, lambda b, i: (0,) * len(shape))

    in_specs = []
    args = []
    for cblk in range(3):
        in_specs += [main_spec(D, cblk), prev_spec(D, cblk), next_spec(D, cblk)]
        args += [rkv, rkv, rkv]
    in_specs += [main_spec(LORA_COLS, 0), prev_spec(LORA_COLS, 0), next_spec(LORA_COLS, 0)]
    args += [lora, lora, lora]
    mu_r, mu_k, mu_v, mu_l = (mu[None, 0:D], mu[None, D:2 * D], mu[None, 2 * D:3 * D], mu[None, 3 * D:])
    in_specs += [row_spec(D), row_spec(D), row_spec(D), row_spec(LORA_COLS),
                 full_spec((2, D)), full_spec((2, D)), full_spec((LORA_TAIL, 4 * D)),
                 full_spec((GATE_LORA, D)), row_spec(D), row_spec(D), row_spec(D)]
    args += [mu_r, mu_k, mu_v, mu_l, w0, a0, lora_w, g_up, k_k[None], k_a[None], r_k.reshape(1, D)]
    out_spec = pl.BlockSpec((1, ts, D), lambda b, i: (b, i, 0))
    n_out = 11
    return pl.pallas_call(
        _rwkv_prep_kernel,
        grid=(B, S // ts),
        in_specs=in_specs,
        out_specs=[out_spec] * n_out,
        out_shape=[jax.ShapeDtypeStruct((B, S, D), F32)] * n_out,
        compiler_params=_cparams("parallel", "arbitrary"),
        name="rwkv_prep",
    )(*args)


_BMM = (((2,), (1,)), ((0,), (0,)))
_BMM_NT = (((2,), (2,)), ((0,), (0,)))
_BMM_TN = (((1,), (1,)), ((0,), (0,)))


def _bdot(a, b, dims=_BMM):
    return lax.dot_general(a.astype(BF16), b.astype(BF16), dims, preferred_element_type=F32)


def _bd_stack(x, m0):
    zero = jnp.zeros_like(x)
    return jnp.concatenate([jnp.where(m0, x, zero), jnp.where(m0, zero, x)], axis=-2)


def _scan_chunk(r, v, kk, lw, cum, ka, kd, s_bd, consts):
    is_rev, strict2, incl2, eye2, m0_1, m0_2, bd_mask, off_masks = consts
    C = r.shape[1]
    cum_prev = cum - lw
    e_cum = jnp.exp(cum)
    e_prev = jnp.exp(cum_prev)
    e_neg = jnp.exp(-cum)
    total = jnp.where(is_rev, cum[:, 0:1, :], cum[:, C - 1:C, :])
    e_rest = jnp.exp(total - cum)
    r0 = r * e_cum
    at0 = -kk * e_prev
    bt = ka * e_neg
    kt = kd * e_neg
    bh = ka * e_rest
    kh = kd * e_rest

    gram = _bdot(jnp.concatenate([at0, r0], axis=1),
                 jnp.concatenate([_bd_stack(bt, m0_1), _bd_stack(kt, m0_1)], axis=1), _BMM_NT)
    zero = jnp.zeros_like(gram[:, :C])
    a_top = jnp.where(strict2, gram[:, :C], zero)
    a_bot = jnp.where(incl2, gram[:, C:], zero)
    a_ab = a_top[:, :, :LANES]
    a_ak = a_top[:, :, LANES:]

    zero1 = jnp.zeros_like(a_ab)
    t = eye2 + jnp.where(off_masks[0], a_ab, zero1)
    for off_mask in off_masks[1:]:
        z = _bdot(t, _bd_stack(jnp.where(off_mask, a_ab, zero1), m0_1))
        t = t + _bdot(z, _bd_stack(t, m0_1))

    av = _bdot(a_ak, _bd_stack(v, m0_1))
    wu = _bdot(t, _bd_stack(jnp.concatenate([at0, av], axis=2), m0_2))
    w_a = wu[:, :, :LANES]
    u_v = wu[:, :, LANES:]

    wr = _bdot(jnp.concatenate([w_a, r0], axis=1), s_bd, _BMM_NT)
    u = wr[:, :C] + u_v
    y = wr[:, C:] + _bdot(a_bot, jnp.concatenate([_bd_stack(u, m0_1), _bd_stack(v, m0_1)], axis=1))
    upd = _bdot(jnp.concatenate([u, v], axis=1), jnp.concatenate([bh, kh], axis=1), _BMM_TN)
    s_new = s_bd * jnp.exp(total) + jnp.where(bd_mask, upd, jnp.zeros_like(upd))
    return y, s_new


def _scan_consts(C, n_fwd, n_bwd):
    G = n_fwd + n_bwd
    is_rev = lax.broadcasted_iota(jnp.int32, (G, 1, 1), 0) >= n_fwd
    sign = jnp.where(is_rev, -1, 1)
    row2 = lax.broadcasted_iota(jnp.int32, (G, C, 2 * LANES), 1)
    col2 = lax.broadcasted_iota(jnp.int32, (G, C, 2 * LANES), 2) % HEAD_DIM
    strict2 = (col2 - row2) * sign < 0
    incl2 = (col2 - row2) * sign <= 0
    lane1 = lax.broadcasted_iota(jnp.int32, (1, C, LANES), 2)
    lane2 = lax.broadcasted_iota(jnp.int32, (1, C, 2 * LANES), 2) % LANES
    m0_1 = lane1 < HEAD_DIM
    m0_2 = lane2 < HEAD_DIM
    rb = lax.broadcasted_iota(jnp.int32, (1, LANES, LANES), 1) // HEAD_DIM
    cb = lax.broadcasted_iota(jnp.int32, (1, LANES, LANES), 2) // HEAD_DIM
    bd_mask = rb == cb
    row1 = lax.broadcasted_iota(jnp.int32, (1, C, LANES), 1)
    col1 = lane1 % HEAD_DIM
    eye2 = (row1 == col1).astype(F32)
    off_masks = [row1 // 2 == col1 // 2]
    b = 2
    while b < C:
        off_masks.append((row1 // (2 * b) == col1 // (2 * b)) & (row1 // b != col1 // b))
        b *= 2
    return (is_rev, strict2, incl2, eye2, m0_1, m0_2, bd_mask, off_masks)


def _rwkv_scan_kernel(rf, vf, kkf, lwf, kaf, kdf, rb, vb, kkb, lwb, kab, kdb,
                      yf_o, yb_o, state_ref):
    c = pl.program_id(1)

    @pl.when(c == 0)
    def _():
        state_ref[...] = jnp.zeros_like(state_ref)

    C = rf.shape[1]
    n_pairs = rf.shape[2] // LANES
    consts = _scan_consts(C, n_pairs, n_pairs)

    row = lax.broadcasted_iota(jnp.int32, (C, C), 0)
    col = lax.broadcasted_iota(jnp.int32, (C, C), 1)
    cum_f = _dot_exact_rhs((col <= row).astype(F32).astype(BF16), lwf[0])
    cum_b = _dot_exact_rhs((col >= row).astype(F32).astype(BF16), lwb[0])

    def pairs(x_f, x_b):
        return jnp.stack([x[:, p * LANES:(p + 1) * LANES] for x in (x_f, x_b) for p in range(n_pairs)],
                         axis=0)

    r, v, kk, lw, ka, kd = (pairs(f[0], b[0]) for f, b in
                            ((rf, rb), (vf, vb), (kkf, kkb), (lwf, lwb), (kaf, kab), (kdf, kdb)))
    y, s_new = _scan_chunk(r, v, kk, lw, pairs(cum_f, cum_b), ka, kd, state_ref[...], consts)
    state_ref[...] = s_new
    for d, y_o in enumerate((yf_o, yb_o)):
        for p in range(n_pairs):
            y_o[0, :, p * LANES:(p + 1) * LANES] = y[d * n_pairs + p]


def rwkv_scan(r, v, kk, lwf, lwb, kaf, kab, kdf, kdb):
    B, S, D = r.shape
    C = CHUNK
    nc = S // C
    fwd = pl.BlockSpec((1, C, D), lambda b, c: (b, c, 0))
    bwd = pl.BlockSpec((1, C, D), lambda b, c: (b, nc - 1 - c, 0))
    return pl.pallas_call(
        _rwkv_scan_kernel,
        grid=(B, nc),
        in_specs=[fwd] * 6 + [bwd] * 6,
        out_specs=[fwd, bwd],
        out_shape=[jax.ShapeDtypeStruct((B, S, D), F32)] * 2,
        scratch_shapes=[pltpu.VMEM((2 * (D // LANES), LANES, LANES), F32)],
        compiler_params=_cparams("parallel", "arbitrary"),
        name="rwkv_scan",
    )(r, v, kk, lwf, kaf, kdf, r, v, kk, lwb, kab, kdb)


def _attn_kernel(table_ref, bucket_ref, q_ref, kp_ref, km_ref, kn_ref, vp_ref, vm_ref, vn_ref,
                 o_ref, lse_ref, bias_ref, *, nblk):
    i = pl.program_id(2)
    TQ = q_ref.shape[0]
    TK = TQ + 2 * ATTN_HALF

    @pl.when((pl.program_id(0) == 0) & (pl.program_id(1) == 0) & (i == 0))
    def _():
        bucket = bucket_ref[...]
        row = lax.broadcasted_iota(jnp.int32, (TQ, TK), 0)
        colk = lax.broadcasted_iota(jnp.int32, (TQ, TK), 1)
        band = jnp.abs(colk - ATTN_HALF - row) <= ATTN_HALF
        for h in range(ATTN_SLOTS):
            acc = jnp.zeros((TQ, TK), F32)
            for b in range(N_BUCKETS):
                acc = jnp.where(bucket == b, table_ref[h * N_BUCKETS + b], acc)
            bias_ref[h] = jnp.where(band, acc, NEG_BIG)

    col = lax.broadcasted_iota(jnp.int32, (TQ, TK), 1)
    edge_ok = ((col >= ATTN_HALF) | (i > 0)) & ((col < TQ + ATTN_HALF) | (i < nblk - 1))
    m0 = lax.broadcasted_iota(jnp.int32, (TQ, LANES), 1) < HEAD_DIM
    for p in range(ATTN_OUT_DIM // LANES):
        sl = slice(p * LANES, (p + 1) * LANES)
        q2 = q_ref[:, sl]
        kwin = jnp.concatenate([kp_ref[:, sl], km_ref[:, sl], kn_ref[:, sl]], axis=0)
        vwin = jnp.concatenate([vp_ref[:, sl], vm_ref[:, sl], vn_ref[:, sl]], axis=0)
        outs, lses = [], []
        for hh in range(2):
            qm = jnp.where(m0 if hh == 0 else ~m0, q2, jnp.zeros_like(q2))
            s = lax.dot_general(qm, kwin, (((1,), (1,)), ((), ())), preferred_element_type=F32)
            s = s * (HEAD_DIM ** -0.5) + bias_ref[2 * p + hh]
            s = jnp.where(edge_ok, s, NEG_BIG)
            m = jnp.max(s, axis=-1, keepdims=True)
            e = jnp.exp(s - m)
            den = jnp.sum(e, axis=-1, keepdims=True)
            pv = jnp.dot(e.astype(BF16), vwin, preferred_element_type=F32)
            outs.append(pv / den)
            lses.append(jnp.broadcast_to(m + jnp.log(den), (TQ, LANES)))
        o_ref[:, sl] = jnp.where(m0, outs[0], outs[1])
        lse_ref[:, sl] = jnp.where(m0, lses[0], lses[1])


def _t5_bucket(rel):
    nb = N_BUCKETS // 2
    max_exact = nb // 2
    ret = jnp.where(rel > 0, nb, 0)
    n = jnp.abs(rel)
    nf = jnp.maximum(n, 1).astype(jnp.float32)
    large = max_exact + (jnp.log(nf / max_exact) / math.log(MAX_DISTANCE / max_exact)
                         * (nb - max_exact)).astype(jnp.int32)
    large = jnp.minimum(large, nb - 1)
    return ret + jnp.where(n < max_exact, n, large)


def dilated_attention(qkv, table, gi, *, tq):
    B, dilation, _, L, _ = qkv.shape
    nblk = L // tq
    hb = tq // ATTN_HALF
    nhalf = L // ATTN_HALF
    W = ATTN_OUT_DIM

    def main(off):
        return pl.BlockSpec((None, None, None, tq, W), lambda b, r, i: (b, r, off, i, 0))

    def prev(off):
        return pl.BlockSpec((None, None, None, ATTN_HALF, W),
                            lambda b, r, i: (b, r, off, jnp.maximum(i * hb - 1, 0), 0))

    def nxt(off):
        return pl.BlockSpec((None, None, None, ATTN_HALF, W),
                            lambda b, r, i: (b, r, off, jnp.minimum((i + 1) * hb, nhalf - 1), 0))

    tk = tq + 2 * ATTN_HALF
    rel = jnp.arange(tk)[None, :] - ATTN_HALF - jnp.arange(tq)[:, None]
    bucket = _t5_bucket(rel * dilation).astype(jnp.int32)
    tbl = table[:, gi * ATTN_SLOTS:(gi + 1) * ATTN_SLOTS].astype(F32).T.reshape(-1)
    out_spec = pl.BlockSpec((None, None, tq, W), lambda b, r, i: (b, r, i, 0))
    return pl.pallas_call(
        functools.partial(_attn_kernel, nblk=nblk),
        grid=(B, dilation, nblk),
        in_specs=[pl.BlockSpec(memory_space=pltpu.SMEM),
                  pl.BlockSpec((tq, tk), lambda b, r, i: (0, 0)),
                  main(0), prev(1), main(1), nxt(1), prev(2), main(2), nxt(2)],
        out_specs=[out_spec, out_spec],
        out_shape=[jax.ShapeDtypeStruct((B, dilation, L, W), F32)] * 2,
        scratch_shapes=[pltpu.VMEM((ATTN_SLOTS, tq, tk), F32)],
        compiler_params=_cparams("arbitrary", "arbitrary", "arbitrary"),
        name=f"dilated_attn_g{gi}",
    )(tbl, bucket, qkv, qkv, qkv, qkv, qkv, qkv, qkv)


def _branch_post_kernel(yf_ref, yb_ref, bonus_ref, g_ref, gnw_ref, gnb_ref,
                        o0, o1, o2, l0, l1, l2, orw_ref, oat_ref, *scratch):
    y = yf_ref[...] + yb_ref[...]
    ones_bd = _head_block_ones(2 * LANES)
    tm, width = y.shape

    def head_mean(x):
        return jnp.concatenate(
            [_head_sum(x[:, c:c + 2 * LANES], ones_bd) for c in range(0, width, 2 * LANES)],
            axis=1) * (1.0 / HEAD_DIM)

    mean = head_mean(y)
    yc = y - mean
    var = head_mean(yc * yc)
    yn = yc * lax.rsqrt(var + GN_EPS) * gnw_ref[...] + gnb_ref[...]
    orw_ref[...] = ((yn + bonus_ref[...]) * g_ref[...]).astype(BF16)

    def natural_order(ref, scr):
        d = ref.shape[0]
        for t in range(ATTN_OUT_DIM // LANES):
            for res in range(d):
                scr[t, pl.ds(res, tm // d, stride=d), :] = ref[res, :, t * LANES:(t + 1) * LANES]
        return jnp.concatenate([scr[t] for t in range(ATTN_OUT_DIM // LANES)], axis=1)

    s_o1, s_o2, s_l1, s_l2 = scratch
    oa, ob, oc = o0[0], natural_order(o1, s_o1), natural_order(o2, s_o2)
    la, lb, lc = l0[0], natural_order(l1, s_l1), natural_order(l2, s_l2)
    m = jnp.maximum(jnp.maximum(la, lb), lc)
    ea, eb, ec = jnp.exp(la - m), jnp.exp(lb - m), jnp.exp(lc - m)
    den = ea + eb + ec
    oat_ref[...] = ((ea * oa + eb * ob + ec * oc) / den).astype(BF16)


def branch_post(yf, yb, bonus, g, gn_w, gn_b, outs, lses, *, tm):
    M, D = yf.shape
    A = ATTN_OUT_DIM
    batch = outs[0].shape[0]
    nb = M // batch // tm
    big = pl.BlockSpec((tm, D), lambda b, i: (b * nb + i, 0))
    rowp = pl.BlockSpec((1, D), lambda b, i: (0, 0))
    attn_specs = [pl.BlockSpec((None, o.shape[1], tm // o.shape[1], A), lambda b, i: (b, 0, i, 0))
                  for o in outs]
    return pl.pallas_call(
        _branch_post_kernel,
        grid=(batch, nb),
        in_specs=[big] * 4 + [rowp, rowp] + attn_specs + attn_specs,
        out_specs=[big, pl.BlockSpec((tm, A), lambda b, i: (b * nb + i, 0))],
        out_shape=[jax.ShapeDtypeStruct((M, D), BF16), jax.ShapeDtypeStruct((M, A), BF16)],
        scratch_shapes=[pltpu.VMEM((A // LANES, tm, LANES), F32)] * 4,
        compiler_params=_cparams("parallel", "arbitrary"),
        name="branch_post",
    )(yf, yb, bonus, g, gn_w[None], gn_b[None], *outs, *lses)


def _merge_kernel(orw_ref, oat_ref, wr_ref, wa_ref, gr_ref, ga_ref, o_ref):
    a = jnp.dot(orw_ref[...], wr_ref[...], preferred_element_type=F32)
    b = jnp.dot(oat_ref[...], wa_ref[...], preferred_element_type=F32)
    o_ref[...] = (gr_ref[...] * a + ga_ref[...] * b).astype(o_ref.dtype)


def branch_merge(orw, oat, wb_r, wb_a, gates, *, tm):
    M = orw.shape[0]
    nj, _, tn = wb_r.shape
    return pl.pallas_call(
        _merge_kernel,
        grid=(M // tm, nj),
        in_specs=[pl.BlockSpec((tm, orw.shape[1]), lambda i, j: (i, 0)),
                  pl.BlockSpec((tm, oat.shape[1]), lambda i, j: (i, 0)),
                  pl.BlockSpec((None, wb_r.shape[1], tn), lambda i, j: (j, 0, 0)),
                  pl.BlockSpec((None, wb_a.shape[1], tn), lambda i, j: (j, 0, 0)),
                  pl.BlockSpec((None, tm, tn), lambda i, j: (j, i, 0)),
                  pl.BlockSpec((None, tm, tn), lambda i, j: (j + nj, i, 0))],
        out_specs=pl.BlockSpec((None, tm, tn), lambda i, j: (j, i, 0)),
        out_shape=jax.ShapeDtypeStruct((nj, M, tn), BF16),
        compiler_params=_cparams("parallel", "arbitrary"),
        name="branch_merge",
    )(orw, oat, wb_r, wb_a, gates, gates)


def _lora_weights(w_up, a_up):
    z = jnp.zeros((DECAY_LORA, RWKV_DIM), F32)
    rows = [jnp.concatenate([w_up[0], z, z, z], axis=1),
            jnp.concatenate([z, w_up[1], z, z], axis=1),
            jnp.concatenate([z, z, a_up[0], z], axis=1),
            jnp.concatenate([z, z, z, a_up[1]], axis=1)]
    return jnp.concatenate(rows, axis=0).astype(BF16)


def kernel(x, norm1_g, w_in, tshift_mu, w0, w_lora_up, a0, a_lora_up, g_lora_up, k_k, k_a, r_k,
           gn_w, gn_b, rel_bias, w_branch_rwkv, w_branch_attn, w_out, norm2_g, w_mlp_in, w_mlp_out,
           final_g):
    B, S, D = x.shape
    M = B * S
    depth = w_in.shape[0]
    c_rkv = 3 * RWKV_DIM
    c_slab = c_rkv + LORA_COLS
    c_attn = c_slab + 3 * ATTN_DIM
    h = x.reshape(M, D)
    for l in range(depth):
        wl = w_in[l]
        g1 = norm1_g[l]
        rkv = norm_matmul(h, g1, _block_cols(wl[:, :c_rkv], RWKV_DIM), tm=1024)
        lora = norm_matmul(h, g1, _block_cols(wl[:, c_rkv:c_slab], LORA_COLS), tm=1024)
        qkv_groups = qkv_proj(h, g1, _block_cols(wl[:, c_slab:c_attn], ATTN_DIM), batch=B, tm=1024)
        gates = norm_matmul(h, g1, _block_cols(wl[:, c_attn:], 1024), tm=1024, act="sigmoid")

        (r, v, kk, lwf, lwb, kaf, kab, kdf, kdb, g, bonus) = rwkv_prep(
            rkv.reshape(3, B, S, RWKV_DIM), lora.reshape(1, B, S, LORA_COLS), tshift_mu[l], w0[l], a0[l],
            _lora_weights(w_lora_up[l], a_lora_up[l]), g_lora_up[l].astype(BF16),
            k_k[l], k_a[l], r_k[l], ts=256)
        yf, yb = rwkv_scan(r, v, kk, lwf, lwb, kaf, kab, kdf, kdb)

        outs, lses = [], []
        for gi, (window, dilation) in enumerate(ATTN_GROUPS):
            assert window // (2 * dilation) == ATTN_HALF
            o, lse = dilated_attention(qkv_groups[gi], rel_bias, gi, tq=128)
            outs.append(o)
            lses.append(lse)

        orw, oat = branch_post(yf.reshape(M, RWKV_DIM), yb.reshape(M, RWKV_DIM),
                               bonus.reshape(M, RWKV_DIM), g.reshape(M, RWKV_DIM),
                               gn_w[l], gn_b[l], outs, lses, tm=512)
        merged = branch_merge(orw, oat, _block_cols(w_branch_rwkv[l], 1024),
                              _block_cols(w_branch_attn[l], 1024), gates, tm=1024)
        h = matmul_residual(merged, w_out[l].astype(BF16), h, tm=512, tk=D)

        act = norm_matmul(h, norm2_g[l], _block_cols(w_mlp_in[l], 1024), tm=1024, act="relu2",
                          out_dtype=BF16)
        h = matmul_residual(act, w_mlp_out[l].astype(BF16), h, tm=512, tk=2048)
    out = rmsnorm_rows(h, final_g, tm=512)
    return out.reshape(B, S, D)
```

```python
import functools
import math

import jax
import jax.numpy as jnp
from jax import lax
from jax.experimental import pallas as pl
from jax.experimental.pallas import tpu as pltpu

F32 = jnp.float32
BF16 = jnp.bfloat16

HEAD_DIM = 64
LANES = 128
RWKV_DIM = 1024
DECAY_LORA = 96
AAA_LORA = 96
GATE_LORA = 256
LORA_COLS = GATE_LORA + 2 * DECAY_LORA + 2 * AAA_LORA
LORA_TAIL = LORA_COLS - GATE_LORA
ATTN_GROUPS = ((128, 1), (512, 4), (2048, 16))
ATTN_SLOTS = 8
ATTN_DIM = 1536
ATTN_OUT_DIM = ATTN_SLOTS * HEAD_DIM
ATTN_HALF = 64
N_BUCKETS = 32
MAX_DISTANCE = 1024
RMS_EPS = 1e-6
GN_EPS = 64e-5
L2_EPS = 1e-12
NEG_BIG = -1e30
CHUNK = 64
VMEM_LIMIT = 56 * 1024 * 1024


def _cparams(*sem):
    return pltpu.CompilerParams(dimension_semantics=sem, vmem_limit_bytes=VMEM_LIMIT)


def _dot(a, b):
    return jnp.dot(a.astype(BF16), b.astype(BF16), preferred_element_type=F32)


def _split3(x):
    p0 = x.astype(BF16)
    r1 = x - p0.astype(F32)
    p1 = r1.astype(BF16)
    p2 = (r1 - p1.astype(F32)).astype(BF16)
    return p0, p1, p2


def _dot_exact_rhs(a_bf16, x):
    return sum(jnp.dot(a_bf16, p, preferred_element_type=F32) for p in _split3(x))


def _head_sum(x, ones_bd):
    return sum(jnp.dot(p, ones_bd, preferred_element_type=F32) for p in _split3(x))


def _head_block_ones(width):
    r = lax.broadcasted_iota(jnp.int32, (width, width), 0) // HEAD_DIM
    c = lax.broadcasted_iota(jnp.int32, (width, width), 1) // HEAD_DIM
    return (r == c).astype(BF16)


def _norm_mm_kernel(x_ref, g_ref, w_ref, o_ref, u_ref, *, act):
    @pl.when(pl.program_id(1) == 0)
    def _():
        x = x_ref[...]
        ms = jnp.mean(x * x, axis=-1, keepdims=True)
        u_ref[...] = (x * lax.rsqrt(ms + RMS_EPS) * g_ref[...]).astype(BF16)

    acc = jnp.dot(u_ref[...], w_ref[...].astype(BF16), preferred_element_type=F32)
    if act == "relu2":
        acc = jnp.square(jnp.maximum(acc, 0.0))
    elif act == "sigmoid":
        acc = jax.nn.sigmoid(acc)
    o_ref[...] = acc.astype(o_ref.dtype)


def norm_matmul(x, g, w, layer, *, n_cols, tm, tn, act=None, out_dtype=F32):
    M, K = x.shape
    return pl.pallas_call(
        functools.partial(_norm_mm_kernel, act=act),
        grid=(M // tm, n_cols // tn),
        in_specs=[pl.BlockSpec((tm, K), lambda i, j: (i, 0)),
                  pl.BlockSpec((None, 1, K), lambda i, j: (layer, 0, 0)),
                  pl.BlockSpec((None, K, tn), lambda i, j: (layer, 0, j))],
        out_specs=pl.BlockSpec((tm, tn), lambda i, j: (i, j)),
        out_shape=jax.ShapeDtypeStruct((M, n_cols), out_dtype),
        scratch_shapes=[pltpu.VMEM((tm, K), BF16)],
        compiler_params=_cparams("parallel", "arbitrary"),
        name="norm_matmul_" + (act or "id"),
    )(x, g.reshape(-1, 1, K), w)


def _qkv_proj_kernel(x_ref, g_ref, w_ref, o0_ref, o1_ref, o2_ref, u_ref, acc_ref):
    @pl.when(pl.program_id(1) == 0)
    def _():
        x = x_ref[...]
        ms = jnp.mean(x * x, axis=-1, keepdims=True)
        u_ref[...] = (x * lax.rsqrt(ms + RMS_EPS) * g_ref[...]).astype(BF16)

    acc = jnp.dot(u_ref[...], w_ref[...], preferred_element_type=F32)
    tm = acc.shape[0]
    tiles_per_group = ATTN_OUT_DIM // LANES
    for gi, (o_ref, (_, d)) in enumerate(zip((o0_ref, o1_ref, o2_ref), ATTN_GROUPS)):
        if d == 1:
            o_ref[0, 0] = acc[:, gi * ATTN_OUT_DIM:(gi + 1) * ATTN_OUT_DIM].astype(BF16)
            continue
        for t in range(tiles_per_group):
            c = gi * tiles_per_group + t
            acc_ref[c] = acc[:, c * LANES:(c + 1) * LANES]
            for res in range(d):
                o_ref[0, res, :, t * LANES:(t + 1) * LANES] = (
                    acc_ref[c, pl.ds(res, tm // d, stride=d), :].astype(BF16))


def qkv_proj(x, g, w, layer, *, batch, tm):
    M, K = x.shape
    S = M // batch
    nb = S // tm
    outs_shape, outs_spec = [], []
    for _, d in ATTN_GROUPS:
        outs_shape.append(jax.ShapeDtypeStruct((batch, d, S // d, 3 * ATTN_OUT_DIM), BF16))
        outs_spec.append(pl.BlockSpec((1, d, tm // d, ATTN_OUT_DIM),
                                      lambda i, j: (i // nb, 0, i % nb, j)))
    return pl.pallas_call(
        _qkv_proj_kernel,
        grid=(M // tm, 3),
        in_specs=[pl.BlockSpec((tm, K), lambda i, j: (i, 0)),
                  pl.BlockSpec((None, 1, K), lambda i, j: (layer, 0, 0)),
                  pl.BlockSpec((None, K, ATTN_DIM), lambda i, j: (layer, 0, j))],
        out_specs=outs_spec,
        out_shape=outs_shape,
        scratch_shapes=[pltpu.VMEM((tm, K), BF16), pltpu.VMEM((ATTN_DIM // LANES, tm, LANES), F32)],
        compiler_params=_cparams("parallel", "arbitrary"),
        name="qkv_proj",
    )(x, g.reshape(-1, 1, K), w)


def _mm_res_kernel(x_ref, w_ref, r_ref, o_ref, acc_ref):
    k = pl.program_id(2)

    @pl.when(k == 0)
    def _():
        acc_ref[...] = jnp.zeros_like(acc_ref)

    acc_ref[...] += jnp.dot(x_ref[...], w_ref[...].astype(BF16), preferred_element_type=F32)

    @pl.when(k == pl.num_programs(2) - 1)
    def _():
        o_ref[...] = r_ref[...] + acc_ref[...]


def matmul_residual(x, w, layer, res, *, tm, tn, tk):
    M, K = x.shape
    N = w.shape[2]
    return pl.pallas_call(
        _mm_res_kernel,
        grid=(M // tm, N // tn, K // tk),
        in_specs=[pl.BlockSpec((tm, tk), lambda i, j, k: (i, k)),
                  pl.BlockSpec((None, tk, tn), lambda i, j, k: (layer, k, j)),
                  pl.BlockSpec((tm, tn), lambda i, j, k: (i, j))],
        out_specs=pl.BlockSpec((tm, tn), lambda i, j, k: (i, j)),
        out_shape=jax.ShapeDtypeStruct((M, N), F32),
        scratch_shapes=[pltpu.VMEM((tm, tn), F32)],
        compiler_params=_cparams("parallel", "parallel", "arbitrary"),
        name="matmul_residual",
    )(x, w, res)


def _rmsnorm_kernel(x_ref, g_ref, o_ref):
    x = x_ref[...]
    ms = jnp.mean(x * x, axis=-1, keepdims=True)
    o_ref[...] = x * lax.rsqrt(ms + RMS_EPS) * g_ref[...]


def rmsnorm_rows(x, g, *, tm):
    M, K = x.shape
    return pl.pallas_call(
        _rmsnorm_kernel,
        grid=(M // tm,),
        in_specs=[pl.BlockSpec((tm, K), lambda i: (i, 0)),
                  pl.BlockSpec((1, K), lambda i: (0, 0))],
        out_specs=pl.BlockSpec((tm, K), lambda i: (i, 0)),
        out_shape=jax.ShapeDtypeStruct((M, K), F32),
        compiler_params=_cparams("parallel"),
        name="final_rmsnorm",
    )(x, g.reshape(1, K))


def _token_shift(main, prev_blk, next_blk, mu, first, last):
    rows = main.shape[0]
    prev_row = jnp.where(first, 0.0, prev_blk[7:8, :])
    next_row = jnp.where(last, 0.0, next_blk[0:1, :])
    rid = lax.broadcasted_iota(jnp.int32, main.shape, 0)
    up = jnp.where(rid == 0, prev_row, pltpu.roll(main, 1, 0))
    dn = jnp.where(rid == rows - 1, next_row, pltpu.roll(main, rows - 1, 0))
    return main + mu * (0.5 * (up + dn) - main)


def _rwkv_prep_kernel(r_m, r_p, r_n, k_m, k_p, k_n, v_m, v_p, v_n, l_m, l_p, l_n,
                      mu_ref, w0_ref, a0_ref, lora_w_ref, g_up_ref, kk_ref, ka_ref, rk_ref,
                      r_o, v_o, kk_o, lwf_o, lwb_o, kaf_o, kab_o, kdf_o, kdb_o, g_o, bonus_o):
    i = pl.program_id(1)
    first = i == 0
    last = i == pl.num_programs(1) - 1
    D = RWKV_DIM
    r = _token_shift(r_m[0], r_p[0], r_n[0], mu_ref[:, 0:D], first, last)
    k = _token_shift(k_m[0], k_p[0], k_n[0], mu_ref[:, D:2 * D], first, last)
    v = _token_shift(v_m[0], v_p[0], v_n[0], mu_ref[:, 2 * D:3 * D], first, last)
    lo = _token_shift(l_m[0], l_p[0], l_n[0], mu_ref[:, 3 * D:], first, last)

    g = _dot(jax.nn.sigmoid(lo[:, :GATE_LORA]), g_up_ref[...])
    tail = lo[:, GATE_LORA:]
    col = lax.broadcasted_iota(jnp.int32, tail.shape, 1)
    tail = jnp.where(col < 2 * DECAY_LORA, jnp.tanh(tail), tail)
    up = _dot(tail, lora_w_ref[...])

    ones_bd = _head_block_ones(2 * LANES)

    def head_sum(x):
        return jnp.concatenate(
            [_head_sum(x[:, c:c + 2 * LANES], ones_bd) for c in range(0, RWKV_DIM, 2 * LANES)], axis=1)

    kk = k * kk_ref[...]
    kk = kk / jnp.maximum(jnp.sqrt(head_sum(kk * kk)), L2_EPS)

    r_o[0] = r
    v_o[0] = v
    kk_o[0] = kk
    g_o[0] = g
    kd_sum = jnp.zeros_like(k)
    for d, (lw_o, ka_o, kd_o) in enumerate(((lwf_o, kaf_o, kdf_o), (lwb_o, kab_o, kdb_o))):
        z = w0_ref[d:d + 1, :] + up[:, d * RWKV_DIM:(d + 1) * RWKV_DIM]
        nz = -z
        softplus = jnp.maximum(nz, 0.0) + jnp.log1p(jnp.exp(-jnp.abs(nz)))
        lw_o[0] = -jnp.exp(-softplus - 0.5)
        a = jax.nn.sigmoid(a0_ref[d:d + 1, :] + up[:, (2 + d) * RWKV_DIM:(3 + d) * RWKV_DIM])
        kd = k * (1.0 + (a - 1.0) * ka_ref[...])
        ka_o[0] = kk * a
        kd_o[0] = kd
        kd_sum = kd_sum + kd
    bonus_o[0] = head_sum(r * kd_sum * rk_ref[...]) * v


def rwkv_prep(rkv, lora, layer, mu, w0, a0, lora_w, g_up, k_k, k_a, r_k, *, ts):
    B, S, _ = rkv.shape
    D = RWKV_DIM
    nblk8 = S // 8

    def main_spec(width, cblk):
        return pl.BlockSpec((1, ts, width), lambda b, i: (b, i, cblk))

    def prev_spec(width, cblk):
        return pl.BlockSpec((1, 8, width), lambda b, i: (b, jnp.maximum(i * (ts // 8) - 1, 0), cblk))

    def next_spec(width, cblk):
        return pl.BlockSpec((1, 8, width), lambda b, i: (b, jnp.minimum((i + 1) * (ts // 8), nblk8 - 1), cblk))

    def layer_spec(rows, width):
        return pl.BlockSpec((None, rows, width), lambda b, i: (layer, 0, 0))

    in_specs = []
    args = []
    for cblk in range(3):
        in_specs += [main_spec(D, cblk), prev_spec(D, cblk), next_spec(D, cblk)]
        args += [rkv, rkv, rkv]
    in_specs += [main_spec(LORA_COLS, 0), prev_spec(LORA_COLS, 0), next_spec(LORA_COLS, 0)]
    args += [lora, lora, lora]
    depth = mu.shape[0]
    in_specs += [layer_spec(1, 3 * D + LORA_COLS), layer_spec(2, D), layer_spec(2, D),
                 layer_spec(LORA_TAIL, 4 * D), layer_spec(GATE_LORA, D),
                 layer_spec(1, D), layer_spec(1, D), layer_spec(1, D)]
    args += [mu.reshape(depth, 1, -1), w0, a0, lora_w, g_up,
             k_k.reshape(depth, 1, D), k_a.reshape(depth, 1, D), r_k.reshape(depth, 1, D)]
    out_spec = pl.BlockSpec((1, ts, D), lambda b, i: (b, i, 0))
    n_out = 11
    return pl.pallas_call(
        _rwkv_prep_kernel,
        grid=(B, S // ts),
        in_specs=in_specs,
        out_specs=[out_spec] * n_out,
        out_shape=[jax.ShapeDtypeStruct((B, S, D), F32)] * n_out,
        compiler_params=_cparams("parallel", "arbitrary"),
        name="rwkv_prep",
    )(*args)


_BMM = (((2,), (1,)), ((0,), (0,)))
_BMM_NT = (((2,), (2,)), ((0,), (0,)))
_BMM_TN = (((1,), (1,)), ((0,), (0,)))


def _bdot(a, b, dims=_BMM):
    return lax.dot_general(a.astype(BF16), b.astype(BF16), dims, preferred_element_type=F32)


def _bd_stack(x, m0):
    zero = jnp.zeros_like(x)
    return jnp.concatenate([jnp.where(m0, x, zero), jnp.where(m0, zero, x)], axis=-2)


def _scan_chunk(r, v, kk, lw, cum, ka, kd, s_bd, consts):
    is_rev, strict2, incl2, eye2, m0_1, m0_2, bd_mask, off_masks = consts
    C = r.shape[1]
    cum_prev = cum - lw
    e_cum = jnp.exp(cum)
    e_prev = jnp.exp(cum_prev)
    e_neg = jnp.exp(-cum)
    total = jnp.where(is_rev, cum[:, 0:1, :], cum[:, C - 1:C, :])
    e_rest = jnp.exp(total - cum)
    r0 = r * e_cum
    at0 = -kk * e_prev
    bt = ka * e_neg
    kt = kd * e_neg
    bh = ka * e_rest
    kh = kd * e_rest

    gram = _bdot(jnp.concatenate([at0, r0], axis=1),
                 jnp.concatenate([_bd_stack(bt, m0_1), _bd_stack(kt, m0_1)], axis=1), _BMM_NT)
    zero = jnp.zeros_like(gram[:, :C])
    a_top = jnp.where(strict2, gram[:, :C], zero)
    a_bot = jnp.where(incl2, gram[:, C:], zero)
    a_ab = a_top[:, :, :LANES]
    a_ak = a_top[:, :, LANES:]

    zero1 = jnp.zeros_like(a_ab)
    t = eye2 + jnp.where(off_masks[0], a_ab, zero1)
    for off_mask in off_masks[1:]:
        z = _bdot(t, _bd_stack(jnp.where(off_mask, a_ab, zero1), m0_1))
        t = t + _bdot(z, _bd_stack(t, m0_1))

    av = _bdot(a_ak, _bd_stack(v, m0_1))
    wu = _bdot(t, _bd_stack(jnp.concatenate([at0, av], axis=2), m0_2))
    w_a = wu[:, :, :LANES]
    u_v = wu[:, :, LANES:]

    wr = _bdot(jnp.concatenate([w_a, r0], axis=1), s_bd, _BMM_NT)
    u = wr[:, :C] + u_v
    y = wr[:, C:] + _bdot(a_bot, jnp.concatenate([_bd_stack(u, m0_1), _bd_stack(v, m0_1)], axis=1))
    upd = _bdot(jnp.concatenate([u, v], axis=1), jnp.concatenate([bh, kh], axis=1), _BMM_TN)
    s_new = s_bd * jnp.exp(total) + jnp.where(bd_mask, upd, jnp.zeros_like(upd))
    return y, s_new


def _scan_consts(C, n_fwd, n_bwd):
    G = n_fwd + n_bwd
    is_rev = lax.broadcasted_iota(jnp.int32, (G, 1, 1), 0) >= n_fwd
    sign = jnp.where(is_rev, -1, 1)
    row2 = lax.broadcasted_iota(jnp.int32, (G, C, 2 * LANES), 1)
    col2 = lax.broadcasted_iota(jnp.int32, (G, C, 2 * LANES), 2) % HEAD_DIM
    strict2 = (col2 - row2) * sign < 0
    incl2 = (col2 - row2) * sign <= 0
    lane1 = lax.broadcasted_iota(jnp.int32, (1, C, LANES), 2)
    lane2 = lax.broadcasted_iota(jnp.int32, (1, C, 2 * LANES), 2) % LANES
    m0_1 = lane1 < HEAD_DIM
    m0_2 = lane2 < HEAD_DIM
    rb = lax.broadcasted_iota(jnp.int32, (1, LANES, LANES), 1) // HEAD_DIM
    cb = lax.broadcasted_iota(jnp.int32, (1, LANES, LANES), 2) // HEAD_DIM
    bd_mask = rb == cb
    row1 = lax.broadcasted_iota(jnp.int32, (1, C, LANES), 1)
    col1 = lane1 % HEAD_DIM
    eye2 = (row1 == col1).astype(F32)
    off_masks = [row1 // 2 == col1 // 2]
    b = 2
    while b < C:
        off_masks.append((row1 // (2 * b) == col1 // (2 * b)) & (row1 // b != col1 // b))
        b *= 2
    return (is_rev, strict2, incl2, eye2, m0_1, m0_2, bd_mask, off_masks)


def _rwkv_scan_kernel(rf, vf, kkf, lwf, kaf, kdf, rb, vb, kkb, lwb, kab, kdb,
                      yf_o, yb_o, state_ref):
    c = pl.program_id(1)

    @pl.when(c == 0)
    def _():
        state_ref[...] = jnp.zeros_like(state_ref)

    C = rf.shape[1]
    n_pairs = rf.shape[2] // LANES
    consts = _scan_consts(C, n_pairs, n_pairs)

    row = lax.broadcasted_iota(jnp.int32, (C, C), 0)
    col = lax.broadcasted_iota(jnp.int32, (C, C), 1)
    cum_f = _dot_exact_rhs((col <= row).astype(F32).astype(BF16), lwf[0])
    cum_b = _dot_exact_rhs((col >= row).astype(F32).astype(BF16), lwb[0])

    def pairs(x_f, x_b):
        return jnp.stack([x[:, p * LANES:(p + 1) * LANES] for x in (x_f, x_b) for p in range(n_pairs)],
                         axis=0)

    r, v, kk, lw, ka, kd = (pairs(f[0], b[0]) for f, b in
                            ((rf, rb), (vf, vb), (kkf, kkb), (lwf, lwb), (kaf, kab), (kdf, kdb)))
    y, s_new = _scan_chunk(r, v, kk, lw, pairs(cum_f, cum_b), ka, kd, state_ref[...], consts)
    state_ref[...] = s_new
    for d, y_o in enumerate((yf_o, yb_o)):
        for p in range(n_pairs):
            y_o[0, :, p * LANES:(p + 1) * LANES] = y[d * n_pairs + p]


def rwkv_scan(r, v, kk, lwf, lwb, kaf, kab, kdf, kdb):
    B, S, D = r.shape
    C = CHUNK
    nc = S // C
    fwd = pl.BlockSpec((1, C, D), lambda b, c: (b, c, 0))
    bwd = pl.BlockSpec((1, C, D), lambda b, c: (b, nc - 1 - c, 0))
    return pl.pallas_call(
        _rwkv_scan_kernel,
        grid=(B, nc),
        in_specs=[fwd] * 6 + [bwd] * 6,
        out_specs=[fwd, bwd],
        out_shape=[jax.ShapeDtypeStruct((B, S, D), F32)] * 2,
        scratch_shapes=[pltpu.VMEM((2 * (D // LANES), LANES, LANES), F32)],
        compiler_params=_cparams("parallel", "arbitrary"),
        name="rwkv_scan",
    )(r, v, kk, lwf, kaf, kdf, r, v, kk, lwb, kab, kdb)


def _attn_kernel(table_ref, bucket_ref, q_ref, kp_ref, km_ref, kn_ref, vp_ref, vm_ref, vn_ref,
                 o_ref, lse_ref, bias_ref, *, nblk):
    i = pl.program_id(2)
    TQ = q_ref.shape[0]
    TK = TQ + 2 * ATTN_HALF

    @pl.when((pl.program_id(0) == 0) & (pl.program_id(1) == 0) & (i == 0))
    def _():
        bucket = bucket_ref[...]
        row = lax.broadcasted_iota(jnp.int32, (TQ, TK), 0)
        colk = lax.broadcasted_iota(jnp.int32, (TQ, TK), 1)
        band = jnp.abs(colk - ATTN_HALF - row) <= ATTN_HALF
        for h in range(ATTN_SLOTS):
            acc = jnp.zeros((TQ, TK), F32)
            for b in range(N_BUCKETS):
                acc = jnp.where(bucket == b, table_ref[h * N_BUCKETS + b], acc)
            bias_ref[h] = jnp.where(band, acc, NEG_BIG)

    col = lax.broadcasted_iota(jnp.int32, (TQ, TK), 1)
    edge_ok = ((col >= ATTN_HALF) | (i > 0)) & ((col < TQ + ATTN_HALF) | (i < nblk - 1))
    m0 = lax.broadcasted_iota(jnp.int32, (TQ, LANES), 1) < HEAD_DIM
    for p in range(ATTN_OUT_DIM // LANES):
        sl = slice(p * LANES, (p + 1) * LANES)
        q2 = q_ref[:, sl]
        kwin = jnp.concatenate([kp_ref[:, sl], km_ref[:, sl], kn_ref[:, sl]], axis=0)
        vwin = jnp.concatenate([vp_ref[:, sl], vm_ref[:, sl], vn_ref[:, sl]], axis=0)
        outs, lses = [], []
        for hh in range(2):
            qm = jnp.where(m0 if hh == 0 else ~m0, q2, jnp.zeros_like(q2))
            s = lax.dot_general(qm, kwin, (((1,), (1,)), ((), ())), preferred_element_type=F32)
            s = s * (HEAD_DIM ** -0.5) + bias_ref[2 * p + hh]
            s = jnp.where(edge_ok, s, NEG_BIG)
            m = jnp.max(s, axis=-1, keepdims=True)
            e = jnp.exp(s - m)
            den = jnp.sum(e, axis=-1, keepdims=True)
            pv = jnp.dot(e.astype(BF16), vwin, preferred_element_type=F32)
            outs.append(pv / den)
            lses.append(jnp.broadcast_to(m + jnp.log(den), (TQ, LANES)))
        o_ref[:, sl] = jnp.where(m0, outs[0], outs[1])
        lse_ref[:, sl] = jnp.where(m0, lses[0], lses[1])


def _t5_bucket(rel):
    nb = N_BUCKETS // 2
    max_exact = nb // 2
    ret = jnp.where(rel > 0, nb, 0)
    n = jnp.abs(rel)
    nf = jnp.maximum(n, 1).astype(jnp.float32)
    large = max_exact + (jnp.log(nf / max_exact) / math.log(MAX_DISTANCE / max_exact)
                         * (nb - max_exact)).astype(jnp.int32)
    large = jnp.minimum(large, nb - 1)
    return ret + jnp.where(n < max_exact, n, large)


def attention_bias_inputs(table, gi, dilation, *, tq):
    tk = tq + 2 * ATTN_HALF
    rel = jnp.arange(tk)[None, :] - ATTN_HALF - jnp.arange(tq)[:, None]
    bucket = _t5_bucket(rel * dilation).astype(jnp.int32)
    tbl = table[:, gi * ATTN_SLOTS:(gi + 1) * ATTN_SLOTS].astype(F32).T.reshape(-1)
    return tbl, bucket


def dilated_attention(qkv, tbl, bucket, gi, *, tq):
    B, dilation, L, _ = qkv.shape
    nblk = L // tq
    hb = tq // ATTN_HALF
    nhalf = L // ATTN_HALF
    W = ATTN_OUT_DIM

    def main(off):
        return pl.BlockSpec((None, None, tq, W), lambda b, r, i: (b, r, i, off))

    def prev(off):
        return pl.BlockSpec((None, None, ATTN_HALF, W),
                            lambda b, r, i: (b, r, jnp.maximum(i * hb - 1, 0), off))

    def nxt(off):
        return pl.BlockSpec((None, None, ATTN_HALF, W),
                            lambda b, r, i: (b, r, jnp.minimum((i + 1) * hb, nhalf - 1), off))

    tk = tq + 2 * ATTN_HALF
    out_spec = pl.BlockSpec((None, None, tq, W), lambda b, r, i: (b, r, i, 0))
    return pl.pallas_call(
        functools.partial(_attn_kernel, nblk=nblk),
        grid=(B, dilation, nblk),
        in_specs=[pl.BlockSpec(memory_space=pltpu.SMEM),
                  pl.BlockSpec((tq, tk), lambda b, r, i: (0, 0)),
                  main(0), prev(1), main(1), nxt(1), prev(2), main(2), nxt(2)],
        out_specs=[out_spec, out_spec],
        out_shape=[jax.ShapeDtypeStruct((B, dilation, L, W), F32)] * 2,
        scratch_shapes=[pltpu.VMEM((ATTN_SLOTS, tq, tk), F32)],
        compiler_params=_cparams("arbitrary", "arbitrary", "arbitrary"),
        name=f"dilated_attn_g{gi}",
    )(tbl, bucket, qkv, qkv, qkv, qkv, qkv, qkv, qkv)


def _branch_post_kernel(yf_ref, yb_ref, bonus_ref, g_ref, gnw_ref, gnb_ref,
                        o0, o1, o2, l0, l1, l2, orw_ref, oat_ref, *scratch):
    y = yf_ref[...] + yb_ref[...]
    ones_bd = _head_block_ones(2 * LANES)
    tm, width = y.shape

    def head_mean(x):
        return jnp.concatenate(
            [_head_sum(x[:, c:c + 2 * LANES], ones_bd) for c in range(0, width, 2 * LANES)],
            axis=1) * (1.0 / HEAD_DIM)

    mean = head_mean(y)
    yc = y - mean
    var = head_mean(yc * yc)
    yn = yc * lax.rsqrt(var + GN_EPS) * gnw_ref[...] + gnb_ref[...]
    orw_ref[...] = ((yn + bonus_ref[...]) * g_ref[...]).astype(BF16)

    def natural_order(ref, scr):
        d = ref.shape[0]
        for t in range(ATTN_OUT_DIM // LANES):
            for res in range(d):
                scr[t, pl.ds(res, tm // d, stride=d), :] = ref[res, :, t * LANES:(t + 1) * LANES]
        return jnp.concatenate([scr[t] for t in range(ATTN_OUT_DIM // LANES)], axis=1)

    s_o1, s_o2, s_l1, s_l2 = scratch
    oa, ob, oc = o0[0], natural_order(o1, s_o1), natural_order(o2, s_o2)
    la, lb, lc = l0[0], natural_order(l1, s_l1), natural_order(l2, s_l2)
    m = jnp.maximum(jnp.maximum(la, lb), lc)
    ea, eb, ec = jnp.exp(la - m), jnp.exp(lb - m), jnp.exp(lc - m)
    den = ea + eb + ec
    oat_ref[...] = ((ea * oa + eb * ob + ec * oc) / den).astype(BF16)


def branch_post(yf, yb, bonus, g, gn_w, gn_b, layer, outs, lses, *, tm):
    M, D = yf.shape
    A = ATTN_OUT_DIM
    batch = outs[0].shape[0]
    nb = M // batch // tm
    big = pl.BlockSpec((tm, D), lambda b, i: (b * nb + i, 0))
    rowp = pl.BlockSpec((None, 1, D), lambda b, i: (layer, 0, 0))
    attn_specs = [pl.BlockSpec((None, o.shape[1], tm // o.shape[1], A), lambda b, i: (b, 0, i, 0))
                  for o in outs]
    return pl.pallas_call(
        _branch_post_kernel,
        grid=(batch, nb),
        in_specs=[big] * 4 + [rowp, rowp] + attn_specs + attn_specs,
        out_specs=[big, pl.BlockSpec((tm, A), lambda b, i: (b * nb + i, 0))],
        out_shape=[jax.ShapeDtypeStruct((M, D), BF16), jax.ShapeDtypeStruct((M, A), BF16)],
        scratch_shapes=[pltpu.VMEM((A // LANES, tm, LANES), F32)] * 4,
        compiler_params=_cparams("parallel", "arbitrary"),
        name="branch_post",
    )(yf, yb, bonus, g, gn_w.reshape(-1, 1, D), gn_b.reshape(-1, 1, D), *outs, *lses)


def _merge_kernel(orw_ref, oat_ref, wr_ref, wa_ref, gr_ref, ga_ref, o_ref):
    a = jnp.dot(orw_ref[...], wr_ref[...].astype(BF16), preferred_element_type=F32)
    b = jnp.dot(oat_ref[...], wa_ref[...].astype(BF16), preferred_element_type=F32)
    o_ref[...] = (gr_ref[...] * a + ga_ref[...] * b).astype(o_ref.dtype)


def branch_merge(orw, oat, w_r, w_a, layer, gates, *, tm, tn):
    M = orw.shape[0]
    N = w_r.shape[2]
    nj = N // tn
    return pl.pallas_call(
        _merge_kernel,
        grid=(M // tm, nj),
        in_specs=[pl.BlockSpec((tm, orw.shape[1]), lambda i, j: (i, 0)),
                  pl.BlockSpec((tm, oat.shape[1]), lambda i, j: (i, 0)),
                  pl.BlockSpec((None, w_r.shape[1], tn), lambda i, j: (layer, 0, j)),
                  pl.BlockSpec((None, w_a.shape[1], tn), lambda i, j: (layer, 0, j)),
                  pl.BlockSpec((tm, tn), lambda i, j: (i, j)),
                  pl.BlockSpec((tm, tn), lambda i, j: (i, j + nj))],
        out_specs=pl.BlockSpec((tm, tn), lambda i, j: (i, j)),
        out_shape=jax.ShapeDtypeStruct((M, N), BF16),
        compiler_params=_cparams("parallel", "arbitrary"),
        name="branch_merge",
    )(orw, oat, w_r, w_a, gates, gates)


def _lora_weights(w_up, a_up):
    z = jnp.zeros_like(w_up[:, 0])
    rows = [jnp.concatenate([w_up[:, 0], z, z, z], axis=2),
            jnp.concatenate([z, w_up[:, 1], z, z], axis=2),
            jnp.concatenate([z, z, a_up[:, 0], z], axis=2),
            jnp.concatenate([z, z, z, a_up[:, 1]], axis=2)]
    return jnp.concatenate(rows, axis=1).astype(BF16)


def kernel(x, norm1_g, w_in, tshift_mu, w0, w_lora_up, a0, a_lora_up, g_lora_up, k_k, k_a, r_k,
           gn_w, gn_b, rel_bias, w_branch_rwkv, w_branch_attn, w_out, norm2_g, w_mlp_in, w_mlp_out,
           final_g):
    B, S, D = x.shape
    M = B * S
    depth = w_in.shape[0]
    c_rkv = 3 * RWKV_DIM
    c_slab = c_rkv + LORA_COLS
    c_attn = c_slab + 3 * ATTN_DIM
    w_lora = w_in[:, :, c_rkv:c_slab].astype(BF16)
    w_qkv = w_in[:, :, c_slab:c_attn].astype(BF16)
    w_gates = w_in[:, :, c_attn:].astype(BF16)
    w_lora_up = _lora_weights(w_lora_up, a_lora_up)
    g_up = g_lora_up.astype(BF16)
    bias_inputs = [attention_bias_inputs(rel_bias, gi, dilation, tq=128)
                   for gi, (_, dilation) in enumerate(ATTN_GROUPS)]
    h = x.reshape(M, D)
    for l in range(depth):
        g1 = norm1_g
        rkv = norm_matmul(h, g1, w_in, l, n_cols=c_rkv, tm=1024, tn=1024)
        lora = norm_matmul(h, g1, w_lora, l, n_cols=LORA_COLS, tm=1024, tn=LORA_COLS)
        qkv_groups = qkv_proj(h, g1, w_qkv, l, batch=B, tm=1024)
        gates = norm_matmul(h, g1, w_gates, l, n_cols=2 * D, tm=1024, tn=1024, act="sigmoid")

        (r, v, kk, lwf, lwb, kaf, kab, kdf, kdb, g, bonus) = rwkv_prep(
            rkv.reshape(B, S, c_rkv), lora.reshape(B, S, LORA_COLS), l, tshift_mu, w0, a0,
            w_lora_up, g_up, k_k, k_a, r_k, ts=256)
        yf, yb = rwkv_scan(r, v, kk, lwf, lwb, kaf, kab, kdf, kdb)

        outs, lses = [], []
        for gi, (window, dilation) in enumerate(ATTN_GROUPS):
            assert window // (2 * dilation) == ATTN_HALF
            o, lse = dilated_attention(qkv_groups[gi], *bias_inputs[gi], gi, tq=128)
            outs.append(o)
            lses.append(lse)

        orw, oat = branch_post(yf.reshape(M, RWKV_DIM), yb.reshape(M, RWKV_DIM),
                               bonus.reshape(M, RWKV_DIM), g.reshape(M, RWKV_DIM),
                               gn_w, gn_b, l, outs, lses, tm=512)
        merged = branch_merge(orw, oat, w_branch_rwkv, w_branch_attn, l, gates, tm=1024, tn=1024)
        h = matmul_residual(merged, w_out, l, h, tm=1024, tn=1024, tk=D)

        act = norm_matmul(h, norm2_g, w_mlp_in, l, n_cols=w_mlp_in.shape[2], tm=1024, tn=1024,
                          act="relu2", out_dtype=BF16)
        h = matmul_residual(act, w_mlp_out, l, h, tm=1024, tn=1024, tk=2048)
    out = rmsnorm_rows(h, final_g, tm=512)
    return out.reshape(B, S, D)
```

```python
import functools
import math

import jax
import jax.numpy as jnp
from jax import lax
from jax.experimental import pallas as pl
from jax.experimental.pallas import tpu as pltpu

F32 = jnp.float32
BF16 = jnp.bfloat16

HEAD_DIM = 64
LANES = 128
RWKV_DIM = 1024
DECAY_LORA = 96
AAA_LORA = 96
GATE_LORA = 256
LORA_COLS = GATE_LORA + 2 * DECAY_LORA + 2 * AAA_LORA
LORA_TAIL = LORA_COLS - GATE_LORA
ATTN_GROUPS = ((128, 1), (512, 4), (2048, 16))
ATTN_SLOTS = 8
ATTN_DIM = 1536
ATTN_OUT_DIM = ATTN_SLOTS * HEAD_DIM
ATTN_HALF = 64
N_BUCKETS = 32
MAX_DISTANCE = 1024
RMS_EPS = 1e-6
GN_EPS = 64e-5
L2_EPS = 1e-12
NEG_BIG = -1e30
CHUNK = 64
VMEM_LIMIT = 56 * 1024 * 1024


def _cparams(*sem):
    return pltpu.CompilerParams(dimension_semantics=sem, vmem_limit_bytes=VMEM_LIMIT)


def _dot(a, b):
    return jnp.dot(a.astype(BF16), b.astype(BF16), preferred_element_type=F32)


def _split3(x):
    p0 = x.astype(BF16)
    r1 = x - p0.astype(F32)
    p1 = r1.astype(BF16)
    p2 = (r1 - p1.astype(F32)).astype(BF16)
    return p0, p1, p2


def _dot_exact_rhs(a_bf16, x):
    return sum(jnp.dot(a_bf16, p, preferred_element_type=F32) for p in _split3(x))


def _head_sum(x, ones_bd):
    return sum(jnp.dot(p, ones_bd, preferred_element_type=F32) for p in _split3(x))


def _head_block_ones(width):
    r = lax.broadcasted_iota(jnp.int32, (width, width), 0) // HEAD_DIM
    c = lax.broadcasted_iota(jnp.int32, (width, width), 1) // HEAD_DIM
    return (r == c).astype(BF16)


def _norm_mm_kernel(x_ref, g_ref, w_ref, o_ref, u_ref, *, act):
    @pl.when(pl.program_id(1) == 0)
    def _():
        x = x_ref[...]
        ms = jnp.mean(x * x, axis=-1, keepdims=True)
        u_ref[...] = (x * lax.rsqrt(ms + RMS_EPS) * g_ref[...]).astype(BF16)

    acc = jnp.dot(u_ref[...], w_ref[...].astype(BF16), preferred_element_type=F32)
    if act == "relu2":
        acc = jnp.square(jnp.maximum(acc, 0.0))
    elif act == "sigmoid":
        acc = jax.nn.sigmoid(acc)
    o_ref[...] = acc.astype(o_ref.dtype)


def norm_matmul(x, g, w, layer, *, n_cols, tm, tn, act=None, out_dtype=F32):
    M, K = x.shape
    return pl.pallas_call(
        functools.partial(_norm_mm_kernel, act=act),
        grid=(M // tm, n_cols // tn),
        in_specs=[pl.BlockSpec((tm, K), lambda i, j: (i, 0)),
                  pl.BlockSpec((None, 1, K), lambda i, j: (layer, 0, 0)),
                  pl.BlockSpec((None, K, tn), lambda i, j: (layer, 0, j))],
        out_specs=pl.BlockSpec((tm, tn), lambda i, j: (i, j)),
        out_shape=jax.ShapeDtypeStruct((M, n_cols), out_dtype),
        scratch_shapes=[pltpu.VMEM((tm, K), BF16)],
        compiler_params=_cparams("parallel", "arbitrary"),
        name="norm_matmul_" + (act or "id"),
    )(x, g.reshape(-1, 1, K), w)


def _qkv_proj_kernel(x_ref, g_ref, w_ref, o0_ref, o1_ref, o2_ref, u_ref, acc_ref):
    @pl.when(pl.program_id(1) == 0)
    def _():
        x = x_ref[...]
        ms = jnp.mean(x * x, axis=-1, keepdims=True)
        u_ref[...] = (x * lax.rsqrt(ms + RMS_EPS) * g_ref[...]).astype(BF16)

    acc = jnp.dot(u_ref[...], w_ref[...], preferred_element_type=F32)
    acc = acc * jnp.where(pl.program_id(1) == 0, HEAD_DIM ** -0.5, 1.0)
    tm = acc.shape[0]
    tiles_per_group = ATTN_OUT_DIM // LANES
    for gi, (o_ref, (_, d)) in enumerate(zip((o0_ref, o1_ref, o2_ref), ATTN_GROUPS)):
        if d == 1:
            o_ref[0, 0] = acc[:, gi * ATTN_OUT_DIM:(gi + 1) * ATTN_OUT_DIM].astype(BF16)
            continue
        for t in range(tiles_per_group):
            c = gi * tiles_per_group + t
            acc_ref[c] = acc[:, c * LANES:(c + 1) * LANES]
            for res in range(d):
                o_ref[0, res, :, t * LANES:(t + 1) * LANES] = (
                    acc_ref[c, pl.ds(res, tm // d, stride=d), :].astype(BF16))


def qkv_proj(x, g, w, layer, *, batch, tm):
    M, K = x.shape
    S = M // batch
    nb = S // tm
    outs_shape, outs_spec = [], []
    for _, d in ATTN_GROUPS:
        outs_shape.append(jax.ShapeDtypeStruct((batch, d, S // d, 3 * ATTN_OUT_DIM), BF16))
        outs_spec.append(pl.BlockSpec((1, d, tm // d, ATTN_OUT_DIM),
                                      lambda i, j: (i // nb, 0, i % nb, j)))
    return pl.pallas_call(
        _qkv_proj_kernel,
        grid=(M // tm, 3),
        in_specs=[pl.BlockSpec((tm, K), lambda i, j: (i, 0)),
                  pl.BlockSpec((None, 1, K), lambda i, j: (layer, 0, 0)),
                  pl.BlockSpec((None, K, ATTN_DIM), lambda i, j: (layer, 0, j))],
        out_specs=outs_spec,
        out_shape=outs_shape,
        scratch_shapes=[pltpu.VMEM((tm, K), BF16), pltpu.VMEM((ATTN_DIM // LANES, tm, LANES), F32)],
        compiler_params=_cparams("parallel", "arbitrary"),
        name="qkv_proj",
    )(x, g.reshape(-1, 1, K), w)


def _mm_res_kernel(x_ref, w_ref, r_ref, o_ref, acc_ref):
    k = pl.program_id(2)

    @pl.when(k == 0)
    def _():
        acc_ref[...] = jnp.zeros_like(acc_ref)

    acc_ref[...] += jnp.dot(x_ref[...], w_ref[...].astype(BF16), preferred_element_type=F32)

    @pl.when(k == pl.num_programs(2) - 1)
    def _():
        o_ref[...] = r_ref[...] + acc_ref[...]


def matmul_residual(x, w, layer, res, *, tm, tn, tk):
    M, K = x.shape
    N = w.shape[2]
    return pl.pallas_call(
        _mm_res_kernel,
        grid=(M // tm, N // tn, K // tk),
        in_specs=[pl.BlockSpec((tm, tk), lambda i, j, k: (i, k)),
                  pl.BlockSpec((None, tk, tn), lambda i, j, k: (layer, k, j)),
                  pl.BlockSpec((tm, tn), lambda i, j, k: (i, j))],
        out_specs=pl.BlockSpec((tm, tn), lambda i, j, k: (i, j)),
        out_shape=jax.ShapeDtypeStruct((M, N), F32),
        scratch_shapes=[pltpu.VMEM((tm, tn), F32)],
        compiler_params=_cparams("parallel", "parallel", "arbitrary"),
        name="matmul_residual",
    )(x, w, res)


def _rmsnorm_kernel(x_ref, g_ref, o_ref):
    x = x_ref[...]
    ms = jnp.mean(x * x, axis=-1, keepdims=True)
    o_ref[...] = x * lax.rsqrt(ms + RMS_EPS) * g_ref[...]


def rmsnorm_rows(x, g, *, tm):
    M, K = x.shape
    return pl.pallas_call(
        _rmsnorm_kernel,
        grid=(M // tm,),
        in_specs=[pl.BlockSpec((tm, K), lambda i: (i, 0)),
                  pl.BlockSpec((1, K), lambda i: (0, 0))],
        out_specs=pl.BlockSpec((tm, K), lambda i: (i, 0)),
        out_shape=jax.ShapeDtypeStruct((M, K), F32),
        compiler_params=_cparams("parallel"),
        name="final_rmsnorm",
    )(x, g.reshape(1, K))


def _token_shift(main, prev_blk, next_blk, mu, first, last):
    rows = main.shape[0]
    prev_row = jnp.where(first, 0.0, prev_blk[7:8, :])
    next_row = jnp.where(last, 0.0, next_blk[0:1, :])
    rid = lax.broadcasted_iota(jnp.int32, main.shape, 0)
    up = jnp.where(rid == 0, prev_row, pltpu.roll(main, 1, 0))
    dn = jnp.where(rid == rows - 1, next_row, pltpu.roll(main, rows - 1, 0))
    return main + mu * (0.5 * (up + dn) - main)


def _rwkv_prep_kernel(r_m, r_p, r_n, k_m, k_p, k_n, v_m, v_p, v_n, l_m, l_p, l_n,
                      mu_ref, w0_ref, a0_ref, lora_w_ref, g_up_ref, kk_ref, ka_ref, rk_ref,
                      r_o, v_o, kk_o, lwf_o, lwb_o, kaf_o, kab_o, kdf_o, kdb_o, g_o, bonus_o):
    i = pl.program_id(1)
    first = i == 0
    last = i == pl.num_programs(1) - 1
    D = RWKV_DIM
    r = _token_shift(r_m[0], r_p[0], r_n[0], mu_ref[:, 0:D], first, last)
    k = _token_shift(k_m[0], k_p[0], k_n[0], mu_ref[:, D:2 * D], first, last)
    v = _token_shift(v_m[0], v_p[0], v_n[0], mu_ref[:, 2 * D:3 * D], first, last)
    lo = _token_shift(l_m[0], l_p[0], l_n[0], mu_ref[:, 3 * D:], first, last)

    g = _dot(jax.nn.sigmoid(lo[:, :GATE_LORA]), g_up_ref[...])
    tail = lo[:, GATE_LORA:]
    col = lax.broadcasted_iota(jnp.int32, tail.shape, 1)
    tail = jnp.where(col < 2 * DECAY_LORA, jnp.tanh(tail), tail)
    up = _dot(tail, lora_w_ref[...])

    ones_bd = _head_block_ones(2 * LANES)

    def head_sum(x):
        return jnp.concatenate(
            [_head_sum(x[:, c:c + 2 * LANES], ones_bd) for c in range(0, RWKV_DIM, 2 * LANES)], axis=1)

    kk = k * kk_ref[...]
    kk = kk / jnp.maximum(jnp.sqrt(head_sum(kk * kk)), L2_EPS)

    r_o[0] = r.astype(r_o.dtype)
    v_o[0] = v.astype(v_o.dtype)
    kk_o[0] = kk.astype(kk_o.dtype)
    g_o[0] = g
    kd_sum = jnp.zeros_like(k)
    for d, (lw_o, ka_o, kd_o) in enumerate(((lwf_o, kaf_o, kdf_o), (lwb_o, kab_o, kdb_o))):
        z = w0_ref[d:d + 1, :] + up[:, d * RWKV_DIM:(d + 1) * RWKV_DIM]
        nz = -z
        softplus = jnp.maximum(nz, 0.0) + jnp.log1p(jnp.exp(-jnp.abs(nz)))
        lw_o[0] = -jnp.exp(-softplus - 0.5)
        a = jax.nn.sigmoid(a0_ref[d:d + 1, :] + up[:, (2 + d) * RWKV_DIM:(3 + d) * RWKV_DIM])
        kd = k * (1.0 + (a - 1.0) * ka_ref[...])
        ka_o[0] = (kk * a).astype(ka_o.dtype)
        kd_o[0] = kd.astype(kd_o.dtype)
        kd_sum = kd_sum + kd
    bonus_o[0] = head_sum(r * kd_sum * rk_ref[...]) * v


def rwkv_prep(rkv, lora, layer, mu, w0, a0, lora_w, g_up, k_k, k_a, r_k, *, ts):
    B, S, _ = rkv.shape
    D = RWKV_DIM
    nblk8 = S // 8

    def main_spec(width, cblk):
        return pl.BlockSpec((1, ts, width), lambda b, i: (b, i, cblk))

    def prev_spec(width, cblk):
        return pl.BlockSpec((1, 8, width), lambda b, i: (b, jnp.maximum(i * (ts // 8) - 1, 0), cblk))

    def next_spec(width, cblk):
        return pl.BlockSpec((1, 8, width), lambda b, i: (b, jnp.minimum((i + 1) * (ts // 8), nblk8 - 1), cblk))

    def layer_spec(rows, width):
        return pl.BlockSpec((None, rows, width), lambda b, i: (layer, 0, 0))

    in_specs = []
    args = []
    for cblk in range(3):
        in_specs += [main_spec(D, cblk), prev_spec(D, cblk), next_spec(D, cblk)]
        args += [rkv, rkv, rkv]
    in_specs += [main_spec(LORA_COLS, 0), prev_spec(LORA_COLS, 0), next_spec(LORA_COLS, 0)]
    args += [lora, lora, lora]
    depth = mu.shape[0]
    in_specs += [layer_spec(1, 3 * D + LORA_COLS), layer_spec(2, D), layer_spec(2, D),
                 layer_spec(LORA_TAIL, 4 * D), layer_spec(GATE_LORA, D),
                 layer_spec(1, D), layer_spec(1, D), layer_spec(1, D)]
    args += [mu.reshape(depth, 1, -1), w0, a0, lora_w, g_up,
             k_k.reshape(depth, 1, D), k_a.reshape(depth, 1, D), r_k.reshape(depth, 1, D)]
    out_spec = pl.BlockSpec((1, ts, D), lambda b, i: (b, i, 0))
    out_dtypes = [BF16, BF16, BF16, F32, F32, BF16, BF16, BF16, BF16, F32, F32]
    n_out = len(out_dtypes)
    return pl.pallas_call(
        _rwkv_prep_kernel,
        grid=(B, S // ts),
        in_specs=in_specs,
        out_specs=[out_spec] * n_out,
        out_shape=[jax.ShapeDtypeStruct((B, S, D), dt) for dt in out_dtypes],
        compiler_params=_cparams("parallel", "arbitrary"),
        name="rwkv_prep",
    )(*args)


_BMM = (((2,), (1,)), ((0,), (0,)))
_BMM_NT = (((2,), (2,)), ((0,), (0,)))
_BMM_TN = (((1,), (1,)), ((0,), (0,)))


def _bdot(a, b, dims=_BMM):
    return lax.dot_general(a.astype(BF16), b.astype(BF16), dims, preferred_element_type=F32)


def _bd_stack(x, m0):
    zero = jnp.zeros_like(x)
    return jnp.concatenate([jnp.where(m0, x, zero), jnp.where(m0, zero, x)], axis=-2)


def _scan_chunk(r, v, kk, lw, cum, ka, kd, s_bd, consts):
    is_rev, strict2, incl2, eye2, m0_1, m0_2, bd_mask, off_masks = consts
    C = r.shape[1]
    cum_prev = cum - lw
    e_cum = jnp.exp(cum)
    e_prev = jnp.exp(cum_prev)
    e_neg = jnp.exp(-cum)
    total = jnp.where(is_rev, cum[:, 0:1, :], cum[:, C - 1:C, :])
    e_rest = jnp.exp(total - cum)
    r0 = r * e_cum
    at0 = -kk * e_prev
    bt = ka * e_neg
    kt = kd * e_neg
    bh = ka * e_rest
    kh = kd * e_rest

    gram = _bdot(jnp.concatenate([at0, r0], axis=1),
                 jnp.concatenate([_bd_stack(bt, m0_1), _bd_stack(kt, m0_1)], axis=1), _BMM_NT)
    zero = jnp.zeros_like(gram[:, :C])
    a_top = jnp.where(strict2, gram[:, :C], zero)
    a_bot = jnp.where(incl2, gram[:, C:], zero)
    a_ab = a_top[:, :, :LANES]
    a_ak = a_top[:, :, LANES:]

    zero1 = jnp.zeros_like(a_ab)
    t = eye2 + jnp.where(off_masks[0], a_ab, zero1)
    for off_mask in off_masks[1:]:
        z = _bdot(t, _bd_stack(jnp.where(off_mask, a_ab, zero1), m0_1))
        t = t + _bdot(z, _bd_stack(t, m0_1))

    av = _bdot(a_ak, _bd_stack(v, m0_1))
    wu = _bdot(t, _bd_stack(jnp.concatenate([at0, av], axis=2), m0_2))
    w_a = wu[:, :, :LANES]
    u_v = wu[:, :, LANES:]

    wr = _bdot(jnp.concatenate([w_a, r0], axis=1), s_bd, _BMM_NT)
    u = wr[:, :C] + u_v
    y = wr[:, C:] + _bdot(a_bot, jnp.concatenate([_bd_stack(u, m0_1), _bd_stack(v, m0_1)], axis=1))
    upd = _bdot(jnp.concatenate([u, v], axis=1), jnp.concatenate([bh, kh], axis=1), _BMM_TN)
    s_new = s_bd * jnp.exp(total) + jnp.where(bd_mask, upd, jnp.zeros_like(upd))
    return y, s_new


def _scan_consts(C, n_fwd, n_bwd):
    G = n_fwd + n_bwd
    is_rev = lax.broadcasted_iota(jnp.int32, (G, 1, 1), 0) >= n_fwd
    sign = jnp.where(is_rev, -1, 1)
    row2 = lax.broadcasted_iota(jnp.int32, (G, C, 2 * LANES), 1)
    col2 = lax.broadcasted_iota(jnp.int32, (G, C, 2 * LANES), 2) % HEAD_DIM
    strict2 = (col2 - row2) * sign < 0
    incl2 = (col2 - row2) * sign <= 0
    lane1 = lax.broadcasted_iota(jnp.int32, (1, C, LANES), 2)
    lane2 = lax.broadcasted_iota(jnp.int32, (1, C, 2 * LANES), 2) % LANES
    m0_1 = lane1 < HEAD_DIM
    m0_2 = lane2 < HEAD_DIM
    rb = lax.broadcasted_iota(jnp.int32, (1, LANES, LANES), 1) // HEAD_DIM
    cb = lax.broadcasted_iota(jnp.int32, (1, LANES, LANES), 2) // HEAD_DIM
    bd_mask = rb == cb
    row1 = lax.broadcasted_iota(jnp.int32, (1, C, LANES), 1)
    col1 = lane1 % HEAD_DIM
    eye2 = (row1 == col1).astype(F32)
    off_masks = [row1 // 2 == col1 // 2]
    b = 2
    while b < C:
        off_masks.append((row1 // (2 * b) == col1 // (2 * b)) & (row1 // b != col1 // b))
        b *= 2
    return (is_rev, strict2, incl2, eye2, m0_1, m0_2, bd_mask, off_masks)


def _rwkv_scan_kernel(rf, vf, kkf, lwf, kaf, kdf, rb, vb, kkb, lwb, kab, kdb,
                      yf_o, yb_o, state_ref):
    c = pl.program_id(1)

    @pl.when(c == 0)
    def _():
        state_ref[...] = jnp.zeros_like(state_ref)

    C = rf.shape[1]
    n_pairs = rf.shape[2] // LANES
    consts = _scan_consts(C, n_pairs, n_pairs)

    row = lax.broadcasted_iota(jnp.int32, (C, C), 0)
    col = lax.broadcasted_iota(jnp.int32, (C, C), 1)
    cum_f = _dot_exact_rhs((col <= row).astype(F32).astype(BF16), lwf[0])
    cum_b = _dot_exact_rhs((col >= row).astype(F32).astype(BF16), lwb[0])

    def pairs(x_f, x_b):
        return jnp.stack([x[:, p * LANES:(p + 1) * LANES] for x in (x_f, x_b) for p in range(n_pairs)],
                         axis=0)

    r, v, kk, lw, ka, kd = (pairs(f[0].astype(F32), b[0].astype(F32)) for f, b in
                            ((rf, rb), (vf, vb), (kkf, kkb), (lwf, lwb), (kaf, kab), (kdf, kdb)))
    y, s_new = _scan_chunk(r, v, kk, lw, pairs(cum_f, cum_b), ka, kd, state_ref[...], consts)
    state_ref[...] = s_new
    for d, y_o in enumerate((yf_o, yb_o)):
        for p in range(n_pairs):
            y_o[0, :, p * LANES:(p + 1) * LANES] = y[d * n_pairs + p]


def rwkv_scan(r, v, kk, lwf, lwb, kaf, kab, kdf, kdb):
    B, S, D = r.shape
    C = CHUNK
    nc = S // C
    fwd = pl.BlockSpec((1, C, D), lambda b, c: (b, c, 0))
    bwd = pl.BlockSpec((1, C, D), lambda b, c: (b, nc - 1 - c, 0))
    return pl.pallas_call(
        _rwkv_scan_kernel,
        grid=(B, nc),
        in_specs=[fwd] * 6 + [bwd] * 6,
        out_specs=[fwd, bwd],
        out_shape=[jax.ShapeDtypeStruct((B, S, D), F32)] * 2,
        scratch_shapes=[pltpu.VMEM((2 * (D // LANES), LANES, LANES), F32)],
        compiler_params=_cparams("parallel", "arbitrary"),
        name="rwkv_scan",
    )(r, v, kk, lwf, kaf, kdf, r, v, kk, lwb, kab, kdb)


def _attn_kernel(table_ref, bucket_ref, q_ref, kp_ref, km_ref, kn_ref, vp_ref, vm_ref, vn_ref,
                 o_ref, lse_ref, bias_ref, *, nblk):
    i = pl.program_id(2)
    TQ = q_ref.shape[0]
    TK = TQ + 2 * ATTN_HALF

    @pl.when((pl.program_id(0) == 0) & (pl.program_id(1) == 0) & (i == 0))
    def _():
        bucket = bucket_ref[...]
        row = lax.broadcasted_iota(jnp.int32, (TQ, TK), 0)
        colk = lax.broadcasted_iota(jnp.int32, (TQ, TK), 1)
        band = jnp.abs(colk - ATTN_HALF - row) <= ATTN_HALF
        for h in range(ATTN_SLOTS):
            acc = jnp.zeros((TQ, TK), F32)
            for b in range(N_BUCKETS):
                acc = jnp.where(bucket == b, table_ref[h * N_BUCKETS + b], acc)
            bias_ref[h] = jnp.where(band, acc, NEG_BIG)

    col = lax.broadcasted_iota(jnp.int32, (TQ, TK), 1)
    edge_ok = ((col >= ATTN_HALF) | (i > 0)) & ((col < TQ + ATTN_HALF) | (i < nblk - 1))
    m0 = lax.broadcasted_iota(jnp.int32, (TQ, LANES), 1) < HEAD_DIM
    lane = lax.broadcasted_iota(jnp.int32, (1, LANES), 1)
    head_keep = [(lane < HEAD_DIM).astype(F32).astype(BF16), (lane >= HEAD_DIM).astype(F32).astype(BF16)]
    for p in range(ATTN_OUT_DIM // LANES):
        sl = slice(p * LANES, (p + 1) * LANES)
        q2 = q_ref[:, sl]
        kwin = jnp.concatenate([kp_ref[:, sl], km_ref[:, sl], kn_ref[:, sl]], axis=0)
        vwin = jnp.concatenate([vp_ref[:, sl], vm_ref[:, sl], vn_ref[:, sl]], axis=0)
        outs, lses = [], []
        for hh in range(2):
            qm = q2 * head_keep[hh]
            s = lax.dot_general(qm, kwin, (((1,), (1,)), ((), ())), preferred_element_type=F32)
            s = s + bias_ref[2 * p + hh]
            s = jnp.where(edge_ok, s, NEG_BIG)
            m = jnp.max(s, axis=-1, keepdims=True)
            e = jnp.exp(s - m)
            den = jnp.sum(e, axis=-1, keepdims=True)
            pv = jnp.dot(e.astype(BF16), vwin, preferred_element_type=F32)
            outs.append(pv / den)
            lses.append(jnp.broadcast_to(m + jnp.log(den), (TQ, LANES)))
        o_ref[:, sl] = jnp.where(m0, outs[0], outs[1]).astype(o_ref.dtype)
        lse_ref[:, sl] = jnp.where(m0, lses[0], lses[1])


def _t5_bucket(rel):
    nb = N_BUCKETS // 2
    max_exact = nb // 2
    ret = jnp.where(rel > 0, nb, 0)
    n = jnp.abs(rel)
    nf = jnp.maximum(n, 1).astype(jnp.float32)
    large = max_exact + (jnp.log(nf / max_exact) / math.log(MAX_DISTANCE / max_exact)
                         * (nb - max_exact)).astype(jnp.int32)
    large = jnp.minimum(large, nb - 1)
    return ret + jnp.where(n < max_exact, n, large)


def attention_bias_inputs(table, gi, dilation, *, tq):
    tk = tq + 2 * ATTN_HALF
    rel = jnp.arange(tk)[None, :] - ATTN_HALF - jnp.arange(tq)[:, None]
    bucket = _t5_bucket(rel * dilation).astype(jnp.int32)
    tbl = table[:, gi * ATTN_SLOTS:(gi + 1) * ATTN_SLOTS].astype(F32).T.reshape(-1)
    return tbl, bucket


def dilated_attention(qkv, tbl, bucket, gi, *, tq):
    B, dilation, L, _ = qkv.shape
    nblk = L // tq
    hb = tq // ATTN_HALF
    nhalf = L // ATTN_HALF
    W = ATTN_OUT_DIM

    def main(off):
        return pl.BlockSpec((None, None, tq, W), lambda b, r, i: (b, r, i, off))

    def prev(off):
        return pl.BlockSpec((None, None, ATTN_HALF, W),
                            lambda b, r, i: (b, r, jnp.maximum(i * hb - 1, 0), off))

    def nxt(off):
        return pl.BlockSpec((None, None, ATTN_HALF, W),
                            lambda b, r, i: (b, r, jnp.minimum((i + 1) * hb, nhalf - 1), off))

    tk = tq + 2 * ATTN_HALF
    out_spec = pl.BlockSpec((None, None, tq, W), lambda b, r, i: (b, r, i, 0))
    return pl.pallas_call(
        functools.partial(_attn_kernel, nblk=nblk),
        grid=(B, dilation, nblk),
        in_specs=[pl.BlockSpec(memory_space=pltpu.SMEM),
                  pl.BlockSpec((tq, tk), lambda b, r, i: (0, 0)),
                  main(0), prev(1), main(1), nxt(1), prev(2), main(2), nxt(2)],
        out_specs=[out_spec, out_spec],
        out_shape=[jax.ShapeDtypeStruct((B, dilation, L, W), BF16),
                   jax.ShapeDtypeStruct((B, dilation, L, W), F32)],
        scratch_shapes=[pltpu.VMEM((ATTN_SLOTS, tq, tk), F32)],
        compiler_params=_cparams("arbitrary", "arbitrary", "arbitrary"),
        name=f"dilated_attn_g{gi}",
    )(tbl, bucket, qkv, qkv, qkv, qkv, qkv, qkv, qkv)


def _branch_post_kernel(yf_ref, yb_ref, bonus_ref, g_ref, gnw_ref, gnb_ref,
                        o0, o1, o2, l0, l1, l2, orw_ref, oat_ref, *scratch):
    y = yf_ref[...] + yb_ref[...]
    ones_bd = _head_block_ones(2 * LANES)
    tm, width = y.shape

    def head_mean(x):
        return jnp.concatenate(
            [_head_sum(x[:, c:c + 2 * LANES], ones_bd) for c in range(0, width, 2 * LANES)],
            axis=1) * (1.0 / HEAD_DIM)

    mean = head_mean(y)
    yc = y - mean
    var = head_mean(yc * yc)
    yn = yc * lax.rsqrt(var + GN_EPS) * gnw_ref[...] + gnb_ref[...]
    orw_ref[...] = ((yn + bonus_ref[...]) * g_ref[...]).astype(BF16)

    def natural_order(ref, scr):
        d = ref.shape[0]
        for t in range(ATTN_OUT_DIM // LANES):
            for res in range(d):
                scr[t, pl.ds(res, tm // d, stride=d), :] = (
                    ref[res, :, t * LANES:(t + 1) * LANES].astype(F32))
        return jnp.concatenate([scr[t] for t in range(ATTN_OUT_DIM // LANES)], axis=1)

    s_o1, s_o2, s_l1, s_l2 = scratch
    oa, ob, oc = o0[0].astype(F32), natural_order(o1, s_o1), natural_order(o2, s_o2)
    la, lb, lc = l0[0], natural_order(l1, s_l1), natural_order(l2, s_l2)
    m = jnp.maximum(jnp.maximum(la, lb), lc)
    ea, eb, ec = jnp.exp(la - m), jnp.exp(lb - m), jnp.exp(lc - m)
    den = ea + eb + ec
    oat_ref[...] = ((ea * oa + eb * ob + ec * oc) / den).astype(BF16)


def branch_post(yf, yb, bonus, g, gn_w, gn_b, layer, outs, lses, *, tm):
    M, D = yf.shape
    A = ATTN_OUT_DIM
    batch = outs[0].shape[0]
    nb = M // batch // tm
    big = pl.BlockSpec((tm, D), lambda b, i: (b * nb + i, 0))
    rowp = pl.BlockSpec((None, 1, D), lambda b, i: (layer, 0, 0))
    attn_specs = [pl.BlockSpec((None, o.shape[1], tm // o.shape[1], A), lambda b, i: (b, 0, i, 0))
                  for o in outs]
    return pl.pallas_call(
        _branch_post_kernel,
        grid=(batch, nb),
        in_specs=[big] * 4 + [rowp, rowp] + attn_specs + attn_specs,
        out_specs=[big, pl.BlockSpec((tm, A), lambda b, i: (b * nb + i, 0))],
        out_shape=[jax.ShapeDtypeStruct((M, D), BF16), jax.ShapeDtypeStruct((M, A), BF16)],
        scratch_shapes=[pltpu.VMEM((A // LANES, tm, LANES), F32)] * 4,
        compiler_params=_cparams("parallel", "arbitrary"),
        name="branch_post",
    )(yf, yb, bonus, g, gn_w.reshape(-1, 1, D), gn_b.reshape(-1, 1, D), *outs, *lses)


def _merge_kernel(orw_ref, oat_ref, wr_ref, wa_ref, gr_ref, ga_ref, o_ref):
    a = jnp.dot(orw_ref[...], wr_ref[...].astype(BF16), preferred_element_type=F32)
    b = jnp.dot(oat_ref[...], wa_ref[...].astype(BF16), preferred_element_type=F32)
    o_ref[...] = (gr_ref[...].astype(F32) * a + ga_ref[...].astype(F32) * b).astype(o_ref.dtype)


def branch_merge(orw, oat, w_r, w_a, layer, gates, *, tm, tn):
    M = orw.shape[0]
    N = w_r.shape[2]
    nj = N // tn
    return pl.pallas_call(
        _merge_kernel,
        grid=(M // tm, nj),
        in_specs=[pl.BlockSpec((tm, orw.shape[1]), lambda i, j: (i, 0)),
                  pl.BlockSpec((tm, oat.shape[1]), lambda i, j: (i, 0)),
                  pl.BlockSpec((None, w_r.shape[1], tn), lambda i, j: (layer, 0, j)),
                  pl.BlockSpec((None, w_a.shape[1], tn), lambda i, j: (layer, 0, j)),
                  pl.BlockSpec((tm, tn), lambda i, j: (i, j)),
                  pl.BlockSpec((tm, tn), lambda i, j: (i, j + nj))],
        out_specs=pl.BlockSpec((tm, tn), lambda i, j: (i, j)),
        out_shape=jax.ShapeDtypeStruct((M, N), BF16),
        compiler_params=_cparams("parallel", "arbitrary"),
        name="branch_merge",
    )(orw, oat, w_r, w_a, gates, gates)


def _lora_weights(w_up, a_up):
    z = jnp.zeros_like(w_up[:, 0])
    rows = [jnp.concatenate([w_up[:, 0], z, z, z], axis=2),
            jnp.concatenate([z, w_up[:, 1], z, z], axis=2),
            jnp.concatenate([z, z, a_up[:, 0], z], axis=2),
            jnp.concatenate([z, z, z, a_up[:, 1]], axis=2)]
    return jnp.concatenate(rows, axis=1).astype(BF16)


def kernel(x, norm1_g, w_in, tshift_mu, w0, w_lora_up, a0, a_lora_up, g_lora_up, k_k, k_a, r_k,
           gn_w, gn_b, rel_bias, w_branch_rwkv, w_branch_attn, w_out, norm2_g, w_mlp_in, w_mlp_out,
           final_g):
    B, S, D = x.shape
    M = B * S
    depth = w_in.shape[0]
    c_rkv = 3 * RWKV_DIM
    c_slab = c_rkv + LORA_COLS
    c_attn = c_slab + 3 * ATTN_DIM
    w_lora = w_in[:, :, c_rkv:c_slab].astype(BF16)
    w_qkv = w_in[:, :, c_slab:c_attn].astype(BF16)
    w_gates = w_in[:, :, c_attn:].astype(BF16)
    w_lora_up = _lora_weights(w_lora_up, a_lora_up)
    g_up = g_lora_up.astype(BF16)
    bias_inputs = [attention_bias_inputs(rel_bias, gi, dilation, tq=128)
                   for gi, (_, dilation) in enumerate(ATTN_GROUPS)]
    h = x.reshape(M, D)
    for l in range(depth):
        g1 = norm1_g
        rkv = norm_matmul(h, g1, w_in, l, n_cols=c_rkv, tm=1024, tn=1024)
        lora = norm_matmul(h, g1, w_lora, l, n_cols=LORA_COLS, tm=1024, tn=LORA_COLS)
        qkv_groups = qkv_proj(h, g1, w_qkv, l, batch=B, tm=1024)
        gates = norm_matmul(h, g1, w_gates, l, n_cols=2 * D, tm=1024, tn=1024, act="sigmoid",
                            out_dtype=BF16)

        (r, v, kk, lwf, lwb, kaf, kab, kdf, kdb, g, bonus) = rwkv_prep(
            rkv.reshape(B, S, c_rkv), lora.reshape(B, S, LORA_COLS), l, tshift_mu, w0, a0,
            w_lora_up, g_up, k_k, k_a, r_k, ts=256)
        yf, yb = rwkv_scan(r, v, kk, lwf, lwb, kaf, kab, kdf, kdb)

        outs, lses = [], []
        for gi, (window, dilation) in enumerate(ATTN_GROUPS):
            assert window // (2 * dilation) == ATTN_HALF
            o, lse = dilated_attention(qkv_groups[gi], *bias_inputs[gi], gi, tq=128)
            outs.append(o)
            lses.append(lse)

        orw, oat = branch_post(yf.reshape(M, RWKV_DIM), yb.reshape(M, RWKV_DIM),
                               bonus.reshape(M, RWKV_DIM), g.reshape(M, RWKV_DIM),
                               gn_w, gn_b, l, outs, lses, tm=512)
        merged = branch_merge(orw, oat, w_branch_rwkv, w_branch_attn, l, gates, tm=1024, tn=1024)
        h = matmul_residual(merged, w_out, l, h, tm=1024, tn=1024, tk=D)

        act = norm_matmul(h, norm2_g, w_mlp_in, l, n_cols=w_mlp_in.shape[2], tm=1024, tn=1024,
                          act="relu2", out_dtype=BF16)
        h = matmul_residual(act, w_mlp_out, l, h, tm=1024, tn=1024, tk=2048)
    out = rmsnorm_rows(h, final_g, tm=512)
    return out.reshape(B, S, D)
```

```python
import functools
import math

import jax
import jax.numpy as jnp
from jax import lax
from jax.experimental import pallas as pl
from jax.experimental.pallas import tpu as pltpu

F32 = jnp.float32
BF16 = jnp.bfloat16

HEAD_DIM = 64
LANES = 128
RWKV_DIM = 1024
DECAY_LORA = 96
AAA_LORA = 96
GATE_LORA = 256
LORA_COLS = GATE_LORA + 2 * DECAY_LORA + 2 * AAA_LORA
LORA_TAIL = LORA_COLS - GATE_LORA
ATTN_GROUPS = ((128, 1), (512, 4), (2048, 16))
ATTN_SLOTS = 8
ATTN_DIM = 1536
ATTN_OUT_DIM = ATTN_SLOTS * HEAD_DIM
ATTN_HALF = 64
N_BUCKETS = 32
MAX_DISTANCE = 1024
RMS_EPS = 1e-6
GN_EPS = 64e-5
L2_EPS = 1e-12
NEG_BIG = -1e30
CHUNK = 64
VMEM_LIMIT = 56 * 1024 * 1024


def _cparams(*sem):
    return pltpu.CompilerParams(dimension_semantics=sem, vmem_limit_bytes=VMEM_LIMIT)


def _dot(a, b):
    return jnp.dot(a.astype(BF16), b.astype(BF16), preferred_element_type=F32)


def _split3(x):
    p0 = x.astype(BF16)
    r1 = x - p0.astype(F32)
    p1 = r1.astype(BF16)
    p2 = (r1 - p1.astype(F32)).astype(BF16)
    return p0, p1, p2


def _dot_exact_rhs(a_bf16, x):
    return sum(jnp.dot(a_bf16, p, preferred_element_type=F32) for p in _split3(x))


def _head_sum(x, ones_bd):
    hi = x.astype(BF16)
    lo = (x - hi.astype(F32)).astype(BF16)
    return (jnp.dot(hi, ones_bd, preferred_element_type=F32)
            + jnp.dot(lo, ones_bd, preferred_element_type=F32))


def _head_block_ones(width):
    r = lax.broadcasted_iota(jnp.int32, (width, width), 0) // HEAD_DIM
    c = lax.broadcasted_iota(jnp.int32, (width, width), 1) // HEAD_DIM
    return (r == c).astype(BF16)


def _norm_mm_kernel(x_ref, g_ref, w_ref, o_ref, u_ref, *, act):
    @pl.when(pl.program_id(1) == 0)
    def _():
        x = x_ref[...]
        ms = jnp.mean(x * x, axis=-1, keepdims=True)
        u_ref[...] = (x * lax.rsqrt(ms + RMS_EPS) * g_ref[...]).astype(BF16)

    acc = jnp.dot(u_ref[...], w_ref[...].astype(BF16), preferred_element_type=F32)
    if act == "relu2":
        acc = jnp.square(jnp.maximum(acc, 0.0))
    elif act == "sigmoid":
        acc = jax.nn.sigmoid(acc)
    o_ref[...] = acc.astype(o_ref.dtype)


def norm_matmul(x, g, w, layer, *, n_cols, tm, tn, act=None, out_dtype=F32):
    M, K = x.shape
    return pl.pallas_call(
        functools.partial(_norm_mm_kernel, act=act),
        grid=(M // tm, n_cols // tn),
        in_specs=[pl.BlockSpec((tm, K), lambda i, j: (i, 0)),
                  pl.BlockSpec((None, 1, K), lambda i, j: (layer, 0, 0)),
                  pl.BlockSpec((None, K, tn), lambda i, j: (layer, 0, j))],
        out_specs=pl.BlockSpec((tm, tn), lambda i, j: (i, j)),
        out_shape=jax.ShapeDtypeStruct((M, n_cols), out_dtype),
        scratch_shapes=[pltpu.VMEM((tm, K), BF16)],
        compiler_params=_cparams("parallel", "arbitrary"),
        name="norm_matmul_" + (act or "id"),
    )(x, g.reshape(-1, 1, K), w)


def _qkv_proj_kernel(x_ref, g_ref, w_ref, o0_ref, o1_ref, o2_ref, u_ref, acc_ref):
    @pl.when(pl.program_id(1) == 0)
    def _():
        x = x_ref[...]
        ms = jnp.mean(x * x, axis=-1, keepdims=True)
        u_ref[...] = (x * lax.rsqrt(ms + RMS_EPS) * g_ref[...]).astype(BF16)

    acc = jnp.dot(u_ref[...], w_ref[...], preferred_element_type=F32)
    acc = acc * jnp.where(pl.program_id(1) == 0, HEAD_DIM ** -0.5, 1.0)
    tm = acc.shape[0]
    tiles_per_group = ATTN_OUT_DIM // LANES
    for gi, (o_ref, (_, d)) in enumerate(zip((o0_ref, o1_ref, o2_ref), ATTN_GROUPS)):
        if d == 1:
            o_ref[0, 0] = acc[:, gi * ATTN_OUT_DIM:(gi + 1) * ATTN_OUT_DIM].astype(BF16)
            continue
        for t in range(tiles_per_group):
            c = gi * tiles_per_group + t
            acc_ref[c] = acc[:, c * LANES:(c + 1) * LANES]
            for res in range(d):
                o_ref[0, res, :, t * LANES:(t + 1) * LANES] = (
                    acc_ref[c, pl.ds(res, tm // d, stride=d), :].astype(BF16))


def qkv_proj(x, g, w, layer, *, batch, tm):
    M, K = x.shape
    S = M // batch
    nb = S // tm
    outs_shape, outs_spec = [], []
    for _, d in ATTN_GROUPS:
        outs_shape.append(jax.ShapeDtypeStruct((batch, d, S // d, 3 * ATTN_OUT_DIM), BF16))
        outs_spec.append(pl.BlockSpec((1, d, tm // d, ATTN_OUT_DIM),
                                      lambda i, j: (i // nb, 0, i % nb, j)))
    return pl.pallas_call(
        _qkv_proj_kernel,
        grid=(M // tm, 3),
        in_specs=[pl.BlockSpec((tm, K), lambda i, j: (i, 0)),
                  pl.BlockSpec((None, 1, K), lambda i, j: (layer, 0, 0)),
                  pl.BlockSpec((None, K, ATTN_DIM), lambda i, j: (layer, 0, j))],
        out_specs=outs_spec,
        out_shape=outs_shape,
        scratch_shapes=[pltpu.VMEM((tm, K), BF16), pltpu.VMEM((ATTN_DIM // LANES, tm, LANES), F32)],
        compiler_params=_cparams("parallel", "arbitrary"),
        name="qkv_proj",
    )(x, g.reshape(-1, 1, K), w)


def _mm_res_kernel(x_ref, w_ref, r_ref, o_ref, acc_ref):
    k = pl.program_id(2)

    @pl.when(k == 0)
    def _():
        acc_ref[...] = jnp.zeros_like(acc_ref)

    acc_ref[...] += jnp.dot(x_ref[...], w_ref[...].astype(BF16), preferred_element_type=F32)

    @pl.when(k == pl.num_programs(2) - 1)
    def _():
        o_ref[...] = r_ref[...] + acc_ref[...]


def matmul_residual(x, w, layer, res, *, tm, tn, tk):
    M, K = x.shape
    N = w.shape[2]
    return pl.pallas_call(
        _mm_res_kernel,
        grid=(M // tm, N // tn, K // tk),
        in_specs=[pl.BlockSpec((tm, tk), lambda i, j, k: (i, k)),
                  pl.BlockSpec((None, tk, tn), lambda i, j, k: (layer, k, j)),
                  pl.BlockSpec((tm, tn), lambda i, j, k: (i, j))],
        out_specs=pl.BlockSpec((tm, tn), lambda i, j, k: (i, j)),
        out_shape=jax.ShapeDtypeStruct((M, N), F32),
        scratch_shapes=[pltpu.VMEM((tm, tn), F32)],
        compiler_params=_cparams("parallel", "parallel", "arbitrary"),
        name="matmul_residual",
    )(x, w, res)


def _rmsnorm_kernel(x_ref, g_ref, o_ref):
    x = x_ref[...]
    ms = jnp.mean(x * x, axis=-1, keepdims=True)
    o_ref[...] = x * lax.rsqrt(ms + RMS_EPS) * g_ref[...]


def rmsnorm_rows(x, g, *, tm):
    M, K = x.shape
    return pl.pallas_call(
        _rmsnorm_kernel,
        grid=(M // tm,),
        in_specs=[pl.BlockSpec((tm, K), lambda i: (i, 0)),
                  pl.BlockSpec((1, K), lambda i: (0, 0))],
        out_specs=pl.BlockSpec((tm, K), lambda i: (i, 0)),
        out_shape=jax.ShapeDtypeStruct((M, K), F32),
        compiler_params=_cparams("parallel"),
        name="final_rmsnorm",
    )(x, g.reshape(1, K))


def _token_shift(main, prev_blk, next_blk, mu, first, last):
    rows = main.shape[0]
    prev_row = jnp.where(first, 0.0, prev_blk[7:8, :])
    next_row = jnp.where(last, 0.0, next_blk[0:1, :])
    rid = lax.broadcasted_iota(jnp.int32, main.shape, 0)
    up = jnp.where(rid == 0, prev_row, pltpu.roll(main, 1, 0))
    dn = jnp.where(rid == rows - 1, next_row, pltpu.roll(main, rows - 1, 0))
    return main * (1.0 - mu) + (0.5 * mu) * (up + dn)


def _rwkv_prep_kernel(r_m, r_p, r_n, k_m, k_p, k_n, v_m, v_p, v_n, l_m, l_p, l_n,
                      mu_ref, w0_ref, a0_ref, lora_w_ref, g_up_ref, kk_ref, ka_ref, rk_ref,
                      r_o, v_o, kk_o, lwf_o, lwb_o, kaf_o, kab_o, kdf_o, kdb_o, g_o, bonus_o):
    i = pl.program_id(1)
    first = i == 0
    last = i == pl.num_programs(1) - 1
    D = RWKV_DIM
    r = _token_shift(r_m[0], r_p[0], r_n[0], mu_ref[:, 0:D], first, last)
    k = _token_shift(k_m[0], k_p[0], k_n[0], mu_ref[:, D:2 * D], first, last)
    v = _token_shift(v_m[0], v_p[0], v_n[0], mu_ref[:, 2 * D:3 * D], first, last)
    lo = _token_shift(l_m[0], l_p[0], l_n[0], mu_ref[:, 3 * D:], first, last)

    g = _dot(jax.nn.sigmoid(lo[:, :GATE_LORA]), g_up_ref[...])
    tail = lo[:, GATE_LORA:]
    col = lax.broadcasted_iota(jnp.int32, tail.shape, 1)
    tail = jnp.where(col < 2 * DECAY_LORA, jnp.tanh(tail), tail)
    up = _dot(tail, lora_w_ref[...])

    ones_bd = _head_block_ones(2 * LANES)

    def head_sum(x):
        return jnp.concatenate(
            [_head_sum(x[:, c:c + 2 * LANES], ones_bd) for c in range(0, RWKV_DIM, 2 * LANES)], axis=1)

    kk = k * kk_ref[...]
    kk = kk * jnp.minimum(lax.rsqrt(head_sum(kk * kk)), 1.0 / L2_EPS)

    r_o[0] = r.astype(r_o.dtype)
    v_o[0] = v.astype(v_o.dtype)
    kk_o[0] = kk.astype(kk_o.dtype)
    g_o[0] = g
    kd_sum = jnp.zeros_like(k)
    for d, (lw_o, ka_o, kd_o) in enumerate(((lwf_o, kaf_o, kdf_o), (lwb_o, kab_o, kdb_o))):
        z = w0_ref[d:d + 1, :] + up[:, d * RWKV_DIM:(d + 1) * RWKV_DIM]
        nz = -z
        softplus = jnp.maximum(nz, 0.0) + jnp.log(1.0 + jnp.exp(-jnp.abs(nz)))
        lw_o[0] = -jnp.exp(-softplus - 0.5)
        a = jax.nn.sigmoid(a0_ref[d:d + 1, :] + up[:, (2 + d) * RWKV_DIM:(3 + d) * RWKV_DIM])
        kd = k * (1.0 + (a - 1.0) * ka_ref[...])
        ka_o[0] = (kk * a).astype(ka_o.dtype)
        kd_o[0] = kd.astype(kd_o.dtype)
        kd_sum = kd_sum + kd
    bonus_o[0] = head_sum(r * kd_sum * rk_ref[...]) * v


def rwkv_prep(rkv, lora, layer, mu, w0, a0, lora_w, g_up, k_k, k_a, r_k, *, ts):
    B, S, _ = rkv.shape
    D = RWKV_DIM
    nblk8 = S // 8

    def main_spec(width, cblk):
        return pl.BlockSpec((1, ts, width), lambda b, i: (b, i, cblk))

    def prev_spec(width, cblk):
        return pl.BlockSpec((1, 8, width), lambda b, i: (b, jnp.maximum(i * (ts // 8) - 1, 0), cblk))

    def next_spec(width, cblk):
        return pl.BlockSpec((1, 8, width), lambda b, i: (b, jnp.minimum((i + 1) * (ts // 8), nblk8 - 1), cblk))

    def layer_spec(rows, width):
        return pl.BlockSpec((None, rows, width), lambda b, i: (layer, 0, 0))

    in_specs = []
    args = []
    for cblk in range(3):
        in_specs += [main_spec(D, cblk), prev_spec(D, cblk), next_spec(D, cblk)]
        args += [rkv, rkv, rkv]
    in_specs += [main_spec(LORA_COLS, 0), prev_spec(LORA_COLS, 0), next_spec(LORA_COLS, 0)]
    args += [lora, lora, lora]
    depth = mu.shape[0]
    in_specs += [layer_spec(1, 3 * D + LORA_COLS), layer_spec(2, D), layer_spec(2, D),
                 layer_spec(LORA_TAIL, 4 * D), layer_spec(GATE_LORA, D),
                 layer_spec(1, D), layer_spec(1, D), layer_spec(1, D)]
    args += [mu.reshape(depth, 1, -1), w0, a0, lora_w, g_up,
             k_k.reshape(depth, 1, D), k_a.reshape(depth, 1, D), r_k.reshape(depth, 1, D)]
    out_spec = pl.BlockSpec((1, ts, D), lambda b, i: (b, i, 0))
    out_dtypes = [BF16, BF16, BF16, F32, F32, BF16, BF16, BF16, BF16, F32, F32]
    n_out = len(out_dtypes)
    return pl.pallas_call(
        _rwkv_prep_kernel,
        grid=(B, S // ts),
        in_specs=in_specs,
        out_specs=[out_spec] * n_out,
        out_shape=[jax.ShapeDtypeStruct((B, S, D), dt) for dt in out_dtypes],
        compiler_params=_cparams("parallel", "arbitrary"),
        name="rwkv_prep",
    )(*args)


_BMM = (((2,), (1,)), ((0,), (0,)))
_BMM_NT = (((2,), (2,)), ((0,), (0,)))
_BMM_TN = (((1,), (1,)), ((0,), (0,)))


def _bdot(a, b, dims=_BMM):
    return lax.dot_general(a.astype(BF16), b.astype(BF16), dims, preferred_element_type=F32)


def _bd_stack(x, m0):
    zero = jnp.zeros_like(x)
    return jnp.concatenate([jnp.where(m0, x, zero), jnp.where(m0, zero, x)], axis=-2)


def _scan_chunk(r, v, kk, lw, cum, ka, kd, s_bd, consts):
    is_rev, strict2, incl2, eye2, m0_1, m0_2, bd_mask, off_masks = consts
    C = r.shape[1]
    cum_prev = cum - lw
    e_cum = jnp.exp(cum)
    e_prev = jnp.exp(cum_prev)
    e_neg = jnp.exp(-cum)
    total = jnp.where(is_rev, cum[:, 0:1, :], cum[:, C - 1:C, :])
    e_rest = jnp.exp(total - cum)
    r0 = r * e_cum
    at0 = -kk * e_prev
    bt = ka * e_neg
    kt = kd * e_neg
    bh = ka * e_rest
    kh = kd * e_rest

    gram = _bdot(jnp.concatenate([at0, r0], axis=1),
                 jnp.concatenate([_bd_stack(bt, m0_1), _bd_stack(kt, m0_1)], axis=1), _BMM_NT)
    zero = jnp.zeros_like(gram[:, :C])
    a_top = jnp.where(strict2, gram[:, :C], zero)
    a_bot = jnp.where(incl2, gram[:, C:], zero)
    a_ab = a_top[:, :, :LANES]
    a_ak = a_top[:, :, LANES:]

    zero1 = jnp.zeros_like(a_ab)
    t = eye2 + jnp.where(off_masks[0], a_ab, zero1)
    for off_mask in off_masks[1:]:
        z = _bdot(t, _bd_stack(jnp.where(off_mask, a_ab, zero1), m0_1))
        t = t + _bdot(z, _bd_stack(t, m0_1))

    av = _bdot(a_ak, _bd_stack(v, m0_1))
    wu = _bdot(t, _bd_stack(jnp.concatenate([at0, av], axis=2), m0_2))
    w_a = wu[:, :, :LANES]
    u_v = wu[:, :, LANES:]

    wr = _bdot(jnp.concatenate([w_a, r0], axis=1), s_bd, _BMM_NT)
    u = wr[:, :C] + u_v
    y = wr[:, C:] + _bdot(a_bot, jnp.concatenate([_bd_stack(u, m0_1), _bd_stack(v, m0_1)], axis=1))
    upd = _bdot(jnp.concatenate([u, v], axis=1), jnp.concatenate([bh, kh], axis=1), _BMM_TN)
    s_new = s_bd * jnp.exp(total) + jnp.where(bd_mask, upd, jnp.zeros_like(upd))
    return y, s_new


def _scan_consts(C, n_fwd, n_bwd):
    G = n_fwd + n_bwd
    is_rev = lax.broadcasted_iota(jnp.int32, (G, 1, 1), 0) >= n_fwd
    sign = jnp.where(is_rev, -1, 1)
    row2 = lax.broadcasted_iota(jnp.int32, (G, C, 2 * LANES), 1)
    col2 = lax.broadcasted_iota(jnp.int32, (G, C, 2 * LANES), 2) % HEAD_DIM
    strict2 = (col2 - row2) * sign < 0
    incl2 = (col2 - row2) * sign <= 0
    lane1 = lax.broadcasted_iota(jnp.int32, (1, C, LANES), 2)
    lane2 = lax.broadcasted_iota(jnp.int32, (1, C, 2 * LANES), 2) % LANES
    m0_1 = lane1 < HEAD_DIM
    m0_2 = lane2 < HEAD_DIM
    rb = lax.broadcasted_iota(jnp.int32, (1, LANES, LANES), 1) // HEAD_DIM
    cb = lax.broadcasted_iota(jnp.int32, (1, LANES, LANES), 2) // HEAD_DIM
    bd_mask = rb == cb
    row1 = lax.broadcasted_iota(jnp.int32, (1, C, LANES), 1)
    col1 = lane1 % HEAD_DIM
    eye2 = (row1 == col1).astype(F32)
    off_masks = [row1 // 2 == col1 // 2]
    b = 2
    while b < C:
        off_masks.append((row1 // (2 * b) == col1 // (2 * b)) & (row1 // b != col1 // b))
        b *= 2
    return (is_rev, strict2, incl2, eye2, m0_1, m0_2, bd_mask, off_masks)


def _rwkv_scan_kernel(rf, vf, kkf, lwf, kaf, kdf, rb, vb, kkb, lwb, kab, kdb,
                      yf_o, yb_o, state_ref):
    c = pl.program_id(1)

    @pl.when(c == 0)
    def _():
        state_ref[...] = jnp.zeros_like(state_ref)

    C = rf.shape[1]
    n_pairs = rf.shape[2] // LANES
    consts = _scan_consts(C, n_pairs, n_pairs)

    row = lax.broadcasted_iota(jnp.int32, (C, C), 0)
    col = lax.broadcasted_iota(jnp.int32, (C, C), 1)
    cum_f = _dot_exact_rhs((col <= row).astype(F32).astype(BF16), lwf[0])
    cum_b = _dot_exact_rhs((col >= row).astype(F32).astype(BF16), lwb[0])

    def pairs(x_f, x_b):
        return jnp.stack([x[:, p * LANES:(p + 1) * LANES] for x in (x_f, x_b) for p in range(n_pairs)],
                         axis=0)

    r, v, kk, lw, ka, kd = (pairs(f[0].astype(F32), b[0].astype(F32)) for f, b in
                            ((rf, rb), (vf, vb), (kkf, kkb), (lwf, lwb), (kaf, kab), (kdf, kdb)))
    y, s_new = _scan_chunk(r, v, kk, lw, pairs(cum_f, cum_b), ka, kd, state_ref[...], consts)
    state_ref[...] = s_new
    for d, y_o in enumerate((yf_o, yb_o)):
        for p in range(n_pairs):
            y_o[0, :, p * LANES:(p + 1) * LANES] = y[d * n_pairs + p]


def rwkv_scan(r, v, kk, lwf, lwb, kaf, kab, kdf, kdb):
    B, S, D = r.shape
    C = CHUNK
    nc = S // C
    fwd = pl.BlockSpec((1, C, D), lambda b, c: (b, c, 0))
    bwd = pl.BlockSpec((1, C, D), lambda b, c: (b, nc - 1 - c, 0))
    return pl.pallas_call(
        _rwkv_scan_kernel,
        grid=(B, nc),
        in_specs=[fwd] * 6 + [bwd] * 6,
        out_specs=[fwd, bwd],
        out_shape=[jax.ShapeDtypeStruct((B, S, D), F32)] * 2,
        scratch_shapes=[pltpu.VMEM((2 * (D // LANES), LANES, LANES), F32)],
        compiler_params=_cparams("parallel", "arbitrary"),
        name="rwkv_scan",
    )(r, v, kk, lwf, kaf, kdf, r, v, kk, lwb, kab, kdb)


def _attn_kernel(table_ref, bucket_ref, q_ref, kp_ref, km_ref, kn_ref, vp_ref, vm_ref, vn_ref,
                 o_ref, lse_ref, bias_ref, *, nblk):
    i = pl.program_id(2)
    TQ = q_ref.shape[0]
    TK = TQ + 2 * ATTN_HALF

    @pl.when((pl.program_id(0) == 0) & (pl.program_id(1) == 0) & (i == 0))
    def _():
        bucket = bucket_ref[...]
        row = lax.broadcasted_iota(jnp.int32, (TQ, TK), 0)
        colk = lax.broadcasted_iota(jnp.int32, (TQ, TK), 1)
        band = jnp.abs(colk - ATTN_HALF - row) <= ATTN_HALF
        for h in range(ATTN_SLOTS):
            acc = jnp.zeros((TQ, TK), F32)
            for b in range(N_BUCKETS):
                acc = jnp.where(bucket == b, table_ref[h * N_BUCKETS + b], acc)
            bias_ref[h] = jnp.where(band, acc, NEG_BIG)

    col = lax.broadcasted_iota(jnp.int32, (TQ, TK), 1)
    edge_ok = ((col >= ATTN_HALF) | (i > 0)) & ((col < TQ + ATTN_HALF) | (i < nblk - 1))
    m0 = lax.broadcasted_iota(jnp.int32, (TQ, LANES), 1) < HEAD_DIM
    lane = lax.broadcasted_iota(jnp.int32, (1, LANES), 1)
    head_keep = [(lane < HEAD_DIM).astype(F32).astype(BF16), (lane >= HEAD_DIM).astype(F32).astype(BF16)]
    for p in range(ATTN_OUT_DIM // LANES):
        sl = slice(p * LANES, (p + 1) * LANES)
        q2 = q_ref[:, sl]
        kwin = jnp.concatenate([kp_ref[:, sl], km_ref[:, sl], kn_ref[:, sl]], axis=0)
        vwin = jnp.concatenate([vp_ref[:, sl], vm_ref[:, sl], vn_ref[:, sl]], axis=0)
        outs, lses = [], []
        for hh in range(2):
            qm = q2 * head_keep[hh]
            s = lax.dot_general(qm, kwin, (((1,), (1,)), ((), ())), preferred_element_type=F32)
            s = s + bias_ref[2 * p + hh]
            s = jnp.where(edge_ok, s, NEG_BIG)
            m = jnp.max(s, axis=-1, keepdims=True)
            e = jnp.exp(s - m)
            den = jnp.sum(e, axis=-1, keepdims=True)
            pv = jnp.dot(e.astype(BF16), vwin, preferred_element_type=F32)
            outs.append(pv / den)
            lses.append(jnp.broadcast_to(m + jnp.log(den), (TQ, LANES)))
        o_ref[:, sl] = jnp.where(m0, outs[0], outs[1]).astype(o_ref.dtype)
        lse_ref[:, sl] = jnp.where(m0, lses[0], lses[1])


def _t5_bucket(rel):
    nb = N_BUCKETS // 2
    max_exact = nb // 2
    ret = jnp.where(rel > 0, nb, 0)
    n = jnp.abs(rel)
    nf = jnp.maximum(n, 1).astype(jnp.float32)
    large = max_exact + (jnp.log(nf / max_exact) / math.log(MAX_DISTANCE / max_exact)
                         * (nb - max_exact)).astype(jnp.int32)
    large = jnp.minimum(large, nb - 1)
    return ret + jnp.where(n < max_exact, n, large)


def attention_bias_inputs(table, gi, dilation, *, tq):
    tk = tq + 2 * ATTN_HALF
    rel = jnp.arange(tk)[None, :] - ATTN_HALF - jnp.arange(tq)[:, None]
    bucket = _t5_bucket(rel * dilation).astype(jnp.int32)
    tbl = table[:, gi * ATTN_SLOTS:(gi + 1) * ATTN_SLOTS].astype(F32).T.reshape(-1)
    return tbl, bucket


def dilated_attention(qkv, tbl, bucket, gi, *, tq):
    B, dilation, L, _ = qkv.shape
    nblk = L // tq
    hb = tq // ATTN_HALF
    nhalf = L // ATTN_HALF
    W = ATTN_OUT_DIM

    def main(off):
        return pl.BlockSpec((None, None, tq, W), lambda b, r, i: (b, r, i, off))

    def prev(off):
        return pl.BlockSpec((None, None, ATTN_HALF, W),
                            lambda b, r, i: (b, r, jnp.maximum(i * hb - 1, 0), off))

    def nxt(off):
        return pl.BlockSpec((None, None, ATTN_HALF, W),
                            lambda b, r, i: (b, r, jnp.minimum((i + 1) * hb, nhalf - 1), off))

    tk = tq + 2 * ATTN_HALF
    out_spec = pl.BlockSpec((None, None, tq, W), lambda b, r, i: (b, r, i, 0))
    return pl.pallas_call(
        functools.partial(_attn_kernel, nblk=nblk),
        grid=(B, dilation, nblk),
        in_specs=[pl.BlockSpec(memory_space=pltpu.SMEM),
                  pl.BlockSpec((tq, tk), lambda b, r, i: (0, 0)),
                  main(0), prev(1), main(1), nxt(1), prev(2), main(2), nxt(2)],
        out_specs=[out_spec, out_spec],
        out_shape=[jax.ShapeDtypeStruct((B, dilation, L, W), BF16),
                   jax.ShapeDtypeStruct((B, dilation, L, W), F32)],
        scratch_shapes=[pltpu.VMEM((ATTN_SLOTS, tq, tk), F32)],
        compiler_params=_cparams("arbitrary", "arbitrary", "arbitrary"),
        name=f"dilated_attn_g{gi}",
    )(tbl, bucket, qkv, qkv, qkv, qkv, qkv, qkv, qkv)


def _branch_post_kernel(yf_ref, yb_ref, bonus_ref, g_ref, gnw_ref, gnb_ref,
                        o0, o1, o2, l0, l1, l2, orw_ref, oat_ref, *scratch):
    y = yf_ref[...] + yb_ref[...]
    ones_bd = _head_block_ones(2 * LANES)
    tm, width = y.shape

    def head_mean(x):
        return jnp.concatenate(
            [_head_sum(x[:, c:c + 2 * LANES], ones_bd) for c in range(0, width, 2 * LANES)],
            axis=1) * (1.0 / HEAD_DIM)

    mean = head_mean(y)
    yc = y - mean
    var = head_mean(yc * yc)
    yn = yc * lax.rsqrt(var + GN_EPS) * gnw_ref[...] + gnb_ref[...]
    orw_ref[...] = ((yn + bonus_ref[...]) * g_ref[...]).astype(BF16)

    def natural_order(ref, scr):
        d = ref.shape[0]
        for t in range(ATTN_OUT_DIM // LANES):
            for res in range(d):
                scr[t, pl.ds(res, tm // d, stride=d), :] = (
                    ref[res, :, t * LANES:(t + 1) * LANES].astype(F32))
        return jnp.concatenate([scr[t] for t in range(ATTN_OUT_DIM // LANES)], axis=1)

    s_o1, s_o2, s_l1, s_l2 = scratch
    oa, ob, oc = o0[0].astype(F32), natural_order(o1, s_o1), natural_order(o2, s_o2)
    la, lb, lc = l0[0], natural_order(l1, s_l1), natural_order(l2, s_l2)
    m = jnp.maximum(jnp.maximum(la, lb), lc)
    ea, eb, ec = jnp.exp(la - m), jnp.exp(lb - m), jnp.exp(lc - m)
    den = ea + eb + ec
    oat_ref[...] = ((ea * oa + eb * ob + ec * oc) / den).astype(BF16)


def branch_post(yf, yb, bonus, g, gn_w, gn_b, layer, outs, lses, *, tm):
    M, D = yf.shape
    A = ATTN_OUT_DIM
    batch = outs[0].shape[0]
    nb = M // batch // tm
    big = pl.BlockSpec((tm, D), lambda b, i: (b * nb + i, 0))
    rowp = pl.BlockSpec((None, 1, D), lambda b, i: (layer, 0, 0))
    attn_specs = [pl.BlockSpec((None, o.shape[1], tm // o.shape[1], A), lambda b, i: (b, 0, i, 0))
                  for o in outs]
    return pl.pallas_call(
        _branch_post_kernel,
        grid=(batch, nb),
        in_specs=[big] * 4 + [rowp, rowp] + attn_specs + attn_specs,
        out_specs=[big, pl.BlockSpec((tm, A), lambda b, i: (b * nb + i, 0))],
        out_shape=[jax.ShapeDtypeStruct((M, D), BF16), jax.ShapeDtypeStruct((M, A), BF16)],
        scratch_shapes=[pltpu.VMEM((A // LANES, tm, LANES), F32)] * 4,
        compiler_params=_cparams("parallel", "arbitrary"),
        name="branch_post",
    )(yf, yb, bonus, g, gn_w.reshape(-1, 1, D), gn_b.reshape(-1, 1, D), *outs, *lses)


def _merge_kernel(orw_ref, oat_ref, wr_ref, wa_ref, gr_ref, ga_ref, o_ref):
    a = jnp.dot(orw_ref[...], wr_ref[...].astype(BF16), preferred_element_type=F32)
    b = jnp.dot(oat_ref[...], wa_ref[...].astype(BF16), preferred_element_type=F32)
    o_ref[...] = (gr_ref[...].astype(F32) * a + ga_ref[...].astype(F32) * b).astype(o_ref.dtype)


def branch_merge(orw, oat, w_r, w_a, layer, gates, *, tm, tn):
    M = orw.shape[0]
    N = w_r.shape[2]
    nj = N // tn
    return pl.pallas_call(
        _merge_kernel,
        grid=(M // tm, nj),
        in_specs=[pl.BlockSpec((tm, orw.shape[1]), lambda i, j: (i, 0)),
                  pl.BlockSpec((tm, oat.shape[1]), lambda i, j: (i, 0)),
                  pl.BlockSpec((None, w_r.shape[1], tn), lambda i, j: (layer, 0, j)),
                  pl.BlockSpec((None, w_a.shape[1], tn), lambda i, j: (layer, 0, j)),
                  pl.BlockSpec((tm, tn), lambda i, j: (i, j)),
                  pl.BlockSpec((tm, tn), lambda i, j: (i, j + nj))],
        out_specs=pl.BlockSpec((tm, tn), lambda i, j: (i, j)),
        out_shape=jax.ShapeDtypeStruct((M, N), BF16),
        compiler_params=_cparams("parallel", "arbitrary"),
        name="branch_merge",
    )(orw, oat, w_r, w_a, gates, gates)


def _lora_weights(w_up, a_up):
    z = jnp.zeros_like(w_up[:, 0])
    rows = [jnp.concatenate([w_up[:, 0], z, z, z], axis=2),
            jnp.concatenate([z, w_up[:, 1], z, z], axis=2),
            jnp.concatenate([z, z, a_up[:, 0], z], axis=2),
            jnp.concatenate([z, z, z, a_up[:, 1]], axis=2)]
    return jnp.concatenate(rows, axis=1).astype(BF16)


def kernel(x, norm1_g, w_in, tshift_mu, w0, w_lora_up, a0, a_lora_up, g_lora_up, k_k, k_a, r_k,
           gn_w, gn_b, rel_bias, w_branch_rwkv, w_branch_attn, w_out, norm2_g, w_mlp_in, w_mlp_out,
           final_g):
    B, S, D = x.shape
    M = B * S
    depth = w_in.shape[0]
    c_rkv = 3 * RWKV_DIM
    c_slab = c_rkv + LORA_COLS
    c_attn = c_slab + 3 * ATTN_DIM
    w_lora = w_in[:, :, c_rkv:c_slab].astype(BF16)
    w_qkv = w_in[:, :, c_slab:c_attn].astype(BF16)
    w_gates = w_in[:, :, c_attn:].astype(BF16)
    w_lora_up = _lora_weights(w_lora_up, a_lora_up)
    g_up = g_lora_up.astype(BF16)
    bias_inputs = [attention_bias_inputs(rel_bias, gi, dilation, tq=128)
                   for gi, (_, dilation) in enumerate(ATTN_GROUPS)]
    h = x.reshape(M, D)
    for l in range(depth):
        g1 = norm1_g
        rkv = norm_matmul(h, g1, w_in, l, n_cols=c_rkv, tm=1024, tn=1024)
        lora = norm_matmul(h, g1, w_lora, l, n_cols=LORA_COLS, tm=1024, tn=LORA_COLS)
        qkv_groups = qkv_proj(h, g1, w_qkv, l, batch=B, tm=1024)
        gates = norm_matmul(h, g1, w_gates, l, n_cols=2 * D, tm=1024, tn=1024, act="sigmoid",
                            out_dtype=BF16)

        (r, v, kk, lwf, lwb, kaf, kab, kdf, kdb, g, bonus) = rwkv_prep(
            rkv.reshape(B, S, c_rkv), lora.reshape(B, S, LORA_COLS), l, tshift_mu, w0, a0,
            w_lora_up, g_up, k_k, k_a, r_k, ts=256)
        yf, yb = rwkv_scan(r, v, kk, lwf, lwb, kaf, kab, kdf, kdb)

        outs, lses = [], []
        for gi, (window, dilation) in enumerate(ATTN_GROUPS):
            assert window // (2 * dilation) == ATTN_HALF
            o, lse = dilated_attention(qkv_groups[gi], *bias_inputs[gi], gi, tq=128)
            outs.append(o)
            lses.append(lse)

        orw, oat = branch_post(yf.reshape(M, RWKV_DIM), yb.reshape(M, RWKV_DIM),
                               bonus.reshape(M, RWKV_DIM), g.reshape(M, RWKV_DIM),
                               gn_w, gn_b, l, outs, lses, tm=512)
        merged = branch_merge(orw, oat, w_branch_rwkv, w_branch_attn, l, gates, tm=1024, tn=1024)
        h = matmul_residual(merged, w_out, l, h, tm=1024, tn=1024, tk=D)

        act = norm_matmul(h, norm2_g, w_mlp_in, l, n_cols=w_mlp_in.shape[2], tm=1024, tn=1024,
                          act="relu2", out_dtype=BF16)
        h = matmul_residual(act, w_mlp_out, l, h, tm=1024, tn=1024, tk=2048)
    out = rmsnorm_rows(h, final_g, tm=512)
    return out.reshape(B, S, D)
```

```python
import functools
import math

import jax
import jax.numpy as jnp
from jax import lax
from jax.experimental import pallas as pl
from jax.experimental.pallas import tpu as pltpu

F32 = jnp.float32
BF16 = jnp.bfloat16

HEAD_DIM = 64
LANES = 128
RWKV_DIM = 1024
DECAY_LORA = 96
AAA_LORA = 96
GATE_LORA = 256
LORA_COLS = GATE_LORA + 2 * DECAY_LORA + 2 * AAA_LORA
LORA_TAIL = LORA_COLS - GATE_LORA
ATTN_GROUPS = ((128, 1), (512, 4), (2048, 16))
ATTN_SLOTS = 8
ATTN_DIM = 1536
ATTN_OUT_DIM = ATTN_SLOTS * HEAD_DIM
ATTN_HALF = 64
N_BUCKETS = 32
MAX_DISTANCE = 1024
RMS_EPS = 1e-6
GN_EPS = 64e-5
L2_EPS = 1e-12
NEG_BIG = -1e30
CHUNK = 64
VMEM_LIMIT = 56 * 1024 * 1024


def _cparams(*sem):
    return pltpu.CompilerParams(dimension_semantics=sem, vmem_limit_bytes=VMEM_LIMIT)


def _dot(a, b):
    return jnp.dot(a.astype(BF16), b.astype(BF16), preferred_element_type=F32)


def _split3(x):
    p0 = x.astype(BF16)
    r1 = x - p0.astype(F32)
    p1 = r1.astype(BF16)
    p2 = (r1 - p1.astype(F32)).astype(BF16)
    return p0, p1, p2


def _dot_exact_rhs(a_bf16, x):
    return sum(jnp.dot(a_bf16, p, preferred_element_type=F32) for p in _split3(x))


def _head_sum(x, ones_bd):
    hi = x.astype(BF16)
    lo = (x - hi.astype(F32)).astype(BF16)
    return (jnp.dot(hi, ones_bd, preferred_element_type=F32)
            + jnp.dot(lo, ones_bd, preferred_element_type=F32))


def _head_block_ones(width):
    r = lax.broadcasted_iota(jnp.int32, (width, width), 0) // HEAD_DIM
    c = lax.broadcasted_iota(jnp.int32, (width, width), 1) // HEAD_DIM
    return (r == c).astype(BF16)


def _norm_mm_kernel(x_ref, g_ref, *refs, act):
    *w_refs, o_ref, u_ref = refs

    @pl.when(pl.program_id(1) == 0)
    def _():
        x = x_ref[...]
        ms = jnp.mean(x * x, axis=-1, keepdims=True)
        u_ref[...] = (x * lax.rsqrt(ms + RMS_EPS) * g_ref[...]).astype(BF16)

    w = jnp.concatenate([w_ref[...].astype(BF16) for w_ref in w_refs], axis=1)
    acc = jnp.dot(u_ref[...], w, preferred_element_type=F32)
    if act == "relu2":
        acc = jnp.square(jnp.maximum(acc, 0.0))
    elif act == "sigmoid":
        acc = jax.nn.sigmoid(acc)
    o_ref[...] = acc.astype(o_ref.dtype)


def norm_matmul(x, g, w, layer, *, col0=0, n_cols, tm, tn, pieces=1, act=None, out_dtype=F32):
    M, K = x.shape
    pw = tn // pieces
    assert pw * pieces == tn and col0 % pw == 0 and n_cols % tn == 0
    w_specs = [pl.BlockSpec((None, K, pw), functools.partial(
        lambda i, j, t: (layer, 0, col0 // pw + j * pieces + t), t=t)) for t in range(pieces)]
    return pl.pallas_call(
        functools.partial(_norm_mm_kernel, act=act),
        grid=(M // tm, n_cols // tn),
        in_specs=[pl.BlockSpec((tm, K), lambda i, j: (i, 0)),
                  pl.BlockSpec((None, 1, K), lambda i, j: (layer, 0, 0))] + w_specs,
        out_specs=pl.BlockSpec((tm, tn), lambda i, j: (i, j)),
        out_shape=jax.ShapeDtypeStruct((M, n_cols), out_dtype),
        scratch_shapes=[pltpu.VMEM((tm, K), BF16)],
        compiler_params=_cparams("parallel", "arbitrary"),
        name="norm_matmul_" + (act or "id"),
    )(x, g.reshape(-1, 1, K), *([w] * pieces))


def _qkv_proj_kernel(x_ref, g_ref, w_ref, o0_ref, o1_ref, o2_ref, u_ref, acc_ref):
    @pl.when(pl.program_id(1) == 0)
    def _():
        x = x_ref[...]
        ms = jnp.mean(x * x, axis=-1, keepdims=True)
        u_ref[...] = (x * lax.rsqrt(ms + RMS_EPS) * g_ref[...]).astype(BF16)

    acc = jnp.dot(u_ref[...], w_ref[...], preferred_element_type=F32)
    acc = acc * jnp.where(pl.program_id(1) == 0, HEAD_DIM ** -0.5, 1.0)
    tm = acc.shape[0]
    tiles_per_group = ATTN_OUT_DIM // LANES
    for gi, (o_ref, (_, d)) in enumerate(zip((o0_ref, o1_ref, o2_ref), ATTN_GROUPS)):
        if d == 1:
            o_ref[0, 0] = acc[:, gi * ATTN_OUT_DIM:(gi + 1) * ATTN_OUT_DIM].astype(BF16)
            continue
        for t in range(tiles_per_group):
            c = gi * tiles_per_group + t
            acc_ref[c] = acc[:, c * LANES:(c + 1) * LANES]
            for res in range(d):
                o_ref[0, res, :, t * LANES:(t + 1) * LANES] = (
                    acc_ref[c, pl.ds(res, tm // d, stride=d), :].astype(BF16))


def qkv_proj(x, g, w, layer, *, batch, tm):
    M, K = x.shape
    S = M // batch
    nb = S // tm
    outs_shape, outs_spec = [], []
    for _, d in ATTN_GROUPS:
        outs_shape.append(jax.ShapeDtypeStruct((batch, d, S // d, 3 * ATTN_OUT_DIM), BF16))
        outs_spec.append(pl.BlockSpec((1, d, tm // d, ATTN_OUT_DIM),
                                      lambda i, j: (i // nb, 0, i % nb, j)))
    return pl.pallas_call(
        _qkv_proj_kernel,
        grid=(M // tm, 3),
        in_specs=[pl.BlockSpec((tm, K), lambda i, j: (i, 0)),
                  pl.BlockSpec((None, 1, K), lambda i, j: (layer, 0, 0)),
                  pl.BlockSpec((None, K, ATTN_DIM), lambda i, j: (layer, 0, j))],
        out_specs=outs_spec,
        out_shape=outs_shape,
        scratch_shapes=[pltpu.VMEM((tm, K), BF16), pltpu.VMEM((ATTN_DIM // LANES, tm, LANES), F32)],
        compiler_params=_cparams("parallel", "arbitrary"),
        name="qkv_proj",
    )(x, g.reshape(-1, 1, K), w)


def _mm_res_kernel(x_ref, w_ref, r_ref, o_ref, acc_ref):
    k = pl.program_id(2)

    @pl.when(k == 0)
    def _():
        acc_ref[...] = jnp.zeros_like(acc_ref)

    acc_ref[...] += jnp.dot(x_ref[...], w_ref[...].astype(BF16), preferred_element_type=F32)

    @pl.when(k == pl.num_programs(2) - 1)
    def _():
        o_ref[...] = r_ref[...] + acc_ref[...]


def matmul_residual(x, w, layer, res, *, tm, tn, tk):
    M, K = x.shape
    N = w.shape[2]
    return pl.pallas_call(
        _mm_res_kernel,
        grid=(M // tm, N // tn, K // tk),
        in_specs=[pl.BlockSpec((tm, tk), lambda i, j, k: (i, k)),
                  pl.BlockSpec((None, tk, tn), lambda i, j, k: (layer, k, j)),
                  pl.BlockSpec((tm, tn), lambda i, j, k: (i, j))],
        out_specs=pl.BlockSpec((tm, tn), lambda i, j, k: (i, j)),
        out_shape=jax.ShapeDtypeStruct((M, N), F32),
        scratch_shapes=[pltpu.VMEM((tm, tn), F32)],
        compiler_params=_cparams("parallel", "parallel", "arbitrary"),
        name="matmul_residual",
    )(x, w, res)


def _rmsnorm_kernel(x_ref, g_ref, o_ref):
    x = x_ref[...]
    ms = jnp.mean(x * x, axis=-1, keepdims=True)
    o_ref[...] = x * lax.rsqrt(ms + RMS_EPS) * g_ref[...]


def rmsnorm_rows(x, g, *, tm):
    M, K = x.shape
    return pl.pallas_call(
        _rmsnorm_kernel,
        grid=(M // tm,),
        in_specs=[pl.BlockSpec((tm, K), lambda i: (i, 0)),
                  pl.BlockSpec((1, K), lambda i: (0, 0))],
        out_specs=pl.BlockSpec((tm, K), lambda i: (i, 0)),
        out_shape=jax.ShapeDtypeStruct((M, K), F32),
        compiler_params=_cparams("parallel"),
        name="final_rmsnorm",
    )(x, g.reshape(1, K))


def _token_shift(main, prev_blk, next_blk, mu, first, last):
    rows = main.shape[0]
    prev_row = jnp.where(first, 0.0, prev_blk[7:8, :])
    next_row = jnp.where(last, 0.0, next_blk[0:1, :])
    rid = lax.broadcasted_iota(jnp.int32, main.shape, 0)
    up = jnp.where(rid == 0, prev_row, pltpu.roll(main, 1, 0))
    dn = jnp.where(rid == rows - 1, next_row, pltpu.roll(main, rows - 1, 0))
    return main * (1.0 - mu) + (0.5 * mu) * (up + dn)


def _rwkv_prep_kernel(r_m, r_p, r_n, k_m, k_p, k_n, v_m, v_p, v_n, l_m, l_p, l_n,
                      mu_ref, w0_ref, a0_ref, lora_w_ref, g_up_ref, kk_ref, ka_ref, rk_ref,
                      r_o, v_o, kk_o, lwf_o, lwb_o, kaf_o, kab_o, kdf_o, kdb_o, g_o, bonus_o):
    i = pl.program_id(1)
    first = i == 0
    last = i == pl.num_programs(1) - 1
    D = RWKV_DIM
    r = _token_shift(r_m[0], r_p[0], r_n[0], mu_ref[:, 0:D], first, last)
    k = _token_shift(k_m[0], k_p[0], k_n[0], mu_ref[:, D:2 * D], first, last)
    v = _token_shift(v_m[0], v_p[0], v_n[0], mu_ref[:, 2 * D:3 * D], first, last)
    lo = _token_shift(l_m[0], l_p[0], l_n[0], mu_ref[:, 3 * D:], first, last)

    g = _dot(jax.nn.sigmoid(lo[:, :GATE_LORA]), g_up_ref[...])
    tail = lo[:, GATE_LORA:]
    col = lax.broadcasted_iota(jnp.int32, tail.shape, 1)
    tail = jnp.where(col < 2 * DECAY_LORA, jnp.tanh(tail), tail)
    up = _dot(tail, lora_w_ref[...])

    ones_bd = _head_block_ones(2 * LANES)

    def head_sum(x):
        return jnp.concatenate(
            [_head_sum(x[:, c:c + 2 * LANES], ones_bd) for c in range(0, RWKV_DIM, 2 * LANES)], axis=1)

    kk = k * kk_ref[...]
    kk = kk * jnp.minimum(lax.rsqrt(head_sum(kk * kk)), 1.0 / L2_EPS)

    r_o[0] = r.astype(r_o.dtype)
    v_o[0] = v.astype(v_o.dtype)
    kk_o[0] = kk.astype(kk_o.dtype)
    g_o[0] = g
    kd_sum = jnp.zeros_like(k)
    for d, (lw_o, ka_o, kd_o) in enumerate(((lwf_o, kaf_o, kdf_o), (lwb_o, kab_o, kdb_o))):
        z = w0_ref[d:d + 1, :] + up[:, d * RWKV_DIM:(d + 1) * RWKV_DIM]
        nz = -z
        softplus = jnp.maximum(nz, 0.0) + jnp.log(1.0 + jnp.exp(-jnp.abs(nz)))
        lw_o[0] = -jnp.exp(-softplus - 0.5)
        a = jax.nn.sigmoid(a0_ref[d:d + 1, :] + up[:, (2 + d) * RWKV_DIM:(3 + d) * RWKV_DIM])
        kd = k * (1.0 + (a - 1.0) * ka_ref[...])
        ka_o[0] = (kk * a).astype(ka_o.dtype)
        kd_o[0] = kd.astype(kd_o.dtype)
        kd_sum = kd_sum + kd
    bonus_o[0] = head_sum(r * kd_sum * rk_ref[...]) * v


def rwkv_prep(rkv, lora, layer, mu, w0, a0, lora_w, g_up, k_k, k_a, r_k, *, ts):
    B, S, _ = rkv.shape
    D = RWKV_DIM
    nblk8 = S // 8

    def main_spec(width, cblk):
        return pl.BlockSpec((1, ts, width), lambda b, i: (b, i, cblk))

    def prev_spec(width, cblk):
        return pl.BlockSpec((1, 8, width), lambda b, i: (b, jnp.maximum(i * (ts // 8) - 1, 0), cblk))

    def next_spec(width, cblk):
        return pl.BlockSpec((1, 8, width), lambda b, i: (b, jnp.minimum((i + 1) * (ts // 8), nblk8 - 1), cblk))

    def layer_spec(rows, width):
        return pl.BlockSpec((None, rows, width), lambda b, i: (layer, 0, 0))

    in_specs = []
    args = []
    for cblk in range(3):
        in_specs += [main_spec(D, cblk), prev_spec(D, cblk), next_spec(D, cblk)]
        args += [rkv, rkv, rkv]
    in_specs += [main_spec(LORA_COLS, 0), prev_spec(LORA_COLS, 0), next_spec(LORA_COLS, 0)]
    args += [lora, lora, lora]
    depth = mu.shape[0]
    in_specs += [layer_spec(1, 3 * D + LORA_COLS), layer_spec(2, D), layer_spec(2, D),
                 layer_spec(LORA_TAIL, 4 * D), layer_spec(GATE_LORA, D),
                 layer_spec(1, D), layer_spec(1, D), layer_spec(1, D)]
    args += [mu.reshape(depth, 1, -1), w0, a0, lora_w, g_up,
             k_k.reshape(depth, 1, D), k_a.reshape(depth, 1, D), r_k.reshape(depth, 1, D)]
    out_spec = pl.BlockSpec((1, ts, D), lambda b, i: (b, i, 0))
    out_dtypes = [BF16, BF16, BF16, F32, F32, BF16, BF16, BF16, BF16, F32, F32]
    n_out = len(out_dtypes)
    return pl.pallas_call(
        _rwkv_prep_kernel,
        grid=(B, S // ts),
        in_specs=in_specs,
        out_specs=[out_spec] * n_out,
        out_shape=[jax.ShapeDtypeStruct((B, S, D), dt) for dt in out_dtypes],
        compiler_params=_cparams("parallel", "arbitrary"),
        name="rwkv_prep",
    )(*args)


_BMM = (((2,), (1,)), ((0,), (0,)))
_BMM_NT = (((2,), (2,)), ((0,), (0,)))
_BMM_TN = (((1,), (1,)), ((0,), (0,)))


def _bdot(a, b, dims=_BMM):
    return lax.dot_general(a.astype(BF16), b.astype(BF16), dims, preferred_element_type=F32)


def _bd_stack(x, m0):
    zero = jnp.zeros_like(x)
    return jnp.concatenate([jnp.where(m0, x, zero), jnp.where(m0, zero, x)], axis=-2)


def _scan_chunk(r, v, kk, lw, cum, ka, kd, s_bd, consts):
    is_rev, strict2, incl2, eye2, m0_1, m0_2, bd_mask, off_masks = consts
    C = r.shape[1]
    cum_prev = cum - lw
    e_cum = jnp.exp(cum)
    e_prev = jnp.exp(cum_prev)
    e_neg = jnp.exp(-cum)
    total = jnp.where(is_rev, cum[:, 0:1, :], cum[:, C - 1:C, :])
    e_rest = jnp.exp(total - cum)
    r0 = r * e_cum
    at0 = -kk * e_prev
    bt = ka * e_neg
    kt = kd * e_neg
    bh = ka * e_rest
    kh = kd * e_rest

    gram = _bdot(jnp.concatenate([at0, r0], axis=1),
                 jnp.concatenate([_bd_stack(bt, m0_1), _bd_stack(kt, m0_1)], axis=1), _BMM_NT)
    zero = jnp.zeros_like(gram[:, :C])
    a_top = jnp.where(strict2, gram[:, :C], zero)
    a_bot = jnp.where(incl2, gram[:, C:], zero)
    a_ab = a_top[:, :, :LANES]
    a_ak = a_top[:, :, LANES:]

    zero1 = jnp.zeros_like(a_ab)
    t = eye2 + jnp.where(off_masks[0], a_ab, zero1)
    for off_mask in off_masks[1:]:
        z = _bdot(t, _bd_stack(jnp.where(off_mask, a_ab, zero1), m0_1))
        t = t + _bdot(z, _bd_stack(t, m0_1))

    av = _bdot(a_ak, _bd_stack(v, m0_1))
    wu = _bdot(t, _bd_stack(jnp.concatenate([at0, av], axis=2), m0_2))
    w_a = wu[:, :, :LANES]
    u_v = wu[:, :, LANES:]

    wr = _bdot(jnp.concatenate([w_a, r0], axis=1), s_bd, _BMM_NT)
    u = wr[:, :C] + u_v
    y = wr[:, C:] + _bdot(a_bot, jnp.concatenate([_bd_stack(u, m0_1), _bd_stack(v, m0_1)], axis=1))
    upd = _bdot(jnp.concatenate([u, v], axis=1), jnp.concatenate([bh, kh], axis=1), _BMM_TN)
    s_new = s_bd * jnp.exp(total) + jnp.where(bd_mask, upd, jnp.zeros_like(upd))
    return y, s_new


def _scan_consts(C, n_fwd, n_bwd):
    G = n_fwd + n_bwd
    is_rev = lax.broadcasted_iota(jnp.int32, (G, 1, 1), 0) >= n_fwd
    sign = jnp.where(is_rev, -1, 1)
    row2 = lax.broadcasted_iota(jnp.int32, (G, C, 2 * LANES), 1)
    col2 = lax.broadcasted_iota(jnp.int32, (G, C, 2 * LANES), 2) % HEAD_DIM
    strict2 = (col2 - row2) * sign < 0
    incl2 = (col2 - row2) * sign <= 0
    lane1 = lax.broadcasted_iota(jnp.int32, (1, C, LANES), 2)
    lane2 = lax.broadcasted_iota(jnp.int32, (1, C, 2 * LANES), 2) % LANES
    m0_1 = lane1 < HEAD_DIM
    m0_2 = lane2 < HEAD_DIM
    rb = lax.broadcasted_iota(jnp.int32, (1, LANES, LANES), 1) // HEAD_DIM
    cb = lax.broadcasted_iota(jnp.int32, (1, LANES, LANES), 2) // HEAD_DIM
    bd_mask = rb == cb
    row1 = lax.broadcasted_iota(jnp.int32, (1, C, LANES), 1)
    col1 = lane1 % HEAD_DIM
    eye2 = (row1 == col1).astype(F32)
    off_masks = [row1 // 2 == col1 // 2]
    b = 2
    while b < C:
        off_masks.append((row1 // (2 * b) == col1 // (2 * b)) & (row1 // b != col1 // b))
        b *= 2
    return (is_rev, strict2, incl2, eye2, m0_1, m0_2, bd_mask, off_masks)


def _rwkv_scan_kernel(rf, vf, kkf, lwf, kaf, kdf, rb, vb, kkb, lwb, kab, kdb,
                      yf_o, yb_o, state_ref):
    c = pl.program_id(0)

    @pl.when(c == 0)
    def _():
        state_ref[...] = jnp.zeros_like(state_ref)

    nb, C = rf.shape[0], rf.shape[1]
    n_pairs = rf.shape[2] // LANES
    n_dir = nb * n_pairs
    consts = _scan_consts(C, n_dir, n_dir)

    row = lax.broadcasted_iota(jnp.int32, (C, C), 0)
    col = lax.broadcasted_iota(jnp.int32, (C, C), 1)
    tri_f = (col <= row).astype(F32).astype(BF16)
    tri_b = (col >= row).astype(F32).astype(BF16)
    cum_f = [_dot_exact_rhs(tri_f, lwf[b]) for b in range(nb)]
    cum_b = [_dot_exact_rhs(tri_b, lwb[b]) for b in range(nb)]

    def chains(xs_f, xs_b):
        return jnp.stack([x[:, p * LANES:(p + 1) * LANES]
                          for xs in (xs_f, xs_b) for x in xs for p in range(n_pairs)], axis=0)

    def rows(ref):
        return [ref[b].astype(F32) for b in range(nb)]

    r, v, kk, lw, ka, kd = (chains(rows(f), rows(b)) for f, b in
                            ((rf, rb), (vf, vb), (kkf, kkb), (lwf, lwb), (kaf, kab), (kdf, kdb)))
    y, s_new = _scan_chunk(r, v, kk, lw, chains(cum_f, cum_b), ka, kd, state_ref[...], consts)
    state_ref[...] = s_new
    for d, y_o in enumerate((yf_o, yb_o)):
        for b in range(nb):
            for p in range(n_pairs):
                y_o[b, :, p * LANES:(p + 1) * LANES] = y[(d * nb + b) * n_pairs + p]


def rwkv_scan(r, v, kk, lwf, lwb, kaf, kab, kdf, kdb):
    B, S, D = r.shape
    C = CHUNK
    nc = S // C
    fwd = pl.BlockSpec((B, C, D), lambda c: (0, c, 0))
    bwd = pl.BlockSpec((B, C, D), lambda c: (0, nc - 1 - c, 0))
    return pl.pallas_call(
        _rwkv_scan_kernel,
        grid=(nc,),
        in_specs=[fwd] * 6 + [bwd] * 6,
        out_specs=[fwd, bwd],
        out_shape=[jax.ShapeDtypeStruct((B, S, D), F32)] * 2,
        scratch_shapes=[pltpu.VMEM((2 * B * (D // LANES), LANES, LANES), F32)],
        compiler_params=_cparams("arbitrary"),
        name="rwkv_scan",
    )(r, v, kk, lwf, kaf, kdf, r, v, kk, lwb, kab, kdb)


def _attn_kernel(table_ref, bucket_ref, q_ref, kp_ref, km_ref, kn_ref, vp_ref, vm_ref, vn_ref,
                 o_ref, lse_ref, bias_ref, *, nblk):
    i = pl.program_id(2)
    TQ = q_ref.shape[0]
    TK = TQ + 2 * ATTN_HALF

    @pl.when((pl.program_id(0) == 0) & (pl.program_id(1) == 0) & (i == 0))
    def _():
        bucket = bucket_ref[...]
        row = lax.broadcasted_iota(jnp.int32, (TQ, TK), 0)
        colk = lax.broadcasted_iota(jnp.int32, (TQ, TK), 1)
        band = jnp.abs(colk - ATTN_HALF - row) <= ATTN_HALF
        has_prev = colk >= ATTN_HALF
        has_next = colk < TQ + ATTN_HALF
        for h in range(ATTN_SLOTS):
            acc = jnp.zeros((TQ, TK), F32)
            for b in range(N_BUCKETS):
                acc = jnp.where(bucket == b, table_ref[h * N_BUCKETS + b], acc)
            acc = jnp.where(band, acc, NEG_BIG)
            bias_ref[0, h] = acc
            bias_ref[1, h] = jnp.where(has_prev, acc, NEG_BIG)
            bias_ref[2, h] = jnp.where(has_next, acc, NEG_BIG)
            bias_ref[3, h] = jnp.where(has_prev & has_next, acc, NEG_BIG)

    variant = jnp.where(i == 0, 1, 0) + jnp.where(i == nblk - 1, 2, 0)
    m0 = lax.broadcasted_iota(jnp.int32, (TQ, LANES), 1) < HEAD_DIM
    lane = lax.broadcasted_iota(jnp.int32, (1, LANES), 1)
    head_keep = [(lane < HEAD_DIM).astype(F32).astype(BF16), (lane >= HEAD_DIM).astype(F32).astype(BF16)]
    for p in range(ATTN_OUT_DIM // LANES):
        sl = slice(p * LANES, (p + 1) * LANES)
        q2 = q_ref[:, sl]
        kwin = jnp.concatenate([kp_ref[:, sl], km_ref[:, sl], kn_ref[:, sl]], axis=0)
        vwin = jnp.concatenate([vp_ref[:, sl], vm_ref[:, sl], vn_ref[:, sl]], axis=0)
        outs, lses = [], []
        for hh in range(2):
            qm = q2 * head_keep[hh]
            s = lax.dot_general(qm, kwin, (((1,), (1,)), ((), ())), preferred_element_type=F32)
            s = s + bias_ref[variant, 2 * p + hh]
            m = jnp.max(s, axis=-1, keepdims=True)
            e = jnp.exp(s - m)
            den = jnp.sum(e, axis=-1, keepdims=True)
            pv = jnp.dot(e.astype(BF16), vwin, preferred_element_type=F32)
            outs.append(pv / den)
            lses.append(jnp.broadcast_to(m + jnp.log(den), (TQ, LANES)))
        o_ref[:, sl] = jnp.where(m0, outs[0], outs[1]).astype(o_ref.dtype)
        lse_ref[:, sl] = jnp.where(m0, lses[0], lses[1])


def _t5_bucket(rel):
    nb = N_BUCKETS // 2
    max_exact = nb // 2
    ret = jnp.where(rel > 0, nb, 0)
    n = jnp.abs(rel)
    nf = jnp.maximum(n, 1).astype(jnp.float32)
    large = max_exact + (jnp.log(nf / max_exact) / math.log(MAX_DISTANCE / max_exact)
                         * (nb - max_exact)).astype(jnp.int32)
    large = jnp.minimum(large, nb - 1)
    return ret + jnp.where(n < max_exact, n, large)


def attention_bias_inputs(table, gi, dilation, *, tq):
    tk = tq + 2 * ATTN_HALF
    rel = jnp.arange(tk)[None, :] - ATTN_HALF - jnp.arange(tq)[:, None]
    bucket = _t5_bucket(rel * dilation).astype(jnp.int32)
    tbl = table[:, gi * ATTN_SLOTS:(gi + 1) * ATTN_SLOTS].astype(F32).T.reshape(-1)
    return tbl, bucket


def dilated_attention(qkv, tbl, bucket, gi, *, tq):
    B, dilation, L, _ = qkv.shape
    nblk = L // tq
    hb = tq // ATTN_HALF
    nhalf = L // ATTN_HALF
    W = ATTN_OUT_DIM

    def main(off):
        return pl.BlockSpec((None, None, tq, W), lambda b, r, i: (b, r, i, off))

    def prev(off):
        return pl.BlockSpec((None, None, ATTN_HALF, W),
                            lambda b, r, i: (b, r, jnp.maximum(i * hb - 1, 0), off))

    def nxt(off):
        return pl.BlockSpec((None, None, ATTN_HALF, W),
                            lambda b, r, i: (b, r, jnp.minimum((i + 1) * hb, nhalf - 1), off))

    tk = tq + 2 * ATTN_HALF
    out_spec = pl.BlockSpec((None, None, tq, W), lambda b, r, i: (b, r, i, 0))
    return pl.pallas_call(
        functools.partial(_attn_kernel, nblk=nblk),
        grid=(B, dilation, nblk),
        in_specs=[pl.BlockSpec(memory_space=pltpu.SMEM),
                  pl.BlockSpec((tq, tk), lambda b, r, i: (0, 0)),
                  main(0), prev(1), main(1), nxt(1), prev(2), main(2), nxt(2)],
        out_specs=[out_spec, out_spec],
        out_shape=[jax.ShapeDtypeStruct((B, dilation, L, W), BF16),
                   jax.ShapeDtypeStruct((B, dilation, L, W), F32)],
        scratch_shapes=[pltpu.VMEM((4, ATTN_SLOTS, tq, tk), F32)],
        compiler_params=_cparams("arbitrary", "arbitrary", "arbitrary"),
        name=f"dilated_attn_g{gi}",
    )(tbl, bucket, qkv, qkv, qkv, qkv, qkv, qkv, qkv)


def _branch_post_kernel(yf_ref, yb_ref, bonus_ref, g_ref, gnw_ref, gnb_ref,
                        o0, o1, o2, l0, l1, l2, orw_ref, oat_ref, *scratch):
    y = yf_ref[...] + yb_ref[...]
    ones_bd = _head_block_ones(2 * LANES)
    tm, width = y.shape

    def head_mean(x):
        return jnp.concatenate(
            [_head_sum(x[:, c:c + 2 * LANES], ones_bd) for c in range(0, width, 2 * LANES)],
            axis=1) * (1.0 / HEAD_DIM)

    mean = head_mean(y)
    yc = y - mean
    var = head_mean(yc * yc)
    yn = yc * lax.rsqrt(var + GN_EPS) * gnw_ref[...] + gnb_ref[...]
    orw_ref[...] = ((yn + bonus_ref[...]) * g_ref[...]).astype(BF16)

    def natural_order(ref, scr):
        d = ref.shape[0]
        for t in range(ATTN_OUT_DIM // LANES):
            for res in range(d):
                scr[t, pl.ds(res, tm // d, stride=d), :] = (
                    ref[res, :, t * LANES:(t + 1) * LANES].astype(F32))
        return jnp.concatenate([scr[t] for t in range(ATTN_OUT_DIM // LANES)], axis=1)

    s_o1, s_o2, s_l1, s_l2 = scratch
    oa, ob, oc = o0[0].astype(F32), natural_order(o1, s_o1), natural_order(o2, s_o2)
    la, lb, lc = l0[0], natural_order(l1, s_l1), natural_order(l2, s_l2)
    m = jnp.maximum(jnp.maximum(la, lb), lc)
    ea, eb, ec = jnp.exp(la - m), jnp.exp(lb - m), jnp.exp(lc - m)
    den = ea + eb + ec
    oat_ref[...] = ((ea * oa + eb * ob + ec * oc) / den).astype(BF16)


def branch_post(yf, yb, bonus, g, gn_w, gn_b, layer, outs, lses, *, tm):
    M, D = yf.shape
    A = ATTN_OUT_DIM
    batch = outs[0].shape[0]
    nb = M // batch // tm
    big = pl.BlockSpec((tm, D), lambda b, i: (b * nb + i, 0))
    rowp = pl.BlockSpec((None, 1, D), lambda b, i: (layer, 0, 0))
    attn_specs = [pl.BlockSpec((None, o.shape[1], tm // o.shape[1], A), lambda b, i: (b, 0, i, 0))
                  for o in outs]
    return pl.pallas_call(
        _branch_post_kernel,
        grid=(batch, nb),
        in_specs=[big] * 4 + [rowp, rowp] + attn_specs + attn_specs,
        out_specs=[big, pl.BlockSpec((tm, A), lambda b, i: (b * nb + i, 0))],
        out_shape=[jax.ShapeDtypeStruct((M, D), BF16), jax.ShapeDtypeStruct((M, A), BF16)],
        scratch_shapes=[pltpu.VMEM((A // LANES, tm, LANES), F32)] * 4,
        compiler_params=_cparams("parallel", "arbitrary"),
        name="branch_post",
    )(yf, yb, bonus, g, gn_w.reshape(-1, 1, D), gn_b.reshape(-1, 1, D), *outs, *lses)


def _merge_kernel(orw_ref, oat_ref, wr_ref, wa_ref, gr_ref, ga_ref, o_ref):
    a = jnp.dot(orw_ref[...], wr_ref[...].astype(BF16), preferred_element_type=F32)
    b = jnp.dot(oat_ref[...], wa_ref[...].astype(BF16), preferred_element_type=F32)
    o_ref[...] = (gr_ref[...].astype(F32) * a + ga_ref[...].astype(F32) * b).astype(o_ref.dtype)


def branch_merge(orw, oat, w_r, w_a, layer, gates, *, tm, tn):
    M = orw.shape[0]
    N = w_r.shape[2]
    nj = N // tn
    return pl.pallas_call(
        _merge_kernel,
        grid=(M // tm, nj),
        in_specs=[pl.BlockSpec((tm, orw.shape[1]), lambda i, j: (i, 0)),
                  pl.BlockSpec((tm, oat.shape[1]), lambda i, j: (i, 0)),
                  pl.BlockSpec((None, w_r.shape[1], tn), lambda i, j: (layer, 0, j)),
                  pl.BlockSpec((None, w_a.shape[1], tn), lambda i, j: (layer, 0, j)),
                  pl.BlockSpec((tm, tn), lambda i, j: (i, j)),
                  pl.BlockSpec((tm, tn), lambda i, j: (i, j + nj))],
        out_specs=pl.BlockSpec((tm, tn), lambda i, j: (i, j)),
        out_shape=jax.ShapeDtypeStruct((M, N), BF16),
        compiler_params=_cparams("parallel", "arbitrary"),
        name="branch_merge",
    )(orw, oat, w_r, w_a, gates, gates)


def _lora_weights(w_up, a_up):
    z = jnp.zeros_like(w_up[:, 0])
    rows = [jnp.concatenate([w_up[:, 0], z, z, z], axis=2),
            jnp.concatenate([z, w_up[:, 1], z, z], axis=2),
            jnp.concatenate([z, z, a_up[:, 0], z], axis=2),
            jnp.concatenate([z, z, z, a_up[:, 1]], axis=2)]
    return jnp.concatenate(rows, axis=1).astype(BF16)


def kernel(x, norm1_g, w_in, tshift_mu, w0, w_lora_up, a0, a_lora_up, g_lora_up, k_k, k_a, r_k,
           gn_w, gn_b, rel_bias, w_branch_rwkv, w_branch_attn, w_out, norm2_g, w_mlp_in, w_mlp_out,
           final_g):
    B, S, D = x.shape
    M = B * S
    depth = w_in.shape[0]
    c_rkv = 3 * RWKV_DIM
    c_slab = c_rkv + LORA_COLS
    c_attn = c_slab + 3 * ATTN_DIM
    w_qkv = w_in[:, :, c_slab:c_attn].astype(BF16)
    w_lora_up = _lora_weights(w_lora_up, a_lora_up)
    g_up = g_lora_up.astype(BF16)
    bias_inputs = [attention_bias_inputs(rel_bias, gi, dilation, tq=128)
                   for gi, (_, dilation) in enumerate(ATTN_GROUPS)]
    h = x.reshape(M, D)
    for l in range(depth):
        g1 = norm1_g
        rkv = norm_matmul(h, g1, w_in, l, n_cols=c_rkv, tm=1024, tn=1024)
        lora = norm_matmul(h, g1, w_in, l, col0=c_rkv, n_cols=LORA_COLS, tm=1024, tn=LORA_COLS,
                           pieces=LORA_COLS // LANES)
        qkv_groups = qkv_proj(h, g1, w_qkv, l, batch=B, tm=1024)
        gates = norm_matmul(h, g1, w_in, l, col0=c_attn, n_cols=2 * D, tm=1024, tn=1024,
                            pieces=1024 // LANES, act="sigmoid", out_dtype=BF16)

        (r, v, kk, lwf, lwb, kaf, kab, kdf, kdb, g, bonus) = rwkv_prep(
            rkv.reshape(B, S, c_rkv), lora.reshape(B, S, LORA_COLS), l, tshift_mu, w0, a0,
            w_lora_up, g_up, k_k, k_a, r_k, ts=256)
        yf, yb = rwkv_scan(r, v, kk, lwf, lwb, kaf, kab, kdf, kdb)

        outs, lses = [], []
        for gi, (window, dilation) in enumerate(ATTN_GROUPS):
            assert window // (2 * dilation) == ATTN_HALF
            o, lse = dilated_attention(qkv_groups[gi], *bias_inputs[gi], gi, tq=128)
            outs.append(o)
            lses.append(lse)

        orw, oat = branch_post(yf.reshape(M, RWKV_DIM), yb.reshape(M, RWKV_DIM),
                               bonus.reshape(M, RWKV_DIM), g.reshape(M, RWKV_DIM),
                               gn_w, gn_b, l, outs, lses, tm=512)
        merged = branch_merge(orw, oat, w_branch_rwkv, w_branch_attn, l, gates, tm=1024, tn=1024)
        h = matmul_residual(merged, w_out, l, h, tm=1024, tn=1024, tk=D)

        act = norm_matmul(h, norm2_g, w_mlp_in, l, n_cols=w_mlp_in.shape[2], tm=1024, tn=1024,
                          act="relu2", out_dtype=BF16)
        h = matmul_residual(act, w_mlp_out, l, h, tm=1024, tn=1024, tk=2048)
    out = rmsnorm_rows(h, final_g, tm=512)
    return out.reshape(B, S, D)
```

```python
import functools
import math

import jax
import jax.numpy as jnp
from jax import lax
from jax.experimental import pallas as pl
from jax.experimental.pallas import tpu as pltpu

F32 = jnp.float32
BF16 = jnp.bfloat16

HEAD_DIM = 64
LANES = 128
RWKV_DIM = 1024
DECAY_LORA = 96
AAA_LORA = 96
GATE_LORA = 256
LORA_COLS = GATE_LORA + 2 * DECAY_LORA + 2 * AAA_LORA
LORA_TAIL = LORA_COLS - GATE_LORA
ATTN_GROUPS = ((128, 1), (512, 4), (2048, 16))
ATTN_SLOTS = 8
ATTN_DIM = 1536
ATTN_OUT_DIM = ATTN_SLOTS * HEAD_DIM
ATTN_HALF = 64
ATTN_SUB = 128
N_BUCKETS = 32
MAX_DISTANCE = 1024
RMS_EPS = 1e-6
GN_EPS = 64e-5
L2_EPS = 1e-12
NEG_BIG = -1e30
CHUNK = 64
VMEM_LIMIT = 56 * 1024 * 1024


def _cparams(*sem):
    return pltpu.CompilerParams(dimension_semantics=sem, vmem_limit_bytes=VMEM_LIMIT)


def _dot(a, b):
    return jnp.dot(a.astype(BF16), b.astype(BF16), preferred_element_type=F32)


def _split3(x):
    p0 = x.astype(BF16)
    r1 = x - p0.astype(F32)
    p1 = r1.astype(BF16)
    p2 = (r1 - p1.astype(F32)).astype(BF16)
    return p0, p1, p2


def _dot_exact_rhs(a_bf16, x):
    return sum(jnp.dot(a_bf16, p, preferred_element_type=F32) for p in _split3(x))


def _head_sum(x, ones_bd):
    hi = x.astype(BF16)
    lo = (x - hi.astype(F32)).astype(BF16)
    return (jnp.dot(hi, ones_bd, preferred_element_type=F32)
            + jnp.dot(lo, ones_bd, preferred_element_type=F32))


def _head_block_ones(width):
    r = lax.broadcasted_iota(jnp.int32, (width, width), 0) // HEAD_DIM
    c = lax.broadcasted_iota(jnp.int32, (width, width), 1) // HEAD_DIM
    return (r == c).astype(BF16)


def _normalise_rows(x_ref, g_ref, u_ref, start, size):
    rows = pl.ds(start, size)
    x = x_ref[rows, :]
    ms = jnp.mean(x * x, axis=-1, keepdims=True)
    u_ref[rows, :] = (x * lax.rsqrt(ms + RMS_EPS) * g_ref[...]).astype(BF16)


def _x_rows_index(i, j, ni, nj):
    if nj == 1:
        return i
    return jnp.minimum(i + jnp.where(j >= 1, 1, 0), ni - 1)


BF16_ROWS = 16


def _with_normalised_tile(x_ref, g_ref, u_refs, nj, body):
    i, j = pl.program_id(0), pl.program_id(1)
    tm = x_ref.shape[0]
    if nj == 1:
        _normalise_rows(x_ref, g_ref, u_refs[0], 0, tm)
        body(u_refs[0])
        return

    @pl.when((i == 0) & (j == 0))
    def _():
        _normalise_rows(x_ref, g_ref, u_refs[0], 0, tm)

    size = -(-tm // (nj - 1))
    size = -(-size // BF16_ROWS) * BF16_ROWS
    start = pl.multiple_of(jnp.clip((j - 1) * size, 0, tm - size), BF16_ROWS)
    for parity in range(2):
        @pl.when(i % 2 == parity)
        def _():
            body(u_refs[parity])
            _normalise_rows(x_ref, g_ref, u_refs[1 - parity], start, size)


def _norm_mm_kernel(x_ref, g_ref, *refs, act, nj):
    *w_refs, o_ref, u0_ref, u1_ref = refs

    def body(u_ref):
        w = jnp.concatenate([w_ref[...].astype(BF16) for w_ref in w_refs], axis=1)
        acc = jnp.dot(u_ref[...], w, preferred_element_type=F32)
        if act == "relu2":
            acc = jnp.square(jnp.maximum(acc, 0.0))
        elif act == "sigmoid":
            acc = 0.5 * jnp.tanh(0.5 * acc) + 0.5
        o_ref[...] = acc.astype(o_ref.dtype)

    _with_normalised_tile(x_ref, g_ref, (u0_ref, u1_ref), nj, body)


def norm_matmul(x, g, w, layer, *, col0=0, n_cols, tm, tn, pieces=1, act=None, out_dtype=F32):
    M, K = x.shape
    pw = tn // pieces
    assert pw * pieces == tn and col0 % pw == 0 and n_cols % tn == 0
    w_specs = [pl.BlockSpec((None, K, pw), functools.partial(
        lambda i, j, t: (layer, 0, col0 // pw + j * pieces + t), t=t)) for t in range(pieces)]
    ni, nj = M // tm, n_cols // tn
    return pl.pallas_call(
        functools.partial(_norm_mm_kernel, act=act, nj=nj),
        grid=(ni, nj),
        in_specs=[pl.BlockSpec((tm, K), lambda i, j: (_x_rows_index(i, j, ni, nj), 0)),
                  pl.BlockSpec((None, 1, K), lambda i, j: (layer, 0, 0))] + w_specs,
        out_specs=pl.BlockSpec((tm, tn), lambda i, j: (i, j)),
        out_shape=jax.ShapeDtypeStruct((M, n_cols), out_dtype),
        scratch_shapes=[pltpu.VMEM((tm, K), BF16)] * 2,
        compiler_params=_cparams("arbitrary", "arbitrary"),
        name="norm_matmul_" + (act or "id"),
    )(x, g.reshape(-1, 1, K), *([w] * pieces))


def _qkv_proj_kernel(x_ref, g_ref, w_ref, o0_ref, o1_ref, o2_ref, u0_ref, u1_ref, acc_ref):
    def body(u_ref):
        acc = jnp.dot(u_ref[...], w_ref[...], preferred_element_type=F32)
        acc = acc * jnp.where(pl.program_id(1) == 0, HEAD_DIM ** -0.5, 1.0)
        tm = acc.shape[0]
        tiles_per_group = ATTN_OUT_DIM // LANES
        for gi, (o_ref, (_, d)) in enumerate(zip((o0_ref, o1_ref, o2_ref), ATTN_GROUPS)):
            if d == 1:
                o_ref[0, 0] = acc[:, gi * ATTN_OUT_DIM:(gi + 1) * ATTN_OUT_DIM].astype(BF16)
                continue
            for t in range(tiles_per_group):
                c = gi * tiles_per_group + t
                acc_ref[c] = acc[:, c * LANES:(c + 1) * LANES]
                for res in range(d):
                    o_ref[0, res, :, t * LANES:(t + 1) * LANES] = (
                        acc_ref[c, pl.ds(res, tm // d, stride=d), :].astype(BF16))

    _with_normalised_tile(x_ref, g_ref, (u0_ref, u1_ref), 3, body)


def qkv_proj(x, g, w, layer, *, batch, tm):
    M, K = x.shape
    S = M // batch
    nb = S // tm
    outs_shape, outs_spec = [], []
    for _, d in ATTN_GROUPS:
        outs_shape.append(jax.ShapeDtypeStruct((batch, d, S // d, 3 * ATTN_OUT_DIM), BF16))
        outs_spec.append(pl.BlockSpec((1, d, tm // d, ATTN_OUT_DIM),
                                      lambda i, j: (i // nb, 0, i % nb, j)))
    ni = M // tm
    return pl.pallas_call(
        _qkv_proj_kernel,
        grid=(ni, 3),
        in_specs=[pl.BlockSpec((tm, K), lambda i, j: (_x_rows_index(i, j, ni, 3), 0)),
                  pl.BlockSpec((None, 1, K), lambda i, j: (layer, 0, 0)),
                  pl.BlockSpec((None, K, ATTN_DIM), lambda i, j: (layer, 0, j))],
        out_specs=outs_spec,
        out_shape=outs_shape,
        scratch_shapes=[pltpu.VMEM((tm, K), BF16), pltpu.VMEM((tm, K), BF16),
                        pltpu.VMEM((ATTN_DIM // LANES, tm, LANES), F32)],
        compiler_params=_cparams("arbitrary", "arbitrary"),
        name="qkv_proj",
    )(x, g.reshape(-1, 1, K), w)


def _mm_res_kernel(x_ref, w_ref, r_ref, o_ref, acc_ref):
    k = pl.program_id(2)

    @pl.when(k == 0)
    def _():
        acc_ref[...] = jnp.zeros_like(acc_ref)

    acc_ref[...] += jnp.dot(x_ref[...], w_ref[...].astype(BF16), preferred_element_type=F32)

    @pl.when(k == pl.num_programs(2) - 1)
    def _():
        o_ref[...] = r_ref[...] + acc_ref[...]


def matmul_residual(x, w, layer, res, *, tm, tn, tk):
    M, K = x.shape
    N = w.shape[2]
    return pl.pallas_call(
        _mm_res_kernel,
        grid=(M // tm, N // tn, K // tk),
        in_specs=[pl.BlockSpec((tm, tk), lambda i, j, k: (i, k)),
                  pl.BlockSpec((None, tk, tn), lambda i, j, k: (layer, k, j)),
                  pl.BlockSpec((tm, tn), lambda i, j, k: (i, j))],
        out_specs=pl.BlockSpec((tm, tn), lambda i, j, k: (i, j)),
        out_shape=jax.ShapeDtypeStruct((M, N), F32),
        scratch_shapes=[pltpu.VMEM((tm, tn), F32)],
        compiler_params=_cparams("parallel", "parallel", "arbitrary"),
        name="matmul_residual",
    )(x, w, res)


def _rmsnorm_kernel(x_ref, g_ref, o_ref):
    x = x_ref[...]
    ms = jnp.mean(x * x, axis=-1, keepdims=True)
    o_ref[...] = x * lax.rsqrt(ms + RMS_EPS) * g_ref[...]


def rmsnorm_rows(x, g, *, tm):
    M, K = x.shape
    return pl.pallas_call(
        _rmsnorm_kernel,
        grid=(M // tm,),
        in_specs=[pl.BlockSpec((tm, K), lambda i: (i, 0)),
                  pl.BlockSpec((1, K), lambda i: (0, 0))],
        out_specs=pl.BlockSpec((tm, K), lambda i: (i, 0)),
        out_shape=jax.ShapeDtypeStruct((M, K), F32),
        compiler_params=_cparams("parallel"),
        name="final_rmsnorm",
    )(x, g.reshape(1, K))


def _token_shift(main, prev_blk, next_blk, mu, first, last):
    rows = main.shape[0]
    prev_row = jnp.where(first, 0.0, prev_blk[7:8, :])
    next_row = jnp.where(last, 0.0, next_blk[0:1, :])
    rid = lax.broadcasted_iota(jnp.int32, main.shape, 0)
    up = jnp.where(rid == 0, prev_row, pltpu.roll(main, 1, 0))
    dn = jnp.where(rid == rows - 1, next_row, pltpu.roll(main, rows - 1, 0))
    return main * (1.0 - mu) + (0.5 * mu) * (up + dn)


def _rwkv_prep_kernel(r_m, r_p, r_n, k_m, k_p, k_n, v_m, v_p, v_n, l_m, l_p, l_n,
                      mu_ref, w0_ref, a0_ref, lora_w_ref, g_up_ref, kk_ref, ka_ref, rk_ref,
                      r_o, v_o, kk_o, lwf_o, lwb_o, kaf_o, kab_o, kdf_o, kdb_o, g_o, bonus_o):
    i = pl.program_id(1)
    first = i == 0
    last = i == pl.num_programs(1) - 1
    D = RWKV_DIM
    r = _token_shift(r_m[0], r_p[0], r_n[0], mu_ref[:, 0:D], first, last)
    k = _token_shift(k_m[0], k_p[0], k_n[0], mu_ref[:, D:2 * D], first, last)
    v = _token_shift(v_m[0], v_p[0], v_n[0], mu_ref[:, 2 * D:3 * D], first, last)
    lo = _token_shift(l_m[0], l_p[0], l_n[0], mu_ref[:, 3 * D:], first, last)

    g = _dot(jax.nn.sigmoid(lo[:, :GATE_LORA]), g_up_ref[...])
    tail = lo[:, GATE_LORA:]
    col = lax.broadcasted_iota(jnp.int32, tail.shape, 1)
    tail = jnp.where(col < 2 * DECAY_LORA, jnp.tanh(tail), tail)
    up = _dot(tail, lora_w_ref[...])

    ones_bd = _head_block_ones(2 * LANES)

    def head_sum(x):
        return jnp.concatenate(
            [_head_sum(x[:, c:c + 2 * LANES], ones_bd) for c in range(0, RWKV_DIM, 2 * LANES)], axis=1)

    kk = k * kk_ref[...]
    kk = kk * jnp.minimum(lax.rsqrt(head_sum(kk * kk)), 1.0 / L2_EPS)

    r_o[0] = r.astype(r_o.dtype)
    v_o[0] = v.astype(v_o.dtype)
    kk_o[0] = kk.astype(kk_o.dtype)
    g_o[0] = g
    kd_sum = jnp.zeros_like(k)
    for d, (lw_o, ka_o, kd_o) in enumerate(((lwf_o, kaf_o, kdf_o), (lwb_o, kab_o, kdb_o))):
        z = w0_ref[d:d + 1, :] + up[:, d * RWKV_DIM:(d + 1) * RWKV_DIM]
        nz = -z
        softplus = jnp.maximum(nz, 0.0) + jnp.log(1.0 + jnp.exp(-jnp.abs(nz)))
        lw_o[0] = -jnp.exp(-softplus - 0.5)
        a = jax.nn.sigmoid(a0_ref[d:d + 1, :] + up[:, (2 + d) * RWKV_DIM:(3 + d) * RWKV_DIM])
        kd = k * (1.0 + (a - 1.0) * ka_ref[...])
        ka_o[0] = (kk * a).astype(ka_o.dtype)
        kd_o[0] = kd.astype(kd_o.dtype)
        kd_sum = kd_sum + kd
    bonus_o[0] = head_sum(r * kd_sum * rk_ref[...]) * v


def rwkv_prep(rkv, lora, layer, mu, w0, a0, lora_w, g_up, k_k, k_a, r_k, *, ts):
    B, S, _ = rkv.shape
    D = RWKV_DIM
    nblk8 = S // 8

    def main_spec(width, cblk):
        return pl.BlockSpec((1, ts, width), lambda b, i: (b, i, cblk))

    def prev_spec(width, cblk):
        return pl.BlockSpec((1, 8, width), lambda b, i: (b, jnp.maximum(i * (ts // 8) - 1, 0), cblk))

    def next_spec(width, cblk):
        return pl.BlockSpec((1, 8, width), lambda b, i: (b, jnp.minimum((i + 1) * (ts // 8), nblk8 - 1), cblk))

    def layer_spec(rows, width):
        return pl.BlockSpec((None, rows, width), lambda b, i: (layer, 0, 0))

    in_specs = []
    args = []
    for cblk in range(3):
        in_specs += [main_spec(D, cblk), prev_spec(D, cblk), next_spec(D, cblk)]
        args += [rkv, rkv, rkv]
    in_specs += [main_spec(LORA_COLS, 0), prev_spec(LORA_COLS, 0), next_spec(LORA_COLS, 0)]
    args += [lora, lora, lora]
    depth = mu.shape[0]
    in_specs += [layer_spec(1, 3 * D + LORA_COLS), layer_spec(2, D), layer_spec(2, D),
                 layer_spec(LORA_TAIL, 4 * D), layer_spec(GATE_LORA, D),
                 layer_spec(1, D), layer_spec(1, D), layer_spec(1, D)]
    args += [mu.reshape(depth, 1, -1), w0, a0, lora_w, g_up,
             k_k.reshape(depth, 1, D), k_a.reshape(depth, 1, D), r_k.reshape(depth, 1, D)]
    out_spec = pl.BlockSpec((1, ts, D), lambda b, i: (b, i, 0))
    out_dtypes = [BF16, BF16, BF16, F32, F32, BF16, BF16, BF16, BF16, F32, F32]
    n_out = len(out_dtypes)
    return pl.pallas_call(
        _rwkv_prep_kernel,
        grid=(B, S // ts),
        in_specs=in_specs,
        out_specs=[out_spec] * n_out,
        out_shape=[jax.ShapeDtypeStruct((B, S, D), dt) for dt in out_dtypes],
        compiler_params=_cparams("parallel", "arbitrary"),
        name="rwkv_prep",
    )(*args)


_BMM = (((2,), (1,)), ((0,), (0,)))
_BMM_NT = (((2,), (2,)), ((0,), (0,)))
_BMM_TN = (((1,), (1,)), ((0,), (0,)))


def _bdot(a, b, dims=_BMM):
    return lax.dot_general(a.astype(BF16), b.astype(BF16), dims, preferred_element_type=F32)


def _bd_stack(x, m0):
    zero = jnp.zeros_like(x)
    return jnp.concatenate([jnp.where(m0, x, zero), jnp.where(m0, zero, x)], axis=-2)


def _scan_chunk(r, v, kk, lw, cum, ka, kd, s_bd, consts):
    is_rev, strict2, incl2, eye2, m0_1, m0_2, bd_mask, off_masks = consts
    C = r.shape[1]
    cum_prev = cum - lw
    e_cum = jnp.exp(cum)
    e_prev = jnp.exp(cum_prev)
    e_neg = jnp.exp(-cum)
    total = jnp.where(is_rev, cum[:, 0:1, :], cum[:, C - 1:C, :])
    e_rest = jnp.exp(total - cum)
    r0 = r * e_cum
    at0 = -kk * e_prev
    bt = ka * e_neg
    kt = kd * e_neg
    bh = ka * e_rest
    kh = kd * e_rest

    gram = _bdot(jnp.concatenate([at0, r0], axis=1),
                 jnp.concatenate([_bd_stack(bt, m0_1), _bd_stack(kt, m0_1)], axis=1), _BMM_NT)
    zero = jnp.zeros_like(gram[:, :C])
    a_top = jnp.where(strict2, gram[:, :C], zero)
    a_bot = jnp.where(incl2, gram[:, C:], zero)
    a_ab = a_top[:, :, :LANES]
    a_ak = a_top[:, :, LANES:]

    zero1 = jnp.zeros_like(a_ab)
    t = eye2 + jnp.where(off_masks[0], a_ab, zero1)
    for off_mask in off_masks[1:]:
        z = _bdot(t, _bd_stack(jnp.where(off_mask, a_ab, zero1), m0_1))
        t = t + _bdot(z, _bd_stack(t, m0_1))

    av = _bdot(a_ak, _bd_stack(v, m0_1))
    wu = _bdot(t, _bd_stack(jnp.concatenate([at0, av], axis=2), m0_2))
    w_a = wu[:, :, :LANES]
    u_v = wu[:, :, LANES:]

    wr = _bdot(jnp.concatenate([w_a, r0], axis=1), s_bd, _BMM_NT)
    u = wr[:, :C] + u_v
    y = wr[:, C:] + _bdot(a_bot, jnp.concatenate([_bd_stack(u, m0_1), _bd_stack(v, m0_1)], axis=1))
    upd = _bdot(jnp.concatenate([u, v], axis=1), jnp.concatenate([bh, kh], axis=1), _BMM_TN)
    s_new = s_bd * jnp.exp(total) + jnp.where(bd_mask, upd, jnp.zeros_like(upd))
    return y, s_new


def _scan_consts(C, n_fwd, n_bwd):
    G = n_fwd + n_bwd
    is_rev = lax.broadcasted_iota(jnp.int32, (G, 1, 1), 0) >= n_fwd
    sign = jnp.where(is_rev, -1, 1)
    row2 = lax.broadcasted_iota(jnp.int32, (G, C, 2 * LANES), 1)
    col2 = lax.broadcasted_iota(jnp.int32, (G, C, 2 * LANES), 2) % HEAD_DIM
    strict2 = (col2 - row2) * sign < 0
    incl2 = (col2 - row2) * sign <= 0
    lane1 = lax.broadcasted_iota(jnp.int32, (1, C, LANES), 2)
    lane2 = lax.broadcasted_iota(jnp.int32, (1, C, 2 * LANES), 2) % LANES
    m0_1 = lane1 < HEAD_DIM
    m0_2 = lane2 < HEAD_DIM
    rb = lax.broadcasted_iota(jnp.int32, (1, LANES, LANES), 1) // HEAD_DIM
    cb = lax.broadcasted_iota(jnp.int32, (1, LANES, LANES), 2) // HEAD_DIM
    bd_mask = rb == cb
    row1 = lax.broadcasted_iota(jnp.int32, (1, C, LANES), 1)
    col1 = lane1 % HEAD_DIM
    eye2 = (row1 == col1).astype(F32)
    off_masks = [row1 // 2 == col1 // 2]
    b = 2
    while b < C:
        off_masks.append((row1 // (2 * b) == col1 // (2 * b)) & (row1 // b != col1 // b))
        b *= 2
    return (is_rev, strict2, incl2, eye2, m0_1, m0_2, bd_mask, off_masks)


def _rwkv_scan_kernel(rf, vf, kkf, lwf, kaf, kdf, rb, vb, kkb, lwb, kab, kdb,
                      yf_o, yb_o, state_ref):
    c = pl.program_id(0)

    @pl.when(c == 0)
    def _():
        state_ref[...] = jnp.zeros_like(state_ref)

    nb, C = rf.shape[0], rf.shape[1]
    n_pairs = rf.shape[2] // LANES
    n_dir = nb * n_pairs
    consts = _scan_consts(C, n_dir, n_dir)

    row = lax.broadcasted_iota(jnp.int32, (C, C), 0)
    col = lax.broadcasted_iota(jnp.int32, (C, C), 1)
    tri_f = (col <= row).astype(F32).astype(BF16)
    tri_b = (col >= row).astype(F32).astype(BF16)
    cum_f = [_dot_exact_rhs(tri_f, lwf[b]) for b in range(nb)]
    cum_b = [_dot_exact_rhs(tri_b, lwb[b]) for b in range(nb)]

    def chains(xs_f, xs_b):
        return jnp.stack([x[:, p * LANES:(p + 1) * LANES]
                          for xs in (xs_f, xs_b) for x in xs for p in range(n_pairs)], axis=0)

    def rows(ref):
        return [ref[b].astype(F32) for b in range(nb)]

    r, v, kk, lw, ka, kd = (chains(rows(f), rows(b)) for f, b in
                            ((rf, rb), (vf, vb), (kkf, kkb), (lwf, lwb), (kaf, kab), (kdf, kdb)))
    y, s_new = _scan_chunk(r, v, kk, lw, chains(cum_f, cum_b), ka, kd, state_ref[...], consts)
    state_ref[...] = s_new
    for d, y_o in enumerate((yf_o, yb_o)):
        for b in range(nb):
            for p in range(n_pairs):
                y_o[b, :, p * LANES:(p + 1) * LANES] = y[(d * nb + b) * n_pairs + p]


def rwkv_scan(r, v, kk, lwf, lwb, kaf, kab, kdf, kdb):
    B, S, D = r.shape
    C = CHUNK
    nc = S // C
    fwd = pl.BlockSpec((B, C, D), lambda c: (0, c, 0))
    bwd = pl.BlockSpec((B, C, D), lambda c: (0, nc - 1 - c, 0))
    return pl.pallas_call(
        _rwkv_scan_kernel,
        grid=(nc,),
        in_specs=[fwd] * 6 + [bwd] * 6,
        out_specs=[fwd, bwd],
        out_shape=[jax.ShapeDtypeStruct((B, S, D), F32)] * 2,
        scratch_shapes=[pltpu.VMEM((2 * B * (D // LANES), LANES, LANES), F32)],
        compiler_params=_cparams("arbitrary"),
        name="rwkv_scan",
    )(r, v, kk, lwf, kaf, kdf, r, v, kk, lwb, kab, kdb)


def _attn_kernel(table_ref, bucket_ref, q_ref, kp_ref, km_ref, kn_ref, vp_ref, vm_ref, vn_ref,
                 o_ref, lse_ref, bias_ref, *, nblk):
    i = pl.program_id(2)
    n_sub = q_ref.shape[0] // ATTN_SUB
    TQ = ATTN_SUB
    TK = TQ + 2 * ATTN_HALF

    @pl.when((pl.program_id(0) == 0) & (pl.program_id(1) == 0) & (i == 0))
    def _():
        bucket = bucket_ref[...]
        row = lax.broadcasted_iota(jnp.int32, (TQ, TK), 0)
        colk = lax.broadcasted_iota(jnp.int32, (TQ, TK), 1)
        band = jnp.abs(colk - ATTN_HALF - row) <= ATTN_HALF
        has_prev = colk >= ATTN_HALF
        has_next = colk < TQ + ATTN_HALF
        for h in range(ATTN_SLOTS):
            acc = jnp.zeros((TQ, TK), F32)
            for b in range(N_BUCKETS):
                acc = jnp.where(bucket == b, table_ref[h * N_BUCKETS + b], acc)
            acc = jnp.where(band, acc, NEG_BIG)
            bias_ref[0, h] = acc
            bias_ref[1, h] = jnp.where(has_prev, acc, NEG_BIG)
            bias_ref[2, h] = jnp.where(has_next, acc, NEG_BIG)
            bias_ref[3, h] = jnp.where(has_prev & has_next, acc, NEG_BIG)

    m0 = lax.broadcasted_iota(jnp.int32, (TQ, LANES), 1) < HEAD_DIM
    lane = lax.broadcasted_iota(jnp.int32, (1, LANES), 1)
    head_keep = [(lane < HEAD_DIM).astype(F32).astype(BF16), (lane >= HEAD_DIM).astype(F32).astype(BF16)]
    for p in range(ATTN_OUT_DIM // LANES):
        sl = slice(p * LANES, (p + 1) * LANES)
        kfull = jnp.concatenate([kp_ref[:, sl], km_ref[:, sl], kn_ref[:, sl]], axis=0)
        vfull = jnp.concatenate([vp_ref[:, sl], vm_ref[:, sl], vn_ref[:, sl]], axis=0)
        for sb in range(n_sub):
            rows = slice(sb * TQ, (sb + 1) * TQ)
            first = (i == 0) if sb == 0 else False
            last = (i == nblk - 1) if sb == n_sub - 1 else False
            variant = jnp.where(first, 1, 0) + jnp.where(last, 2, 0)
            q2 = q_ref[rows, sl]
            kwin = kfull[sb * TQ:sb * TQ + TK]
            vwin = vfull[sb * TQ:sb * TQ + TK]
            outs, lses = [], []
            for hh in range(2):
                qm = q2 * head_keep[hh]
                s = lax.dot_general(qm, kwin, (((1,), (1,)), ((), ())), preferred_element_type=F32)
                s = s + bias_ref[variant, 2 * p + hh]
                m = jnp.max(s, axis=-1, keepdims=True)
                e = jnp.exp(s - m)
                den = jnp.sum(e, axis=-1, keepdims=True)
                pv = jnp.dot(e.astype(BF16), vwin, preferred_element_type=F32)
                outs.append(pv / den)
                lses.append(jnp.broadcast_to(m + jnp.log(den), (TQ, LANES)))
            o_ref[rows, sl] = jnp.where(m0, outs[0], outs[1]).astype(o_ref.dtype)
            lse_ref[rows, sl] = jnp.where(m0, lses[0], lses[1])


def _t5_bucket(rel):
    nb = N_BUCKETS // 2
    max_exact = nb // 2
    ret = jnp.where(rel > 0, nb, 0)
    n = jnp.abs(rel)
    nf = jnp.maximum(n, 1).astype(jnp.float32)
    large = max_exact + (jnp.log(nf / max_exact) / math.log(MAX_DISTANCE / max_exact)
                         * (nb - max_exact)).astype(jnp.int32)
    large = jnp.minimum(large, nb - 1)
    return ret + jnp.where(n < max_exact, n, large)


def attention_bias_inputs(table, gi, dilation):
    tq = ATTN_SUB
    tk = tq + 2 * ATTN_HALF
    rel = jnp.arange(tk)[None, :] - ATTN_HALF - jnp.arange(tq)[:, None]
    bucket = _t5_bucket(rel * dilation).astype(jnp.int32)
    tbl = table[:, gi * ATTN_SLOTS:(gi + 1) * ATTN_SLOTS].astype(F32).T.reshape(-1)
    return tbl, bucket


def dilated_attention(qkv, tbl, bucket, gi, *, tq):
    B, dilation, L, _ = qkv.shape
    nblk = L // tq
    hb = tq // ATTN_HALF
    nhalf = L // ATTN_HALF
    W = ATTN_OUT_DIM

    def main(off):
        return pl.BlockSpec((None, None, tq, W), lambda b, r, i: (b, r, i, off))

    def prev(off):
        return pl.BlockSpec((None, None, ATTN_HALF, W),
                            lambda b, r, i: (b, r, jnp.maximum(i * hb - 1, 0), off))

    def nxt(off):
        return pl.BlockSpec((None, None, ATTN_HALF, W),
                            lambda b, r, i: (b, r, jnp.minimum((i + 1) * hb, nhalf - 1), off))

    sub_shape = (ATTN_SUB, ATTN_SUB + 2 * ATTN_HALF)
    out_spec = pl.BlockSpec((None, None, tq, W), lambda b, r, i: (b, r, i, 0))
    return pl.pallas_call(
        functools.partial(_attn_kernel, nblk=nblk),
        grid=(B, dilation, nblk),
        in_specs=[pl.BlockSpec(memory_space=pltpu.SMEM),
                  pl.BlockSpec(sub_shape, lambda b, r, i: (0, 0)),
                  main(0), prev(1), main(1), nxt(1), prev(2), main(2), nxt(2)],
        out_specs=[out_spec, out_spec],
        out_shape=[jax.ShapeDtypeStruct((B, dilation, L, W), BF16),
                   jax.ShapeDtypeStruct((B, dilation, L, W), F32)],
        scratch_shapes=[pltpu.VMEM((4, ATTN_SLOTS) + sub_shape, F32)],
        compiler_params=_cparams("arbitrary", "arbitrary", "arbitrary"),
        name=f"dilated_attn_g{gi}",
    )(tbl, bucket, qkv, qkv, qkv, qkv, qkv, qkv, qkv)


def _branch_post_kernel(yf_ref, yb_ref, bonus_ref, g_ref, gnw_ref, gnb_ref,
                        o0, o1, o2, l0, l1, l2, orw_ref, oat_ref, *scratch):
    y = yf_ref[...] + yb_ref[...]
    ones_bd = _head_block_ones(2 * LANES)
    tm, width = y.shape

    def head_mean(x):
        return jnp.concatenate(
            [_head_sum(x[:, c:c + 2 * LANES], ones_bd) for c in range(0, width, 2 * LANES)],
            axis=1) * (1.0 / HEAD_DIM)

    mean = head_mean(y)
    yc = y - mean
    var = head_mean(yc * yc)
    yn = yc * lax.rsqrt(var + GN_EPS) * gnw_ref[...] + gnb_ref[...]
    orw_ref[...] = ((yn + bonus_ref[...]) * g_ref[...]).astype(BF16)

    def natural_order(ref, scr):
        d = ref.shape[0]
        for t in range(ATTN_OUT_DIM // LANES):
            for res in range(d):
                scr[t, pl.ds(res, tm // d, stride=d), :] = (
                    ref[res, :, t * LANES:(t + 1) * LANES].astype(F32))
        return jnp.concatenate([scr[t] for t in range(ATTN_OUT_DIM // LANES)], axis=1)

    s_o1, s_o2, s_l1, s_l2 = scratch
    oa, ob, oc = o0[0].astype(F32), natural_order(o1, s_o1), natural_order(o2, s_o2)
    la, lb, lc = l0[0], natural_order(l1, s_l1), natural_order(l2, s_l2)
    m = jnp.maximum(jnp.maximum(la, lb), lc)
    ea, eb, ec = jnp.exp(la - m), jnp.exp(lb - m), jnp.exp(lc - m)
    den = ea + eb + ec
    oat_ref[...] = ((ea * oa + eb * ob + ec * oc) / den).astype(BF16)


def branch_post(yf, yb, bonus, g, gn_w, gn_b, layer, outs, lses, *, tm):
    M, D = yf.shape
    A = ATTN_OUT_DIM
    batch = outs[0].shape[0]
    nb = M // batch // tm
    big = pl.BlockSpec((tm, D), lambda b, i: (b * nb + i, 0))
    rowp = pl.BlockSpec((None, 1, D), lambda b, i: (layer, 0, 0))
    attn_specs = [pl.BlockSpec((None, o.shape[1], tm // o.shape[1], A), lambda b, i: (b, 0, i, 0))
                  for o in outs]
    return pl.pallas_call(
        _branch_post_kernel,
        grid=(batch, nb),
        in_specs=[big] * 4 + [rowp, rowp] + attn_specs + attn_specs,
        out_specs=[big, pl.BlockSpec((tm, A), lambda b, i: (b * nb + i, 0))],
        out_shape=[jax.ShapeDtypeStruct((M, D), BF16), jax.ShapeDtypeStruct((M, A), BF16)],
        scratch_shapes=[pltpu.VMEM((A // LANES, tm, LANES), F32)] * 4,
        compiler_params=_cparams("parallel", "arbitrary"),
        name="branch_post",
    )(yf, yb, bonus, g, gn_w.reshape(-1, 1, D), gn_b.reshape(-1, 1, D), *outs, *lses)


def _merge_kernel(orw_ref, oat_ref, wr_ref, wa_ref, gr_ref, ga_ref, o_ref):
    a = jnp.dot(orw_ref[...], wr_ref[...].astype(BF16), preferred_element_type=F32)
    b = jnp.dot(oat_ref[...], wa_ref[...].astype(BF16), preferred_element_type=F32)
    o_ref[...] = (gr_ref[...].astype(F32) * a + ga_ref[...].astype(F32) * b).astype(o_ref.dtype)


def branch_merge(orw, oat, w_r, w_a, layer, gates, *, tm, tn):
    M = orw.shape[0]
    N = w_r.shape[2]
    nj = N // tn
    return pl.pallas_call(
        _merge_kernel,
        grid=(M // tm, nj),
        in_specs=[pl.BlockSpec((tm, orw.shape[1]), lambda i, j: (i, 0)),
                  pl.BlockSpec((tm, oat.shape[1]), lambda i, j: (i, 0)),
                  pl.BlockSpec((None, w_r.shape[1], tn), lambda i, j: (layer, 0, j)),
                  pl.BlockSpec((None, w_a.shape[1], tn), lambda i, j: (layer, 0, j)),
                  pl.BlockSpec((tm, tn), lambda i, j: (i, j)),
                  pl.BlockSpec((tm, tn), lambda i, j: (i, j + nj))],
        out_specs=pl.BlockSpec((tm, tn), lambda i, j: (i, j)),
        out_shape=jax.ShapeDtypeStruct((M, N), BF16),
        compiler_params=_cparams("parallel", "arbitrary"),
        name="branch_merge",
    )(orw, oat, w_r, w_a, gates, gates)


def _lora_weights(w_up, a_up):
    z = jnp.zeros_like(w_up[:, 0])
    rows = [jnp.concatenate([w_up[:, 0], z, z, z], axis=2),
            jnp.concatenate([z, w_up[:, 1], z, z], axis=2),
            jnp.concatenate([z, z, a_up[:, 0], z], axis=2),
            jnp.concatenate([z, z, z, a_up[:, 1]], axis=2)]
    return jnp.concatenate(rows, axis=1).astype(BF16)


def kernel(x, norm1_g, w_in, tshift_mu, w0, w_lora_up, a0, a_lora_up, g_lora_up, k_k, k_a, r_k,
           gn_w, gn_b, rel_bias, w_branch_rwkv, w_branch_attn, w_out, norm2_g, w_mlp_in, w_mlp_out,
           final_g):
    B, S, D = x.shape
    M = B * S
    depth = w_in.shape[0]
    c_rkv = 3 * RWKV_DIM
    c_slab = c_rkv + LORA_COLS
    c_attn = c_slab + 3 * ATTN_DIM
    w_qkv = w_in[:, :, c_slab:c_attn].astype(BF16)
    w_lora_up = _lora_weights(w_lora_up, a_lora_up)
    g_up = g_lora_up.astype(BF16)
    bias_inputs = [attention_bias_inputs(rel_bias, gi, dilation)
                   for gi, (_, dilation) in enumerate(ATTN_GROUPS)]
    h = x.reshape(M, D)
    for l in range(depth):
        g1 = norm1_g
        rkv = norm_matmul(h, g1, w_in, l, n_cols=c_rkv, tm=1024, tn=1024)
        lora = norm_matmul(h, g1, w_in, l, col0=c_rkv, n_cols=LORA_COLS, tm=1024, tn=LORA_COLS,
                           pieces=LORA_COLS // LANES)
        qkv_groups = qkv_proj(h, g1, w_qkv, l, batch=B, tm=1024)
        gates = norm_matmul(h, g1, w_in, l, col0=c_attn, n_cols=2 * D, tm=1024, tn=1024,
                            pieces=1024 // LANES, act="sigmoid", out_dtype=BF16)

        (r, v, kk, lwf, lwb, kaf, kab, kdf, kdb, g, bonus) = rwkv_prep(
            rkv.reshape(B, S, c_rkv), lora.reshape(B, S, LORA_COLS), l, tshift_mu, w0, a0,
            w_lora_up, g_up, k_k, k_a, r_k, ts=256)
        yf, yb = rwkv_scan(r, v, kk, lwf, lwb, kaf, kab, kdf, kdb)

        outs, lses = [], []
        for gi, (window, dilation) in enumerate(ATTN_GROUPS):
            assert window // (2 * dilation) == ATTN_HALF
            o, lse = dilated_attention(qkv_groups[gi], *bias_inputs[gi], gi, tq=256)
            outs.append(o)
            lses.append(lse)

        orw, oat = branch_post(yf.reshape(M, RWKV_DIM), yb.reshape(M, RWKV_DIM),
                               bonus.reshape(M, RWKV_DIM), g.reshape(M, RWKV_DIM),
                               gn_w, gn_b, l, outs, lses, tm=512)
        merged = branch_merge(orw, oat, w_branch_rwkv, w_branch_attn, l, gates, tm=1024, tn=1024)
        h = matmul_residual(merged, w_out, l, h, tm=1024, tn=1024, tk=D)

        act = norm_matmul(h, norm2_g, w_mlp_in, l, n_cols=w_mlp_in.shape[2], tm=1024, tn=1024,
                          act="relu2", out_dtype=BF16)
        h = matmul_residual(act, w_mlp_out, l, h, tm=1024, tn=1024, tk=2048)
    out = rmsnorm_rows(h, final_g, tm=512)
    return out.reshape(B, S, D)
```

```python
import functools
import math

import jax
import jax.numpy as jnp
from jax import lax
from jax.experimental import pallas as pl
from jax.experimental.pallas import tpu as pltpu

F32 = jnp.float32
BF16 = jnp.bfloat16

HEAD_DIM = 64
LANES = 128
RWKV_DIM = 1024
DECAY_LORA = 96
AAA_LORA = 96
GATE_LORA = 256
LORA_COLS = GATE_LORA + 2 * DECAY_LORA + 2 * AAA_LORA
LORA_TAIL = LORA_COLS - GATE_LORA
ATTN_GROUPS = ((128, 1), (512, 4), (2048, 16))
ATTN_SLOTS = 8
ATTN_DIM = 1536
ATTN_OUT_DIM = ATTN_SLOTS * HEAD_DIM
ATTN_HALF = 64
ATTN_SUB = 128
ATTN_MAX_BLOCK = 1024
N_BUCKETS = 32
MAX_DISTANCE = 1024
RMS_EPS = 1e-6
GN_EPS = 64e-5
L2_EPS = 1e-12
NEG_BIG = -1e30
CHUNK = 64
VMEM_LIMIT = 56 * 1024 * 1024


def _cparams(*sem):
    return pltpu.CompilerParams(dimension_semantics=sem, vmem_limit_bytes=VMEM_LIMIT)


def _dot(a, b):
    return jnp.dot(a.astype(BF16), b.astype(BF16), preferred_element_type=F32)


def _split3(x):
    p0 = x.astype(BF16)
    r1 = x - p0.astype(F32)
    p1 = r1.astype(BF16)
    p2 = (r1 - p1.astype(F32)).astype(BF16)
    return p0, p1, p2


def _dot_exact_rhs(a_bf16, x):
    return sum(jnp.dot(a_bf16, p, preferred_element_type=F32) for p in _split3(x))


def _head_sum(x, ones_bd):
    hi = x.astype(BF16)
    lo = (x - hi.astype(F32)).astype(BF16)
    return (jnp.dot(hi, ones_bd, preferred_element_type=F32)
            + jnp.dot(lo, ones_bd, preferred_element_type=F32))


def _head_block_ones(width):
    r = lax.broadcasted_iota(jnp.int32, (width, width), 0) // HEAD_DIM
    c = lax.broadcasted_iota(jnp.int32, (width, width), 1) // HEAD_DIM
    return (r == c).astype(BF16)


def _normalise_rows(x_ref, g_ref, u_ref, start, size):
    rows = pl.ds(start, size)
    x = x_ref[rows, :]
    ms = jnp.mean(x * x, axis=-1, keepdims=True)
    u_ref[rows, :] = (x * lax.rsqrt(ms + RMS_EPS) * g_ref[...]).astype(BF16)


def _x_rows_index(i, j, ni, nj):
    if nj == 1:
        return i
    return jnp.minimum(i + jnp.where(j >= 1, 1, 0), ni - 1)


BF16_ROWS = 16


def _with_normalised_tile(x_ref, g_ref, u_refs, nj, body):
    i, j = pl.program_id(0), pl.program_id(1)
    tm = x_ref.shape[0]
    if nj == 1:
        _normalise_rows(x_ref, g_ref, u_refs[0], 0, tm)
        body(u_refs[0])
        return

    @pl.when((i == 0) & (j == 0))
    def _():
        _normalise_rows(x_ref, g_ref, u_refs[0], 0, tm)

    size = -(-tm // (nj - 1))
    size = -(-size // BF16_ROWS) * BF16_ROWS
    start = pl.multiple_of(jnp.clip((j - 1) * size, 0, tm - size), BF16_ROWS)
    for parity in range(2):
        @pl.when(i % 2 == parity)
        def _():
            body(u_refs[parity])
            _normalise_rows(x_ref, g_ref, u_refs[1 - parity], start, size)


def _norm_mm_kernel(x_ref, g_ref, *refs, act, nj):
    *w_refs, o_ref, u0_ref, u1_ref = refs

    def body(u_ref):
        w = jnp.concatenate([w_ref[...].astype(BF16) for w_ref in w_refs], axis=1)
        acc = jnp.dot(u_ref[...], w, preferred_element_type=F32)
        if act == "relu2":
            acc = jnp.square(jnp.maximum(acc, 0.0))
        elif act == "sigmoid":
            acc = 0.5 * jnp.tanh(0.5 * acc) + 0.5
        o_ref[...] = acc.astype(o_ref.dtype)

    _with_normalised_tile(x_ref, g_ref, (u0_ref, u1_ref), nj, body)


def norm_matmul(x, g, w, layer, *, col0=0, n_cols, tm, tn, pieces=1, act=None, out_dtype=F32):
    M, K = x.shape
    pw = tn // pieces
    assert pw * pieces == tn and col0 % pw == 0 and n_cols % tn == 0
    w_specs = [pl.BlockSpec((None, K, pw), functools.partial(
        lambda i, j, t: (layer, 0, col0 // pw + j * pieces + t), t=t)) for t in range(pieces)]
    ni, nj = M // tm, n_cols // tn
    return pl.pallas_call(
        functools.partial(_norm_mm_kernel, act=act, nj=nj),
        grid=(ni, nj),
        in_specs=[pl.BlockSpec((tm, K), lambda i, j: (_x_rows_index(i, j, ni, nj), 0)),
                  pl.BlockSpec((None, 1, K), lambda i, j: (layer, 0, 0))] + w_specs,
        out_specs=pl.BlockSpec((tm, tn), lambda i, j: (i, j)),
        out_shape=jax.ShapeDtypeStruct((M, n_cols), out_dtype),
        scratch_shapes=[pltpu.VMEM((tm, K), BF16)] * 2,
        compiler_params=_cparams("arbitrary", "arbitrary"),
        name="norm_matmul_" + (act or "id"),
    )(x, g.reshape(-1, 1, K), *([w] * pieces))


def _qkv_proj_kernel(x_ref, g_ref, w_ref, o0_ref, o1_ref, o2_ref, u_ref, acc_ref):
    tm = x_ref.shape[0]

    @pl.when(pl.program_id(1) == 0)
    def _():
        _normalise_rows(x_ref, g_ref, u_ref, 0, tm)

    acc = jnp.dot(u_ref[...], w_ref[...], preferred_element_type=F32)
    acc = acc * jnp.where(pl.program_id(1) == 0, HEAD_DIM ** -0.5, 1.0)
    tiles_per_group = ATTN_OUT_DIM // LANES
    for gi, (o_ref, (_, d)) in enumerate(zip((o0_ref, o1_ref, o2_ref), ATTN_GROUPS)):
        if d == 1:
            o_ref[0, 0] = acc[:, gi * ATTN_OUT_DIM:(gi + 1) * ATTN_OUT_DIM].astype(BF16)
            continue
        for t in range(tiles_per_group):
            c = gi * tiles_per_group + t
            acc_ref[c] = acc[:, c * LANES:(c + 1) * LANES]
            for res in range(d):
                o_ref[0, res, :, t * LANES:(t + 1) * LANES] = (
                    acc_ref[c, pl.ds(res, tm // d, stride=d), :].astype(BF16))


def qkv_proj(x, g, w, layer, *, batch, tm):
    M, K = x.shape
    S = M // batch
    nb = S // tm
    outs_shape, outs_spec = [], []
    for _, d in ATTN_GROUPS:
        outs_shape.append(jax.ShapeDtypeStruct((batch, d, S // d, 3 * ATTN_OUT_DIM), BF16))
        outs_spec.append(pl.BlockSpec((1, d, tm // d, ATTN_OUT_DIM),
                                      lambda i, j: (i // nb, 0, i % nb, j)))
    return pl.pallas_call(
        _qkv_proj_kernel,
        grid=(M // tm, 3),
        in_specs=[pl.BlockSpec((tm, K), lambda i, j: (i, 0)),
                  pl.BlockSpec((None, 1, K), lambda i, j: (layer, 0, 0)),
                  pl.BlockSpec((None, K, ATTN_DIM), lambda i, j: (layer, 0, j))],
        out_specs=outs_spec,
        out_shape=outs_shape,
        scratch_shapes=[pltpu.VMEM((tm, K), BF16), pltpu.VMEM((ATTN_DIM // LANES, tm, LANES), F32)],
        compiler_params=_cparams("parallel", "arbitrary"),
        name="qkv_proj",
    )(x, g.reshape(-1, 1, K), w)


def _mm_res_kernel(x_ref, w_ref, r_ref, o_ref, acc_ref):
    k = pl.program_id(2)

    @pl.when(k == 0)
    def _():
        acc_ref[...] = jnp.zeros_like(acc_ref)

    acc_ref[...] += jnp.dot(x_ref[...], w_ref[...].astype(BF16), preferred_element_type=F32)

    @pl.when(k == pl.num_programs(2) - 1)
    def _():
        o_ref[...] = r_ref[...] + acc_ref[...]


def matmul_residual(x, w, layer, res, *, tm, tn, tk):
    M, K = x.shape
    N = w.shape[2]
    return pl.pallas_call(
        _mm_res_kernel,
        grid=(M // tm, N // tn, K // tk),
        in_specs=[pl.BlockSpec((tm, tk), lambda i, j, k: (i, k)),
                  pl.BlockSpec((None, tk, tn), lambda i, j, k: (layer, k, j)),
                  pl.BlockSpec((tm, tn), lambda i, j, k: (i, j))],
        out_specs=pl.BlockSpec((tm, tn), lambda i, j, k: (i, j)),
        out_shape=jax.ShapeDtypeStruct((M, N), F32),
        scratch_shapes=[pltpu.VMEM((tm, tn), F32)],
        compiler_params=_cparams("parallel", "parallel", "arbitrary"),
        name="matmul_residual",
    )(x, w, res)


def _rmsnorm_kernel(x_ref, g_ref, o_ref):
    x = x_ref[...]
    ms = jnp.mean(x * x, axis=-1, keepdims=True)
    o_ref[...] = x * lax.rsqrt(ms + RMS_EPS) * g_ref[...]


def rmsnorm_rows(x, g, *, tm):
    M, K = x.shape
    return pl.pallas_call(
        _rmsnorm_kernel,
        grid=(M // tm,),
        in_specs=[pl.BlockSpec((tm, K), lambda i: (i, 0)),
                  pl.BlockSpec((1, K), lambda i: (0, 0))],
        out_specs=pl.BlockSpec((tm, K), lambda i: (i, 0)),
        out_shape=jax.ShapeDtypeStruct((M, K), F32),
        compiler_params=_cparams("parallel"),
        name="final_rmsnorm",
    )(x, g.reshape(1, K))


def _token_shift(main, prev_blk, next_blk, mu, first, last):
    rows = main.shape[0]
    prev_row = jnp.where(first, 0.0, prev_blk[7:8, :])
    next_row = jnp.where(last, 0.0, next_blk[0:1, :])
    rid = lax.broadcasted_iota(jnp.int32, main.shape, 0)
    up = jnp.where(rid == 0, prev_row, pltpu.roll(main, 1, 0))
    dn = jnp.where(rid == rows - 1, next_row, pltpu.roll(main, rows - 1, 0))
    return main * (1.0 - mu) + (0.5 * mu) * (up + dn)


def _rwkv_prep_kernel(r_m, r_p, r_n, k_m, k_p, k_n, v_m, v_p, v_n, l_m, l_p, l_n,
                      mu_ref, w0_ref, a0_ref, lora_w_ref, g_up_ref, kk_ref, ka_ref, rk_ref,
                      r_o, v_o, kk_o, lwf_o, lwb_o, kaf_o, kab_o, kdf_o, kdb_o, g_o, bonus_o):
    i = pl.program_id(1)
    first = i == 0
    last = i == pl.num_programs(1) - 1
    D = RWKV_DIM
    r = _token_shift(r_m[0], r_p[0], r_n[0], mu_ref[:, 0:D], first, last)
    k = _token_shift(k_m[0], k_p[0], k_n[0], mu_ref[:, D:2 * D], first, last)
    v = _token_shift(v_m[0], v_p[0], v_n[0], mu_ref[:, 2 * D:3 * D], first, last)
    lo = _token_shift(l_m[0], l_p[0], l_n[0], mu_ref[:, 3 * D:], first, last)

    g = _dot(jax.nn.sigmoid(lo[:, :GATE_LORA]), g_up_ref[...])
    tail = lo[:, GATE_LORA:]
    col = lax.broadcasted_iota(jnp.int32, tail.shape, 1)
    tail = jnp.where(col < 2 * DECAY_LORA, jnp.tanh(tail), tail)
    up = _dot(tail, lora_w_ref[...])

    ones_bd = _head_block_ones(2 * LANES)

    def head_sum(x):
        return jnp.concatenate(
            [_head_sum(x[:, c:c + 2 * LANES], ones_bd) for c in range(0, RWKV_DIM, 2 * LANES)], axis=1)

    kk = k * kk_ref[...]
    kk = kk * jnp.minimum(lax.rsqrt(head_sum(kk * kk)), 1.0 / L2_EPS)

    r_o[0] = r.astype(r_o.dtype)
    v_o[0] = v.astype(v_o.dtype)
    kk_o[0] = kk.astype(kk_o.dtype)
    g_o[0] = g
    kd_sum = jnp.zeros_like(k)
    for d, (lw_o, ka_o, kd_o) in enumerate(((lwf_o, kaf_o, kdf_o), (lwb_o, kab_o, kdb_o))):
        z = w0_ref[d:d + 1, :] + up[:, d * RWKV_DIM:(d + 1) * RWKV_DIM]
        nz = -z
        softplus = jnp.maximum(nz, 0.0) + jnp.log(1.0 + jnp.exp(-jnp.abs(nz)))
        lw_o[0] = -jnp.exp(-softplus - 0.5)
        a = jax.nn.sigmoid(a0_ref[d:d + 1, :] + up[:, (2 + d) * RWKV_DIM:(3 + d) * RWKV_DIM])
        kd = k * (1.0 + (a - 1.0) * ka_ref[...])
        ka_o[0] = (kk * a).astype(ka_o.dtype)
        kd_o[0] = kd.astype(kd_o.dtype)
        kd_sum = kd_sum + kd
    bonus_o[0] = head_sum(r * kd_sum * rk_ref[...]) * v


def rwkv_prep(rkv, lora, layer, mu, w0, a0, lora_w, g_up, k_k, k_a, r_k, *, ts):
    B, S, _ = rkv.shape
    D = RWKV_DIM
    nblk8 = S // 8

    def main_spec(width, cblk):
        return pl.BlockSpec((1, ts, width), lambda b, i: (b, i, cblk))

    def prev_spec(width, cblk):
        return pl.BlockSpec((1, 8, width), lambda b, i: (b, jnp.maximum(i * (ts // 8) - 1, 0), cblk))

    def next_spec(width, cblk):
        return pl.BlockSpec((1, 8, width), lambda b, i: (b, jnp.minimum((i + 1) * (ts // 8), nblk8 - 1), cblk))

    def layer_spec(rows, width):
        return pl.BlockSpec((None, rows, width), lambda b, i: (layer, 0, 0))

    in_specs = []
    args = []
    for cblk in range(3):
        in_specs += [main_spec(D, cblk), prev_spec(D, cblk), next_spec(D, cblk)]
        args += [rkv, rkv, rkv]
    in_specs += [main_spec(LORA_COLS, 0), prev_spec(LORA_COLS, 0), next_spec(LORA_COLS, 0)]
    args += [lora, lora, lora]
    depth = mu.shape[0]
    in_specs += [layer_spec(1, 3 * D + LORA_COLS), layer_spec(2, D), layer_spec(2, D),
                 layer_spec(LORA_TAIL, 4 * D), layer_spec(GATE_LORA, D),
                 layer_spec(1, D), layer_spec(1, D), layer_spec(1, D)]
    args += [mu.reshape(depth, 1, -1), w0, a0, lora_w, g_up,
             k_k.reshape(depth, 1, D), k_a.reshape(depth, 1, D), r_k.reshape(depth, 1, D)]
    out_spec = pl.BlockSpec((1, ts, D), lambda b, i: (b, i, 0))
    out_dtypes = [BF16, BF16, BF16, F32, F32, BF16, BF16, BF16, BF16, F32, F32]
    n_out = len(out_dtypes)
    return pl.pallas_call(
        _rwkv_prep_kernel,
        grid=(B, S // ts),
        in_specs=in_specs,
        out_specs=[out_spec] * n_out,
        out_shape=[jax.ShapeDtypeStruct((B, S, D), dt) for dt in out_dtypes],
        compiler_params=_cparams("parallel", "arbitrary"),
        name="rwkv_prep",
    )(*args)


_BMM = (((2,), (1,)), ((0,), (0,)))
_BMM_NT = (((2,), (2,)), ((0,), (0,)))
_BMM_TN = (((1,), (1,)), ((0,), (0,)))


def _bdot(a, b, dims=_BMM):
    return lax.dot_general(a.astype(BF16), b.astype(BF16), dims, preferred_element_type=F32)


def _bd_stack(x, m0):
    zero = jnp.zeros_like(x)
    return jnp.concatenate([jnp.where(m0, x, zero), jnp.where(m0, zero, x)], axis=-2)


def _scan_chunk(r, v, kk, lw, cum, ka, kd, s_bd, consts):
    is_rev, strict2, incl2, eye2, m0_1, m0_2, bd_mask, off_masks = consts
    C = r.shape[1]
    cum_prev = cum - lw
    e_cum = jnp.exp(cum)
    e_prev = jnp.exp(cum_prev)
    e_neg = jnp.exp(-cum)
    total = jnp.where(is_rev, cum[:, 0:1, :], cum[:, C - 1:C, :])
    e_rest = jnp.exp(total - cum)
    r0 = r * e_cum
    at0 = -kk * e_prev
    bt = ka * e_neg
    kt = kd * e_neg
    bh = ka * e_rest
    kh = kd * e_rest

    gram = _bdot(jnp.concatenate([at0, r0], axis=1),
                 jnp.concatenate([_bd_stack(bt, m0_1), _bd_stack(kt, m0_1)], axis=1), _BMM_NT)
    zero = jnp.zeros_like(gram[:, :C])
    a_top = jnp.where(strict2, gram[:, :C], zero)
    a_bot = jnp.where(incl2, gram[:, C:], zero)
    a_ab = a_top[:, :, :LANES]
    a_ak = a_top[:, :, LANES:]

    zero1 = jnp.zeros_like(a_ab)
    t = eye2 + jnp.where(off_masks[0], a_ab, zero1)
    for off_mask in off_masks[1:]:
        z = _bdot(t, _bd_stack(jnp.where(off_mask, a_ab, zero1), m0_1))
        t = t + _bdot(z, _bd_stack(t, m0_1))

    av = _bdot(a_ak, _bd_stack(v, m0_1))
    wu = _bdot(t, _bd_stack(jnp.concatenate([at0, av], axis=2), m0_2))
    w_a = wu[:, :, :LANES]
    u_v = wu[:, :, LANES:]

    wr = _bdot(jnp.concatenate([w_a, r0], axis=1), s_bd, _BMM_NT)
    u = wr[:, :C] + u_v
    y = wr[:, C:] + _bdot(a_bot, jnp.concatenate([_bd_stack(u, m0_1), _bd_stack(v, m0_1)], axis=1))
    upd = _bdot(jnp.concatenate([u, v], axis=1), jnp.concatenate([bh, kh], axis=1), _BMM_TN)
    s_new = s_bd * jnp.exp(total) + jnp.where(bd_mask, upd, jnp.zeros_like(upd))
    return y, s_new


def _scan_consts(C, n_fwd, n_bwd):
    G = n_fwd + n_bwd
    is_rev = lax.broadcasted_iota(jnp.int32, (G, 1, 1), 0) >= n_fwd
    sign = jnp.where(is_rev, -1, 1)
    row2 = lax.broadcasted_iota(jnp.int32, (G, C, 2 * LANES), 1)
    col2 = lax.broadcasted_iota(jnp.int32, (G, C, 2 * LANES), 2) % HEAD_DIM
    strict2 = (col2 - row2) * sign < 0
    incl2 = (col2 - row2) * sign <= 0
    lane1 = lax.broadcasted_iota(jnp.int32, (1, C, LANES), 2)
    lane2 = lax.broadcasted_iota(jnp.int32, (1, C, 2 * LANES), 2) % LANES
    m0_1 = lane1 < HEAD_DIM
    m0_2 = lane2 < HEAD_DIM
    rb = lax.broadcasted_iota(jnp.int32, (1, LANES, LANES), 1) // HEAD_DIM
    cb = lax.broadcasted_iota(jnp.int32, (1, LANES, LANES), 2) // HEAD_DIM
    bd_mask = rb == cb
    row1 = lax.broadcasted_iota(jnp.int32, (1, C, LANES), 1)
    col1 = lane1 % HEAD_DIM
    eye2 = (row1 == col1).astype(F32)
    off_masks = [row1 // 2 == col1 // 2]
    b = 2
    while b < C:
        off_masks.append((row1 // (2 * b) == col1 // (2 * b)) & (row1 // b != col1 // b))
        b *= 2
    return (is_rev, strict2, incl2, eye2, m0_1, m0_2, bd_mask, off_masks)


def _rwkv_scan_kernel(rf, vf, kkf, lwf, kaf, kdf, rb, vb, kkb, lwb, kab, kdb,
                      yf_o, yb_o, state_ref):
    c = pl.program_id(0)

    @pl.when(c == 0)
    def _():
        state_ref[...] = jnp.zeros_like(state_ref)

    nb, C = rf.shape[0], rf.shape[1]
    n_pairs = rf.shape[2] // LANES
    n_dir = nb * n_pairs
    consts = _scan_consts(C, n_dir, n_dir)

    row = lax.broadcasted_iota(jnp.int32, (C, C), 0)
    col = lax.broadcasted_iota(jnp.int32, (C, C), 1)
    tri_f = (col <= row).astype(F32).astype(BF16)
    tri_b = (col >= row).astype(F32).astype(BF16)
    cum_f = [_dot_exact_rhs(tri_f, lwf[b]) for b in range(nb)]
    cum_b = [_dot_exact_rhs(tri_b, lwb[b]) for b in range(nb)]

    def chains(xs_f, xs_b):
        return jnp.stack([x[:, p * LANES:(p + 1) * LANES]
                          for xs in (xs_f, xs_b) for x in xs for p in range(n_pairs)], axis=0)

    def rows(ref):
        return [ref[b].astype(F32) for b in range(nb)]

    r, v, kk, lw, ka, kd = (chains(rows(f), rows(b)) for f, b in
                            ((rf, rb), (vf, vb), (kkf, kkb), (lwf, lwb), (kaf, kab), (kdf, kdb)))
    y, s_new = _scan_chunk(r, v, kk, lw, chains(cum_f, cum_b), ka, kd, state_ref[...], consts)
    state_ref[...] = s_new
    for d, y_o in enumerate((yf_o, yb_o)):
        for b in range(nb):
            for p in range(n_pairs):
                y_o[b, :, p * LANES:(p + 1) * LANES] = y[(d * nb + b) * n_pairs + p]


def rwkv_scan(r, v, kk, lwf, lwb, kaf, kab, kdf, kdb):
    B, S, D = r.shape
    C = CHUNK
    nc = S // C
    fwd = pl.BlockSpec((B, C, D), lambda c: (0, c, 0))
    bwd = pl.BlockSpec((B, C, D), lambda c: (0, nc - 1 - c, 0))
    return pl.pallas_call(
        _rwkv_scan_kernel,
        grid=(nc,),
        in_specs=[fwd] * 6 + [bwd] * 6,
        out_specs=[fwd, bwd],
        out_shape=[jax.ShapeDtypeStruct((B, S, D), F32)] * 2,
        scratch_shapes=[pltpu.VMEM((2 * B * (D // LANES), LANES, LANES), F32)],
        compiler_params=_cparams("arbitrary"),
        name="rwkv_scan",
    )(r, v, kk, lwf, kaf, kdf, r, v, kk, lwb, kab, kdb)


def _attn_kernel(table_ref, bucket_ref, q_ref, kp_ref, km_ref, kn_ref, vp_ref, vm_ref, vn_ref,
                 o_ref, lse_ref, bias_ref, *, nblk):
    i = pl.program_id(2)
    n_sub = q_ref.shape[0] // ATTN_SUB
    TQ = ATTN_SUB
    TK = TQ + 2 * ATTN_HALF

    @pl.when((pl.program_id(0) == 0) & (pl.program_id(1) == 0) & (i == 0))
    def _():
        bucket = bucket_ref[...]
        row = lax.broadcasted_iota(jnp.int32, (TQ, TK), 0)
        colk = lax.broadcasted_iota(jnp.int32, (TQ, TK), 1)
        band = jnp.abs(colk - ATTN_HALF - row) <= ATTN_HALF
        has_prev = colk >= ATTN_HALF
        has_next = colk < TQ + ATTN_HALF
        for h in range(ATTN_SLOTS):
            acc = jnp.zeros((TQ, TK), F32)
            for b in range(N_BUCKETS):
                acc = jnp.where(bucket == b, table_ref[h * N_BUCKETS + b], acc)
            acc = jnp.where(band, acc, NEG_BIG)
            bias_ref[0, h] = acc
            bias_ref[1, h] = jnp.where(has_prev, acc, NEG_BIG)
            bias_ref[2, h] = jnp.where(has_next, acc, NEG_BIG)
            bias_ref[3, h] = jnp.where(has_prev & has_next, acc, NEG_BIG)

    m0 = lax.broadcasted_iota(jnp.int32, (TQ, LANES), 1) < HEAD_DIM
    lane = lax.broadcasted_iota(jnp.int32, (1, LANES), 1)
    head_keep = [(lane < HEAD_DIM).astype(F32).astype(BF16), (lane >= HEAD_DIM).astype(F32).astype(BF16)]
    for p in range(ATTN_OUT_DIM // LANES):
        sl = slice(p * LANES, (p + 1) * LANES)
        kfull = jnp.concatenate([kp_ref[:, sl], km_ref[:, sl], kn_ref[:, sl]], axis=0)
        vfull = jnp.concatenate([vp_ref[:, sl], vm_ref[:, sl], vn_ref[:, sl]], axis=0)
        for sb in range(n_sub):
            rows = slice(sb * TQ, (sb + 1) * TQ)
            first = (i == 0) if sb == 0 else False
            last = (i == nblk - 1) if sb == n_sub - 1 else False
            variant = jnp.where(first, 1, 0) + jnp.where(last, 2, 0)
            q2 = q_ref[rows, sl]
            kwin = kfull[sb * TQ:sb * TQ + TK]
            vwin = vfull[sb * TQ:sb * TQ + TK]
            outs, lses = [], []
            for hh in range(2):
                qm = q2 * head_keep[hh]
                s = lax.dot_general(qm, kwin, (((1,), (1,)), ((), ())), preferred_element_type=F32)
                s = s + bias_ref[variant, 2 * p + hh]
                m = jnp.max(s, axis=-1, keepdims=True)
                e = jnp.exp(s - m)
                den = jnp.sum(e, axis=-1, keepdims=True)
                pv = jnp.dot(e.astype(BF16), vwin, preferred_element_type=F32)
                outs.append(pv / den)
                lses.append(jnp.broadcast_to(m + jnp.log(den), (TQ, LANES)))
            o_ref[rows, sl] = jnp.where(m0, outs[0], outs[1]).astype(o_ref.dtype)
            lse_ref[rows, sl] = jnp.where(m0, lses[0], lses[1])


def _t5_bucket(rel):
    nb = N_BUCKETS // 2
    max_exact = nb // 2
    ret = jnp.where(rel > 0, nb, 0)
    n = jnp.abs(rel)
    nf = jnp.maximum(n, 1).astype(jnp.float32)
    large = max_exact + (jnp.log(nf / max_exact) / math.log(MAX_DISTANCE / max_exact)
                         * (nb - max_exact)).astype(jnp.int32)
    large = jnp.minimum(large, nb - 1)
    return ret + jnp.where(n < max_exact, n, large)


def attention_bias_inputs(table, gi, dilation):
    tq = ATTN_SUB
    tk = tq + 2 * ATTN_HALF
    rel = jnp.arange(tk)[None, :] - ATTN_HALF - jnp.arange(tq)[:, None]
    bucket = _t5_bucket(rel * dilation).astype(jnp.int32)
    tbl = table[:, gi * ATTN_SLOTS:(gi + 1) * ATTN_SLOTS].astype(F32).T.reshape(-1)
    return tbl, bucket


def dilated_attention(qkv, tbl, bucket, gi, *, tq):
    B, dilation, L, _ = qkv.shape
    nblk = L // tq
    hb = tq // ATTN_HALF
    nhalf = L // ATTN_HALF
    W = ATTN_OUT_DIM

    def main(off):
        return pl.BlockSpec((None, None, tq, W), lambda b, r, i: (b, r, i, off))

    def prev(off):
        return pl.BlockSpec((None, None, ATTN_HALF, W),
                            lambda b, r, i: (b, r, jnp.maximum(i * hb - 1, 0), off))

    def nxt(off):
        return pl.BlockSpec((None, None, ATTN_HALF, W),
                            lambda b, r, i: (b, r, jnp.minimum((i + 1) * hb, nhalf - 1), off))

    sub_shape = (ATTN_SUB, ATTN_SUB + 2 * ATTN_HALF)
    out_spec = pl.BlockSpec((None, None, tq, W), lambda b, r, i: (b, r, i, 0))
    return pl.pallas_call(
        functools.partial(_attn_kernel, nblk=nblk),
        grid=(B, dilation, nblk),
        in_specs=[pl.BlockSpec(memory_space=pltpu.SMEM),
                  pl.BlockSpec(sub_shape, lambda b, r, i: (0, 0)),
                  main(0), prev(1), main(1), nxt(1), prev(2), main(2), nxt(2)],
        out_specs=[out_spec, out_spec],
        out_shape=[jax.ShapeDtypeStruct((B, dilation, L, W), BF16),
                   jax.ShapeDtypeStruct((B, dilation, L, W), F32)],
        scratch_shapes=[pltpu.VMEM((4, ATTN_SLOTS) + sub_shape, F32)],
        compiler_params=_cparams("arbitrary", "arbitrary", "arbitrary"),
        name=f"dilated_attn_g{gi}",
    )(tbl, bucket, qkv, qkv, qkv, qkv, qkv, qkv, qkv)


def _branch_post_kernel(yf_ref, yb_ref, bonus_ref, g_ref, gnw_ref, gnb_ref,
                        o0, o1, o2, l0, l1, l2, orw_ref, oat_ref, *scratch):
    y = yf_ref[...] + yb_ref[...]
    ones_bd = _head_block_ones(2 * LANES)
    tm, width = y.shape

    def head_mean(x):
        return jnp.concatenate(
            [_head_sum(x[:, c:c + 2 * LANES], ones_bd) for c in range(0, width, 2 * LANES)],
            axis=1) * (1.0 / HEAD_DIM)

    mean = head_mean(y)
    yc = y - mean
    var = head_mean(yc * yc)
    yn = yc * lax.rsqrt(var + GN_EPS) * gnw_ref[...] + gnb_ref[...]
    orw_ref[...] = ((yn + bonus_ref[...]) * g_ref[...]).astype(BF16)

    def natural_order(ref, scr):
        d = ref.shape[0]
        for t in range(ATTN_OUT_DIM // LANES):
            for res in range(d):
                scr[t, pl.ds(res, tm // d, stride=d), :] = (
                    ref[res, :, t * LANES:(t + 1) * LANES].astype(F32))
        return jnp.concatenate([scr[t] for t in range(ATTN_OUT_DIM // LANES)], axis=1)

    s_o1, s_o2, s_l1, s_l2 = scratch
    oa, ob, oc = o0[0].astype(F32), natural_order(o1, s_o1), natural_order(o2, s_o2)
    la, lb, lc = l0[0], natural_order(l1, s_l1), natural_order(l2, s_l2)
    m = jnp.maximum(jnp.maximum(la, lb), lc)
    ea, eb, ec = jnp.exp(la - m), jnp.exp(lb - m), jnp.exp(lc - m)
    den = ea + eb + ec
    oat_ref[...] = ((ea * oa + eb * ob + ec * oc) / den).astype(BF16)


def branch_post(yf, yb, bonus, g, gn_w, gn_b, layer, outs, lses, *, tm):
    M, D = yf.shape
    A = ATTN_OUT_DIM
    batch = outs[0].shape[0]
    nb = M // batch // tm
    big = pl.BlockSpec((tm, D), lambda b, i: (b * nb + i, 0))
    rowp = pl.BlockSpec((None, 1, D), lambda b, i: (layer, 0, 0))
    attn_specs = [pl.BlockSpec((None, o.shape[1], tm // o.shape[1], A), lambda b, i: (b, 0, i, 0))
                  for o in outs]
    return pl.pallas_call(
        _branch_post_kernel,
        grid=(batch, nb),
        in_specs=[big] * 4 + [rowp, rowp] + attn_specs + attn_specs,
        out_specs=[big, pl.BlockSpec((tm, A), lambda b, i: (b * nb + i, 0))],
        out_shape=[jax.ShapeDtypeStruct((M, D), BF16), jax.ShapeDtypeStruct((M, A), BF16)],
        scratch_shapes=[pltpu.VMEM((A // LANES, tm, LANES), F32)] * 4,
        compiler_params=_cparams("parallel", "arbitrary"),
        name="branch_post",
    )(yf, yb, bonus, g, gn_w.reshape(-1, 1, D), gn_b.reshape(-1, 1, D), *outs, *lses)


def _merge_kernel(orw_ref, oat_ref, wr_ref, wa_ref, gr_ref, ga_ref, o_ref):
    a = jnp.dot(orw_ref[...], wr_ref[...].astype(BF16), preferred_element_type=F32)
    b = jnp.dot(oat_ref[...], wa_ref[...].astype(BF16), preferred_element_type=F32)
    o_ref[...] = (gr_ref[...].astype(F32) * a + ga_ref[...].astype(F32) * b).astype(o_ref.dtype)


def branch_merge(orw, oat, w_r, w_a, layer, gates, *, tm, tn):
    M = orw.shape[0]
    N = w_r.shape[2]
    nj = N // tn
    return pl.pallas_call(
        _merge_kernel,
        grid=(M // tm, nj),
        in_specs=[pl.BlockSpec((tm, orw.shape[1]), lambda i, j: (i, 0)),
                  pl.BlockSpec((tm, oat.shape[1]), lambda i, j: (i, 0)),
                  pl.BlockSpec((None, w_r.shape[1], tn), lambda i, j: (layer, 0, j)),
                  pl.BlockSpec((None, w_a.shape[1], tn), lambda i, j: (layer, 0, j)),
                  pl.BlockSpec((tm, tn), lambda i, j: (i, j)),
                  pl.BlockSpec((tm, tn), lambda i, j: (i, j + nj))],
        out_specs=pl.BlockSpec((tm, tn), lambda i, j: (i, j)),
        out_shape=jax.ShapeDtypeStruct((M, N), BF16),
        compiler_params=_cparams("parallel", "arbitrary"),
        name="branch_merge",
    )(orw, oat, w_r, w_a, gates, gates)


def _lora_weights(w_up, a_up):
    z = jnp.zeros_like(w_up[:, 0])
    rows = [jnp.concatenate([w_up[:, 0], z, z, z], axis=2),
            jnp.concatenate([z, w_up[:, 1], z, z], axis=2),
            jnp.concatenate([z, z, a_up[:, 0], z], axis=2),
            jnp.concatenate([z, z, z, a_up[:, 1]], axis=2)]
    return jnp.concatenate(rows, axis=1).astype(BF16)


def kernel(x, norm1_g, w_in, tshift_mu, w0, w_lora_up, a0, a_lora_up, g_lora_up, k_k, k_a, r_k,
           gn_w, gn_b, rel_bias, w_branch_rwkv, w_branch_attn, w_out, norm2_g, w_mlp_in, w_mlp_out,
           final_g):
    B, S, D = x.shape
    M = B * S
    depth = w_in.shape[0]
    c_rkv = 3 * RWKV_DIM
    c_slab = c_rkv + LORA_COLS
    c_attn = c_slab + 3 * ATTN_DIM
    w_qkv = w_in[:, :, c_slab:c_attn].astype(BF16)
    w_lora_up = _lora_weights(w_lora_up, a_lora_up)
    g_up = g_lora_up.astype(BF16)
    bias_inputs = [attention_bias_inputs(rel_bias, gi, dilation)
                   for gi, (_, dilation) in enumerate(ATTN_GROUPS)]
    h = x.reshape(M, D)
    for l in range(depth):
        g1 = norm1_g
        rkv = norm_matmul(h, g1, w_in, l, n_cols=c_rkv, tm=1024, tn=1024)
        lora = norm_matmul(h, g1, w_in, l, col0=c_rkv, n_cols=LORA_COLS, tm=1024, tn=LORA_COLS,
                           pieces=LORA_COLS // LANES)
        qkv_groups = qkv_proj(h, g1, w_qkv, l, batch=B, tm=1024)
        gates = norm_matmul(h, g1, w_in, l, col0=c_attn, n_cols=2 * D, tm=1024, tn=1024,
                            pieces=1024 // LANES, act="sigmoid", out_dtype=BF16)

        (r, v, kk, lwf, lwb, kaf, kab, kdf, kdb, g, bonus) = rwkv_prep(
            rkv.reshape(B, S, c_rkv), lora.reshape(B, S, LORA_COLS), l, tshift_mu, w0, a0,
            w_lora_up, g_up, k_k, k_a, r_k, ts=256)
        yf, yb = rwkv_scan(r, v, kk, lwf, lwb, kaf, kab, kdf, kdb)

        outs, lses = [], []
        for gi, (window, dilation) in enumerate(ATTN_GROUPS):
            assert window // (2 * dilation) == ATTN_HALF
            o, lse = dilated_attention(qkv_groups[gi], *bias_inputs[gi], gi,
                                       tq=min(S // dilation, ATTN_MAX_BLOCK))
            outs.append(o)
            lses.append(lse)

        orw, oat = branch_post(yf.reshape(M, RWKV_DIM), yb.reshape(M, RWKV_DIM),
                               bonus.reshape(M, RWKV_DIM), g.reshape(M, RWKV_DIM),
                               gn_w, gn_b, l, outs, lses, tm=512)
        merged = branch_merge(orw, oat, w_branch_rwkv, w_branch_attn, l, gates, tm=1024, tn=1024)
        h = matmul_residual(merged, w_out, l, h, tm=1024, tn=1024, tk=D)

        act = norm_matmul(h, norm2_g, w_mlp_in, l, n_cols=w_mlp_in.shape[2], tm=1024, tn=1024,
                          act="relu2", out_dtype=BF16)
        h = matmul_residual(act, w_mlp_out, l, h, tm=1024, tn=1024, tk=2048)
    out = rmsnorm_rows(h, final_g, tm=512)
    return out.reshape(B, S, D)
```

```python
import functools
import math

import jax
import jax.numpy as jnp
from jax import lax
from jax.experimental import pallas as pl
from jax.experimental.pallas import tpu as pltpu

F32 = jnp.float32
BF16 = jnp.bfloat16

HEAD_DIM = 64
LANES = 128
RWKV_DIM = 1024
DECAY_LORA = 96
AAA_LORA = 96
GATE_LORA = 256
LORA_COLS = GATE_LORA + 2 * DECAY_LORA + 2 * AAA_LORA
LORA_TAIL = LORA_COLS - GATE_LORA
ATTN_GROUPS = ((128, 1), (512, 4), (2048, 16))
ATTN_SLOTS = 8
ATTN_DIM = 1536
ATTN_OUT_DIM = ATTN_SLOTS * HEAD_DIM
ATTN_HALF = 64
ATTN_SUB = 128
ATTN_MAX_BLOCK = 1024
N_BUCKETS = 32
MAX_DISTANCE = 1024
RMS_EPS = 1e-6
GN_EPS = 64e-5
L2_EPS = 1e-12
NEG_BIG = -1e30
CHUNK = 64
VMEM_LIMIT = 56 * 1024 * 1024


def _cparams(*sem):
    return pltpu.CompilerParams(dimension_semantics=sem, vmem_limit_bytes=VMEM_LIMIT)


def _dot(a, b):
    return jnp.dot(a.astype(BF16), b.astype(BF16), preferred_element_type=F32)


def _split3(x):
    p0 = x.astype(BF16)
    r1 = x - p0.astype(F32)
    p1 = r1.astype(BF16)
    p2 = (r1 - p1.astype(F32)).astype(BF16)
    return p0, p1, p2


def _dot_exact_rhs(a_bf16, x):
    return sum(jnp.dot(a_bf16, p, preferred_element_type=F32) for p in _split3(x))


def _head_sum(x, ones_bd):
    hi = x.astype(BF16)
    lo = (x - hi.astype(F32)).astype(BF16)
    return (jnp.dot(hi, ones_bd, preferred_element_type=F32)
            + jnp.dot(lo, ones_bd, preferred_element_type=F32))


def _head_block_ones(width):
    r = lax.broadcasted_iota(jnp.int32, (width, width), 0) // HEAD_DIM
    c = lax.broadcasted_iota(jnp.int32, (width, width), 1) // HEAD_DIM
    return (r == c).astype(BF16)


def _normalise_rows(x_ref, g_ref, u_ref, start, size):
    rows = pl.ds(start, size)
    x = x_ref[rows, :]
    ms = jnp.mean(x * x, axis=-1, keepdims=True)
    u_ref[rows, :] = (x * lax.rsqrt(ms + RMS_EPS) * g_ref[...]).astype(BF16)


def _x_rows_index(i, j, ni, nj):
    if nj == 1:
        return i
    return jnp.minimum(i + jnp.where(j >= 1, 1, 0), ni - 1)


BF16_ROWS = 16


def _with_normalised_tile(x_ref, g_ref, u_refs, nj, body):
    i, j = pl.program_id(0), pl.program_id(1)
    tm = x_ref.shape[0]
    if nj == 1:
        _normalise_rows(x_ref, g_ref, u_refs[0], 0, tm)
        body(u_refs[0])
        return

    @pl.when((i == 0) & (j == 0))
    def _():
        _normalise_rows(x_ref, g_ref, u_refs[0], 0, tm)

    size = -(-tm // (nj - 1))
    size = -(-size // BF16_ROWS) * BF16_ROWS
    start = pl.multiple_of(jnp.clip((j - 1) * size, 0, tm - size), BF16_ROWS)
    for parity in range(2):
        @pl.when(i % 2 == parity)
        def _():
            body(u_refs[parity])
            _normalise_rows(x_ref, g_ref, u_refs[1 - parity], start, size)


def _norm_mm_kernel(x_ref, g_ref, *refs, act, nj):
    *w_refs, o_ref, u0_ref, u1_ref = refs

    def body(u_ref):
        w = jnp.concatenate([w_ref[...].astype(BF16) for w_ref in w_refs], axis=1)
        acc = jnp.dot(u_ref[...], w, preferred_element_type=F32)
        if act == "relu2":
            acc = jnp.square(jnp.maximum(acc, 0.0))
        elif act == "sigmoid":
            acc = 0.5 * jnp.tanh(0.5 * acc) + 0.5
        o_ref[...] = acc.astype(o_ref.dtype)

    _with_normalised_tile(x_ref, g_ref, (u0_ref, u1_ref), nj, body)


def norm_matmul(x, g, w, layer, *, col0=0, n_cols, tm, tn, pieces=1, act=None, out_dtype=F32):
    M, K = x.shape
    pw = tn // pieces
    assert pw * pieces == tn and col0 % pw == 0 and n_cols % tn == 0
    w_specs = [pl.BlockSpec((None, K, pw), functools.partial(
        lambda i, j, t: (layer, 0, col0 // pw + j * pieces + t), t=t)) for t in range(pieces)]
    ni, nj = M // tm, n_cols // tn
    return pl.pallas_call(
        functools.partial(_norm_mm_kernel, act=act, nj=nj),
        grid=(ni, nj),
        in_specs=[pl.BlockSpec((tm, K), lambda i, j: (_x_rows_index(i, j, ni, nj), 0)),
                  pl.BlockSpec((None, 1, K), lambda i, j: (layer, 0, 0))] + w_specs,
        out_specs=pl.BlockSpec((tm, tn), lambda i, j: (i, j)),
        out_shape=jax.ShapeDtypeStruct((M, n_cols), out_dtype),
        scratch_shapes=[pltpu.VMEM((tm, K), BF16)] * 2,
        compiler_params=_cparams("arbitrary", "arbitrary"),
        name="norm_matmul_" + (act or "id"),
    )(x, g.reshape(-1, 1, K), *([w] * pieces))


def _qkv_proj_kernel(x_ref, g_ref, w_ref, o0_ref, o1_ref, o2_ref, u_ref, acc_ref):
    tm = x_ref.shape[0]

    @pl.when(pl.program_id(1) == 0)
    def _():
        _normalise_rows(x_ref, g_ref, u_ref, 0, tm)

    acc = jnp.dot(u_ref[...], w_ref[...], preferred_element_type=F32)
    acc = acc * jnp.where(pl.program_id(1) == 0, HEAD_DIM ** -0.5, 1.0)
    tiles_per_group = ATTN_OUT_DIM // LANES
    for gi, (o_ref, (_, d)) in enumerate(zip((o0_ref, o1_ref, o2_ref), ATTN_GROUPS)):
        if d == 1:
            o_ref[0, 0] = acc[:, gi * ATTN_OUT_DIM:(gi + 1) * ATTN_OUT_DIM].astype(BF16)
            continue
        for t in range(tiles_per_group):
            c = gi * tiles_per_group + t
            acc_ref[c] = acc[:, c * LANES:(c + 1) * LANES]
            for res in range(d):
                o_ref[0, res, :, t * LANES:(t + 1) * LANES] = (
                    acc_ref[c, pl.ds(res, tm // d, stride=d), :].astype(BF16))


def qkv_proj(x, g, w, layer, *, batch, tm):
    M, K = x.shape
    S = M // batch
    nb = S // tm
    outs_shape, outs_spec = [], []
    for _, d in ATTN_GROUPS:
        outs_shape.append(jax.ShapeDtypeStruct((batch, d, S // d, 3 * ATTN_OUT_DIM), BF16))
        outs_spec.append(pl.BlockSpec((1, d, tm // d, ATTN_OUT_DIM),
                                      lambda i, j: (i // nb, 0, i % nb, j)))
    return pl.pallas_call(
        _qkv_proj_kernel,
        grid=(M // tm, 3),
        in_specs=[pl.BlockSpec((tm, K), lambda i, j: (i, 0)),
                  pl.BlockSpec((None, 1, K), lambda i, j: (layer, 0, 0)),
                  pl.BlockSpec((None, K, ATTN_DIM), lambda i, j: (layer, 0, j))],
        out_specs=outs_spec,
        out_shape=outs_shape,
        scratch_shapes=[pltpu.VMEM((tm, K), BF16), pltpu.VMEM((ATTN_DIM // LANES, tm, LANES), F32)],
        compiler_params=_cparams("parallel", "arbitrary"),
        name="qkv_proj",
    )(x, g.reshape(-1, 1, K), w)


def _mm_res_kernel(x_ref, w_ref, r_ref, o_ref, acc_ref):
    k = pl.program_id(2)

    @pl.when(k == 0)
    def _():
        acc_ref[...] = jnp.zeros_like(acc_ref)

    acc_ref[...] += jnp.dot(x_ref[...], w_ref[...].astype(BF16), preferred_element_type=F32)

    @pl.when(k == pl.num_programs(2) - 1)
    def _():
        o_ref[...] = r_ref[...] + acc_ref[...]


def matmul_residual(x, w, layer, res, *, tm, tn, tk):
    M, K = x.shape
    N = w.shape[2]
    return pl.pallas_call(
        _mm_res_kernel,
        grid=(M // tm, N // tn, K // tk),
        in_specs=[pl.BlockSpec((tm, tk), lambda i, j, k: (i, k)),
                  pl.BlockSpec((None, tk, tn), lambda i, j, k: (layer, k, j)),
                  pl.BlockSpec((tm, tn), lambda i, j, k: (i, j))],
        out_specs=pl.BlockSpec((tm, tn), lambda i, j, k: (i, j)),
        out_shape=jax.ShapeDtypeStruct((M, N), F32),
        scratch_shapes=[pltpu.VMEM((tm, tn), F32)],
        compiler_params=_cparams("parallel", "parallel", "arbitrary"),
        name="matmul_residual",
    )(x, w, res)


def _rmsnorm_kernel(x_ref, g_ref, o_ref):
    x = x_ref[...]
    ms = jnp.mean(x * x, axis=-1, keepdims=True)
    o_ref[...] = x * lax.rsqrt(ms + RMS_EPS) * g_ref[...]


def rmsnorm_rows(x, g, *, tm):
    M, K = x.shape
    return pl.pallas_call(
        _rmsnorm_kernel,
        grid=(M // tm,),
        in_specs=[pl.BlockSpec((tm, K), lambda i: (i, 0)),
                  pl.BlockSpec((1, K), lambda i: (0, 0))],
        out_specs=pl.BlockSpec((tm, K), lambda i: (i, 0)),
        out_shape=jax.ShapeDtypeStruct((M, K), F32),
        compiler_params=_cparams("parallel"),
        name="final_rmsnorm",
    )(x, g.reshape(1, K))


def _token_shift(main, prev_blk, next_blk, mu, first, last):
    rows = main.shape[0]
    prev_row = jnp.where(first, 0.0, prev_blk[7:8, :])
    next_row = jnp.where(last, 0.0, next_blk[0:1, :])
    rid = lax.broadcasted_iota(jnp.int32, main.shape, 0)
    up = jnp.where(rid == 0, prev_row, pltpu.roll(main, 1, 0))
    dn = jnp.where(rid == rows - 1, next_row, pltpu.roll(main, rows - 1, 0))
    return main * (1.0 - mu) + (0.5 * mu) * (up + dn)


def _rwkv_prep_kernel(r_m, r_p, r_n, k_m, k_p, k_n, v_m, v_p, v_n, l_m, l_p, l_n,
                      mu_ref, w0_ref, a0_ref, lora_w_ref, g_up_ref, kk_ref, ka_ref, rk_ref,
                      r_o, v_o, kk_o, lwf_o, lwb_o, kaf_o, kab_o, kdf_o, kdb_o, g_o, bonus_o):
    i = pl.program_id(1)
    first = i == 0
    last = i == pl.num_programs(1) - 1
    D = RWKV_DIM
    r = _token_shift(r_m[0], r_p[0], r_n[0], mu_ref[:, 0:D], first, last)
    k = _token_shift(k_m[0], k_p[0], k_n[0], mu_ref[:, D:2 * D], first, last)
    v = _token_shift(v_m[0], v_p[0], v_n[0], mu_ref[:, 2 * D:3 * D], first, last)
    lo = _token_shift(l_m[0], l_p[0], l_n[0], mu_ref[:, 3 * D:], first, last)

    g = _dot(jax.nn.sigmoid(lo[:, :GATE_LORA]), g_up_ref[...])
    tail = lo[:, GATE_LORA:]
    col = lax.broadcasted_iota(jnp.int32, tail.shape, 1)
    tail = jnp.where(col < 2 * DECAY_LORA, jnp.tanh(tail), tail)
    up = _dot(tail, lora_w_ref[...])

    ones_bd = _head_block_ones(2 * LANES)

    def head_sum(x):
        return jnp.concatenate(
            [_head_sum(x[:, c:c + 2 * LANES], ones_bd) for c in range(0, RWKV_DIM, 2 * LANES)], axis=1)

    kk = k * kk_ref[...]
    kk = kk * jnp.minimum(lax.rsqrt(head_sum(kk * kk)), 1.0 / L2_EPS)

    r_o[0] = r.astype(r_o.dtype)
    v_o[0] = v.astype(v_o.dtype)
    kk_o[0] = kk.astype(kk_o.dtype)
    g_o[0] = g.astype(g_o.dtype)
    kd_sum = jnp.zeros_like(k)
    for d, (lw_o, ka_o, kd_o) in enumerate(((lwf_o, kaf_o, kdf_o), (lwb_o, kab_o, kdb_o))):
        z = w0_ref[d:d + 1, :] + up[:, d * RWKV_DIM:(d + 1) * RWKV_DIM]
        nz = -z
        softplus = jnp.maximum(nz, 0.0) + jnp.log(1.0 + jnp.exp(-jnp.abs(nz)))
        lw_o[0] = -jnp.exp(-softplus - 0.5)
        a = jax.nn.sigmoid(a0_ref[d:d + 1, :] + up[:, (2 + d) * RWKV_DIM:(3 + d) * RWKV_DIM])
        kd = k * (1.0 + (a - 1.0) * ka_ref[...])
        ka_o[0] = (kk * a).astype(ka_o.dtype)
        kd_o[0] = kd.astype(kd_o.dtype)
        kd_sum = kd_sum + kd
    bonus_o[0] = (head_sum(r * kd_sum * rk_ref[...]) * v).astype(bonus_o.dtype)


def rwkv_prep(rkv, lora, layer, mu, w0, a0, lora_w, g_up, k_k, k_a, r_k, *, ts):
    B, S, _ = rkv.shape
    D = RWKV_DIM
    nblk8 = S // 8

    def main_spec(width, cblk):
        return pl.BlockSpec((1, ts, width), lambda b, i: (b, i, cblk))

    def prev_spec(width, cblk):
        return pl.BlockSpec((1, 8, width), lambda b, i: (b, jnp.maximum(i * (ts // 8) - 1, 0), cblk))

    def next_spec(width, cblk):
        return pl.BlockSpec((1, 8, width), lambda b, i: (b, jnp.minimum((i + 1) * (ts // 8), nblk8 - 1), cblk))

    def layer_spec(rows, width):
        return pl.BlockSpec((None, rows, width), lambda b, i: (layer, 0, 0))

    in_specs = []
    args = []
    for cblk in range(3):
        in_specs += [main_spec(D, cblk), prev_spec(D, cblk), next_spec(D, cblk)]
        args += [rkv, rkv, rkv]
    in_specs += [main_spec(LORA_COLS, 0), prev_spec(LORA_COLS, 0), next_spec(LORA_COLS, 0)]
    args += [lora, lora, lora]
    depth = mu.shape[0]
    in_specs += [layer_spec(1, 3 * D + LORA_COLS), layer_spec(2, D), layer_spec(2, D),
                 layer_spec(LORA_TAIL, 4 * D), layer_spec(GATE_LORA, D),
                 layer_spec(1, D), layer_spec(1, D), layer_spec(1, D)]
    args += [mu.reshape(depth, 1, -1), w0, a0, lora_w, g_up,
             k_k.reshape(depth, 1, D), k_a.reshape(depth, 1, D), r_k.reshape(depth, 1, D)]
    out_spec = pl.BlockSpec((1, ts, D), lambda b, i: (b, i, 0))
    out_dtypes = [BF16, BF16, BF16, F32, F32, BF16, BF16, BF16, BF16, BF16, BF16]
    n_out = len(out_dtypes)
    return pl.pallas_call(
        _rwkv_prep_kernel,
        grid=(B, S // ts),
        in_specs=in_specs,
        out_specs=[out_spec] * n_out,
        out_shape=[jax.ShapeDtypeStruct((B, S, D), dt) for dt in out_dtypes],
        compiler_params=_cparams("parallel", "arbitrary"),
        name="rwkv_prep",
    )(*args)


_BMM = (((2,), (1,)), ((0,), (0,)))
_BMM_NT = (((2,), (2,)), ((0,), (0,)))
_BMM_TN = (((1,), (1,)), ((0,), (0,)))


def _bdot(a, b, dims=_BMM):
    return lax.dot_general(a.astype(BF16), b.astype(BF16), dims, preferred_element_type=F32)


def _bd_stack(x, m0):
    zero = jnp.zeros_like(x)
    return jnp.concatenate([jnp.where(m0, x, zero), jnp.where(m0, zero, x)], axis=-2)


def _scan_chunk(r, v, kk, lw, cum, ka, kd, s_bd, consts):
    is_rev, strict2, incl2, eye2, m0_1, m0_2, bd_mask, off_masks = consts
    C = r.shape[1]
    cum_prev = cum - lw
    e_cum = jnp.exp(cum)
    e_prev = jnp.exp(cum_prev)
    e_neg = jnp.exp(-cum)
    total = jnp.where(is_rev, cum[:, 0:1, :], cum[:, C - 1:C, :])
    e_rest = jnp.exp(total - cum)
    r0 = r * e_cum
    at0 = -kk * e_prev
    bt = ka * e_neg
    kt = kd * e_neg
    bh = ka * e_rest
    kh = kd * e_rest

    gram = _bdot(jnp.concatenate([at0, r0], axis=1),
                 jnp.concatenate([_bd_stack(bt, m0_1), _bd_stack(kt, m0_1)], axis=1), _BMM_NT)
    zero = jnp.zeros_like(gram[:, :C])
    a_top = jnp.where(strict2, gram[:, :C], zero)
    a_bot = jnp.where(incl2, gram[:, C:], zero)
    a_ab = a_top[:, :, :LANES]
    a_ak = a_top[:, :, LANES:]

    zero1 = jnp.zeros_like(a_ab)
    t = eye2 + jnp.where(off_masks[0], a_ab, zero1)
    for off_mask in off_masks[1:]:
        z = _bdot(t, _bd_stack(jnp.where(off_mask, a_ab, zero1), m0_1))
        t = t + _bdot(z, _bd_stack(t, m0_1))

    av = _bdot(a_ak, _bd_stack(v, m0_1))
    wu = _bdot(t, _bd_stack(jnp.concatenate([at0, av], axis=2), m0_2))
    w_a = wu[:, :, :LANES]
    u_v = wu[:, :, LANES:]

    wr = _bdot(jnp.concatenate([w_a, r0], axis=1), s_bd, _BMM_NT)
    u = wr[:, :C] + u_v
    y = wr[:, C:] + _bdot(a_bot, jnp.concatenate([_bd_stack(u, m0_1), _bd_stack(v, m0_1)], axis=1))
    upd = _bdot(jnp.concatenate([u, v], axis=1), jnp.concatenate([bh, kh], axis=1), _BMM_TN)
    s_new = s_bd * jnp.exp(total) + jnp.where(bd_mask, upd, jnp.zeros_like(upd))
    return y, s_new


def _scan_consts(C, n_fwd, n_bwd):
    G = n_fwd + n_bwd
    is_rev = lax.broadcasted_iota(jnp.int32, (G, 1, 1), 0) >= n_fwd
    sign = jnp.where(is_rev, -1, 1)
    row2 = lax.broadcasted_iota(jnp.int32, (G, C, 2 * LANES), 1)
    col2 = lax.broadcasted_iota(jnp.int32, (G, C, 2 * LANES), 2) % HEAD_DIM
    strict2 = (col2 - row2) * sign < 0
    incl2 = (col2 - row2) * sign <= 0
    lane1 = lax.broadcasted_iota(jnp.int32, (1, C, LANES), 2)
    lane2 = lax.broadcasted_iota(jnp.int32, (1, C, 2 * LANES), 2) % LANES
    m0_1 = lane1 < HEAD_DIM
    m0_2 = lane2 < HEAD_DIM
    rb = lax.broadcasted_iota(jnp.int32, (1, LANES, LANES), 1) // HEAD_DIM
    cb = lax.broadcasted_iota(jnp.int32, (1, LANES, LANES), 2) // HEAD_DIM
    bd_mask = rb == cb
    row1 = lax.broadcasted_iota(jnp.int32, (1, C, LANES), 1)
    col1 = lane1 % HEAD_DIM
    eye2 = (row1 == col1).astype(F32)
    off_masks = [row1 // 2 == col1 // 2]
    b = 2
    while b < C:
        off_masks.append((row1 // (2 * b) == col1 // (2 * b)) & (row1 // b != col1 // b))
        b *= 2
    return (is_rev, strict2, incl2, eye2, m0_1, m0_2, bd_mask, off_masks)


def _rwkv_scan_kernel(rf, vf, kkf, lwf, kaf, kdf, rb, vb, kkb, lwb, kab, kdb,
                      yf_o, yb_o, state_ref):
    c = pl.program_id(0)

    @pl.when(c == 0)
    def _():
        state_ref[...] = jnp.zeros_like(state_ref)

    nb, C = rf.shape[0], rf.shape[1]
    n_pairs = rf.shape[2] // LANES
    n_dir = nb * n_pairs
    consts = _scan_consts(C, n_dir, n_dir)

    row = lax.broadcasted_iota(jnp.int32, (C, C), 0)
    col = lax.broadcasted_iota(jnp.int32, (C, C), 1)
    tri_f = (col <= row).astype(F32).astype(BF16)
    tri_b = (col >= row).astype(F32).astype(BF16)
    cum_f = [_dot_exact_rhs(tri_f, lwf[b]) for b in range(nb)]
    cum_b = [_dot_exact_rhs(tri_b, lwb[b]) for b in range(nb)]

    def chains(xs_f, xs_b):
        return jnp.stack([x[:, p * LANES:(p + 1) * LANES]
                          for xs in (xs_f, xs_b) for x in xs for p in range(n_pairs)], axis=0)

    def rows(ref):
        return [ref[b].astype(F32) for b in range(nb)]

    r, v, kk, lw, ka, kd = (chains(rows(f), rows(b)) for f, b in
                            ((rf, rb), (vf, vb), (kkf, kkb), (lwf, lwb), (kaf, kab), (kdf, kdb)))
    y, s_new = _scan_chunk(r, v, kk, lw, chains(cum_f, cum_b), ka, kd, state_ref[...], consts)
    state_ref[...] = s_new
    for d, y_o in enumerate((yf_o, yb_o)):
        for b in range(nb):
            for p in range(n_pairs):
                y_o[b, :, p * LANES:(p + 1) * LANES] = y[(d * nb + b) * n_pairs + p].astype(y_o.dtype)


def rwkv_scan(r, v, kk, lwf, lwb, kaf, kab, kdf, kdb):
    B, S, D = r.shape
    C = CHUNK
    nc = S // C
    fwd = pl.BlockSpec((B, C, D), lambda c: (0, c, 0))
    bwd = pl.BlockSpec((B, C, D), lambda c: (0, nc - 1 - c, 0))
    return pl.pallas_call(
        _rwkv_scan_kernel,
        grid=(nc,),
        in_specs=[fwd] * 6 + [bwd] * 6,
        out_specs=[fwd, bwd],
        out_shape=[jax.ShapeDtypeStruct((B, S, D), BF16)] * 2,
        scratch_shapes=[pltpu.VMEM((2 * B * (D // LANES), LANES, LANES), F32)],
        compiler_params=_cparams("arbitrary"),
        name="rwkv_scan",
    )(r, v, kk, lwf, kaf, kdf, r, v, kk, lwb, kab, kdb)


def _attn_kernel(table_ref, bucket_ref, q_ref, kp_ref, km_ref, kn_ref, vp_ref, vm_ref, vn_ref,
                 o_ref, lse_ref, bias_ref, *, nblk):
    i = pl.program_id(2)
    n_sub = q_ref.shape[0] // ATTN_SUB
    TQ = ATTN_SUB
    TK = TQ + 2 * ATTN_HALF

    @pl.when((pl.program_id(0) == 0) & (pl.program_id(1) == 0) & (i == 0))
    def _():
        bucket = bucket_ref[...]
        row = lax.broadcasted_iota(jnp.int32, (TQ, TK), 0)
        colk = lax.broadcasted_iota(jnp.int32, (TQ, TK), 1)
        band = jnp.abs(colk - ATTN_HALF - row) <= ATTN_HALF
        has_prev = colk >= ATTN_HALF
        has_next = colk < TQ + ATTN_HALF
        for h in range(ATTN_SLOTS):
            acc = jnp.zeros((TQ, TK), F32)
            for b in range(N_BUCKETS):
                acc = jnp.where(bucket == b, table_ref[h * N_BUCKETS + b], acc)
            acc = jnp.where(band, acc, NEG_BIG)
            bias_ref[0, h] = acc
            bias_ref[1, h] = jnp.where(has_prev, acc, NEG_BIG)
            bias_ref[2, h] = jnp.where(has_next, acc, NEG_BIG)
            bias_ref[3, h] = jnp.where(has_prev & has_next, acc, NEG_BIG)

    m0 = lax.broadcasted_iota(jnp.int32, (TQ, LANES), 1) < HEAD_DIM
    lane = lax.broadcasted_iota(jnp.int32, (1, LANES), 1)
    head_keep = [(lane < HEAD_DIM).astype(F32).astype(BF16), (lane >= HEAD_DIM).astype(F32).astype(BF16)]
    for p in range(ATTN_OUT_DIM // LANES):
        sl = slice(p * LANES, (p + 1) * LANES)
        kfull = jnp.concatenate([kp_ref[:, sl], km_ref[:, sl], kn_ref[:, sl]], axis=0)
        vfull = jnp.concatenate([vp_ref[:, sl], vm_ref[:, sl], vn_ref[:, sl]], axis=0)
        for sb in range(n_sub):
            rows = slice(sb * TQ, (sb + 1) * TQ)
            first = (i == 0) if sb == 0 else False
            last = (i == nblk - 1) if sb == n_sub - 1 else False
            variant = jnp.where(first, 1, 0) + jnp.where(last, 2, 0)
            q2 = q_ref[rows, sl]
            kwin = kfull[sb * TQ:sb * TQ + TK]
            vwin = vfull[sb * TQ:sb * TQ + TK]
            outs, lses = [], []
            for hh in range(2):
                qm = q2 * head_keep[hh]
                s = lax.dot_general(qm, kwin, (((1,), (1,)), ((), ())), preferred_element_type=F32)
                s = s + bias_ref[variant, 2 * p + hh]
                m = jnp.max(s, axis=-1, keepdims=True)
                e = jnp.exp(s - m)
                den = jnp.sum(e, axis=-1, keepdims=True)
                pv = jnp.dot(e.astype(BF16), vwin, preferred_element_type=F32)
                outs.append(pv / den)
                lses.append(jnp.broadcast_to(m + jnp.log(den), (TQ, LANES)))
            o_ref[rows, sl] = jnp.where(m0, outs[0], outs[1]).astype(o_ref.dtype)
            lse_ref[rows, sl] = jnp.where(m0, lses[0], lses[1])


def _t5_bucket(rel):
    nb = N_BUCKETS // 2
    max_exact = nb // 2
    ret = jnp.where(rel > 0, nb, 0)
    n = jnp.abs(rel)
    nf = jnp.maximum(n, 1).astype(jnp.float32)
    large = max_exact + (jnp.log(nf / max_exact) / math.log(MAX_DISTANCE / max_exact)
                         * (nb - max_exact)).astype(jnp.int32)
    large = jnp.minimum(large, nb - 1)
    return ret + jnp.where(n < max_exact, n, large)


def attention_bias_inputs(table, gi, dilation):
    tq = ATTN_SUB
    tk = tq + 2 * ATTN_HALF
    rel = jnp.arange(tk)[None, :] - ATTN_HALF - jnp.arange(tq)[:, None]
    bucket = _t5_bucket(rel * dilation).astype(jnp.int32)
    tbl = table[:, gi * ATTN_SLOTS:(gi + 1) * ATTN_SLOTS].astype(F32).T.reshape(-1)
    return tbl, bucket


def dilated_attention(qkv, tbl, bucket, gi, *, tq):
    B, dilation, L, _ = qkv.shape
    nblk = L // tq
    hb = tq // ATTN_HALF
    nhalf = L // ATTN_HALF
    W = ATTN_OUT_DIM

    def main(off):
        return pl.BlockSpec((None, None, tq, W), lambda b, r, i: (b, r, i, off))

    def prev(off):
        return pl.BlockSpec((None, None, ATTN_HALF, W),
                            lambda b, r, i: (b, r, jnp.maximum(i * hb - 1, 0), off))

    def nxt(off):
        return pl.BlockSpec((None, None, ATTN_HALF, W),
                            lambda b, r, i: (b, r, jnp.minimum((i + 1) * hb, nhalf - 1), off))

    sub_shape = (ATTN_SUB, ATTN_SUB + 2 * ATTN_HALF)
    out_spec = pl.BlockSpec((None, None, tq, W), lambda b, r, i: (b, r, i, 0))
    return pl.pallas_call(
        functools.partial(_attn_kernel, nblk=nblk),
        grid=(B, dilation, nblk),
        in_specs=[pl.BlockSpec(memory_space=pltpu.SMEM),
                  pl.BlockSpec(sub_shape, lambda b, r, i: (0, 0)),
                  main(0), prev(1), main(1), nxt(1), prev(2), main(2), nxt(2)],
        out_specs=[out_spec, out_spec],
        out_shape=[jax.ShapeDtypeStruct((B, dilation, L, W), BF16),
                   jax.ShapeDtypeStruct((B, dilation, L, W), F32)],
        scratch_shapes=[pltpu.VMEM((4, ATTN_SLOTS) + sub_shape, F32)],
        compiler_params=_cparams("arbitrary", "arbitrary", "arbitrary"),
        name=f"dilated_attn_g{gi}",
    )(tbl, bucket, qkv, qkv, qkv, qkv, qkv, qkv, qkv)


def _branch_post_kernel(yf_ref, yb_ref, bonus_ref, g_ref, gnw_ref, gnb_ref,
                        o0, o1, o2, l0, l1, l2, orw_ref, oat_ref, *scratch):
    y = yf_ref[...].astype(F32) + yb_ref[...].astype(F32)
    ones_bd = _head_block_ones(2 * LANES)
    tm, width = y.shape

    def head_mean(x):
        return jnp.concatenate(
            [_head_sum(x[:, c:c + 2 * LANES], ones_bd) for c in range(0, width, 2 * LANES)],
            axis=1) * (1.0 / HEAD_DIM)

    mean = head_mean(y)
    yc = y - mean
    var = head_mean(yc * yc)
    yn = yc * lax.rsqrt(var + GN_EPS) * gnw_ref[...] + gnb_ref[...]
    orw_ref[...] = ((yn + bonus_ref[...].astype(F32)) * g_ref[...].astype(F32)).astype(BF16)

    def natural_order(ref, scr):
        d = ref.shape[0]
        for t in range(ATTN_OUT_DIM // LANES):
            for res in range(d):
                scr[t, pl.ds(res, tm // d, stride=d), :] = (
                    ref[res, :, t * LANES:(t + 1) * LANES].astype(F32))
        return jnp.concatenate([scr[t] for t in range(ATTN_OUT_DIM // LANES)], axis=1)

    s_o1, s_o2, s_l1, s_l2 = scratch
    oa, ob, oc = o0[0].astype(F32), natural_order(o1, s_o1), natural_order(o2, s_o2)
    la, lb, lc = l0[0], natural_order(l1, s_l1), natural_order(l2, s_l2)
    m = jnp.maximum(jnp.maximum(la, lb), lc)
    ea, eb, ec = jnp.exp(la - m), jnp.exp(lb - m), jnp.exp(lc - m)
    den = ea + eb + ec
    oat_ref[...] = ((ea * oa + eb * ob + ec * oc) / den).astype(BF16)


def branch_post(yf, yb, bonus, g, gn_w, gn_b, layer, outs, lses, *, tm):
    M, D = yf.shape
    A = ATTN_OUT_DIM
    batch = outs[0].shape[0]
    nb = M // batch // tm
    big = pl.BlockSpec((tm, D), lambda b, i: (b * nb + i, 0))
    rowp = pl.BlockSpec((None, 1, D), lambda b, i: (layer, 0, 0))
    attn_specs = [pl.BlockSpec((None, o.shape[1], tm // o.shape[1], A), lambda b, i: (b, 0, i, 0))
                  for o in outs]
    return pl.pallas_call(
        _branch_post_kernel,
        grid=(batch, nb),
        in_specs=[big] * 4 + [rowp, rowp] + attn_specs + attn_specs,
        out_specs=[big, pl.BlockSpec((tm, A), lambda b, i: (b * nb + i, 0))],
        out_shape=[jax.ShapeDtypeStruct((M, D), BF16), jax.ShapeDtypeStruct((M, A), BF16)],
        scratch_shapes=[pltpu.VMEM((A // LANES, tm, LANES), F32)] * 4,
        compiler_params=_cparams("parallel", "arbitrary"),
        name="branch_post",
    )(yf, yb, bonus, g, gn_w.reshape(-1, 1, D), gn_b.reshape(-1, 1, D), *outs, *lses)


def _merge_kernel(orw_ref, oat_ref, wr_ref, wa_ref, gr_ref, ga_ref, o_ref):
    a = jnp.dot(orw_ref[...], wr_ref[...].astype(BF16), preferred_element_type=F32)
    b = jnp.dot(oat_ref[...], wa_ref[...].astype(BF16), preferred_element_type=F32)
    o_ref[...] = (gr_ref[...].astype(F32) * a + ga_ref[...].astype(F32) * b).astype(o_ref.dtype)


def branch_merge(orw, oat, w_r, w_a, layer, gates, *, tm, tn):
    M = orw.shape[0]
    N = w_r.shape[2]
    nj = N // tn
    return pl.pallas_call(
        _merge_kernel,
        grid=(M // tm, nj),
        in_specs=[pl.BlockSpec((tm, orw.shape[1]), lambda i, j: (i, 0)),
                  pl.BlockSpec((tm, oat.shape[1]), lambda i, j: (i, 0)),
                  pl.BlockSpec((None, w_r.shape[1], tn), lambda i, j: (layer, 0, j)),
                  pl.BlockSpec((None, w_a.shape[1], tn), lambda i, j: (layer, 0, j)),
                  pl.BlockSpec((tm, tn), lambda i, j: (i, j)),
                  pl.BlockSpec((tm, tn), lambda i, j: (i, j + nj))],
        out_specs=pl.BlockSpec((tm, tn), lambda i, j: (i, j)),
        out_shape=jax.ShapeDtypeStruct((M, N), BF16),
        compiler_params=_cparams("parallel", "arbitrary"),
        name="branch_merge",
    )(orw, oat, w_r, w_a, gates, gates)


def _lora_weights(w_up, a_up):
    z = jnp.zeros_like(w_up[:, 0])
    rows = [jnp.concatenate([w_up[:, 0], z, z, z], axis=2),
            jnp.concatenate([z, w_up[:, 1], z, z], axis=2),
            jnp.concatenate([z, z, a_up[:, 0], z], axis=2),
            jnp.concatenate([z, z, z, a_up[:, 1]], axis=2)]
    return jnp.concatenate(rows, axis=1).astype(BF16)


def kernel(x, norm1_g, w_in, tshift_mu, w0, w_lora_up, a0, a_lora_up, g_lora_up, k_k, k_a, r_k,
           gn_w, gn_b, rel_bias, w_branch_rwkv, w_branch_attn, w_out, norm2_g, w_mlp_in, w_mlp_out,
           final_g):
    B, S, D = x.shape
    M = B * S
    depth = w_in.shape[0]
    c_rkv = 3 * RWKV_DIM
    c_slab = c_rkv + LORA_COLS
    c_attn = c_slab + 3 * ATTN_DIM
    w_qkv = w_in[:, :, c_slab:c_attn].astype(BF16)
    w_lora_up = _lora_weights(w_lora_up, a_lora_up)
    g_up = g_lora_up.astype(BF16)
    bias_inputs = [attention_bias_inputs(rel_bias, gi, dilation)
                   for gi, (_, dilation) in enumerate(ATTN_GROUPS)]
    h = x.reshape(M, D)
    for l in range(depth):
        g1 = norm1_g
        rkv = norm_matmul(h, g1, w_in, l, n_cols=c_rkv, tm=1024, tn=1024)
        lora = norm_matmul(h, g1, w_in, l, col0=c_rkv, n_cols=LORA_COLS, tm=1024, tn=LORA_COLS,
                           pieces=LORA_COLS // LANES)
        qkv_groups = qkv_proj(h, g1, w_qkv, l, batch=B, tm=1024)
        gates = norm_matmul(h, g1, w_in, l, col0=c_attn, n_cols=2 * D, tm=1024, tn=1024,
                            pieces=1024 // LANES, act="sigmoid", out_dtype=BF16)

        (r, v, kk, lwf, lwb, kaf, kab, kdf, kdb, g, bonus) = rwkv_prep(
            rkv.reshape(B, S, c_rkv), lora.reshape(B, S, LORA_COLS), l, tshift_mu, w0, a0,
            w_lora_up, g_up, k_k, k_a, r_k, ts=256)
        yf, yb = rwkv_scan(r, v, kk, lwf, lwb, kaf, kab, kdf, kdb)

        outs, lses = [], []
        for gi, (window, dilation) in enumerate(ATTN_GROUPS):
            assert window // (2 * dilation) == ATTN_HALF
            o, lse = dilated_attention(qkv_groups[gi], *bias_inputs[gi], gi,
                                       tq=min(S // dilation, ATTN_MAX_BLOCK))
            outs.append(o)
            lses.append(lse)

        orw, oat = branch_post(yf.reshape(M, RWKV_DIM), yb.reshape(M, RWKV_DIM),
                               bonus.reshape(M, RWKV_DIM), g.reshape(M, RWKV_DIM),
                               gn_w, gn_b, l, outs, lses, tm=512)
        merged = branch_merge(orw, oat, w_branch_rwkv, w_branch_attn, l, gates, tm=1024, tn=1024)
        h = matmul_residual(merged, w_out, l, h, tm=1024, tn=1024, tk=D)

        act = norm_matmul(h, norm2_g, w_mlp_in, l, n_cols=w_mlp_in.shape[2], tm=1024, tn=1024,
                          act="relu2", out_dtype=BF16)
        h = matmul_residual(act, w_mlp_out, l, h, tm=1024, tn=1024, tk=2048)
    out = rmsnorm_rows(h, final_g, tm=512)
    return out.reshape(B, S, D)
```

```python
import functools
import math

import jax
import jax.numpy as jnp
from jax import lax
from jax.experimental import pallas as pl
from jax.experimental.pallas import tpu as pltpu

F32 = jnp.float32
BF16 = jnp.bfloat16

HEAD_DIM = 64
LANES = 128
RWKV_DIM = 1024
DECAY_LORA = 96
AAA_LORA = 96
GATE_LORA = 256
LORA_COLS = GATE_LORA + 2 * DECAY_LORA + 2 * AAA_LORA
LORA_TAIL = LORA_COLS - GATE_LORA
ATTN_GROUPS = ((128, 1), (512, 4), (2048, 16))
ATTN_SLOTS = 8
ATTN_DIM = 1536
ATTN_OUT_DIM = ATTN_SLOTS * HEAD_DIM
ATTN_HALF = 64
ATTN_SUB = 128
ATTN_MAX_BLOCK = 1024
N_BUCKETS = 32
MAX_DISTANCE = 1024
RMS_EPS = 1e-6
GN_EPS = 64e-5
L2_EPS = 1e-12
NEG_BIG = -1e30
CHUNK = 64
PACK_DIR = 5
PACK_G = 2 * PACK_DIR
PACK_BONUS = PACK_G + 1
PACK_SLOTS = PACK_BONUS + 1
VMEM_LIMIT = 56 * 1024 * 1024


def _cparams(*sem):
    return pltpu.CompilerParams(dimension_semantics=sem, vmem_limit_bytes=VMEM_LIMIT)


def _dot(a, b):
    return jnp.dot(a.astype(BF16), b.astype(BF16), preferred_element_type=F32)


def _split3(x):
    p0 = x.astype(BF16)
    r1 = x - p0.astype(F32)
    p1 = r1.astype(BF16)
    p2 = (r1 - p1.astype(F32)).astype(BF16)
    return p0, p1, p2


def _dot_exact_rhs(a_bf16, x):
    return sum(jnp.dot(a_bf16, p, preferred_element_type=F32) for p in _split3(x))


def _head_sum(x, ones_bd):
    hi = x.astype(BF16)
    lo = (x - hi.astype(F32)).astype(BF16)
    return (jnp.dot(hi, ones_bd, preferred_element_type=F32)
            + jnp.dot(lo, ones_bd, preferred_element_type=F32))


def _head_block_ones(width):
    r = lax.broadcasted_iota(jnp.int32, (width, width), 0) // HEAD_DIM
    c = lax.broadcasted_iota(jnp.int32, (width, width), 1) // HEAD_DIM
    return (r == c).astype(BF16)


def _normalise_rows(x_ref, g_ref, u_ref, start, size):
    rows = pl.ds(start, size)
    x = x_ref[rows, :]
    ms = jnp.mean(x * x, axis=-1, keepdims=True)
    u_ref[rows, :] = (x * lax.rsqrt(ms + RMS_EPS) * g_ref[...]).astype(BF16)


def _x_rows_index(i, j, ni, nj):
    if nj == 1:
        return i
    return jnp.minimum(i + jnp.where(j >= 1, 1, 0), ni - 1)


BF16_ROWS = 16


def _with_normalised_tile(x_ref, g_ref, u_refs, nj, body):
    i, j = pl.program_id(0), pl.program_id(1)
    tm = x_ref.shape[0]
    if nj == 1:
        _normalise_rows(x_ref, g_ref, u_refs[0], 0, tm)
        body(u_refs[0])
        return

    @pl.when((i == 0) & (j == 0))
    def _():
        _normalise_rows(x_ref, g_ref, u_refs[0], 0, tm)

    size = -(-tm // (nj - 1))
    size = -(-size // BF16_ROWS) * BF16_ROWS
    start = pl.multiple_of(jnp.clip((j - 1) * size, 0, tm - size), BF16_ROWS)
    for parity in range(2):
        @pl.when(i % 2 == parity)
        def _():
            body(u_refs[parity])
            _normalise_rows(x_ref, g_ref, u_refs[1 - parity], start, size)


def _norm_mm_kernel(x_ref, g_ref, *refs, act, nj):
    *w_refs, o_ref, u0_ref, u1_ref = refs

    def body(u_ref):
        w = jnp.concatenate([w_ref[...].astype(BF16) for w_ref in w_refs], axis=1)
        acc = jnp.dot(u_ref[...], w, preferred_element_type=F32)
        if act == "relu2":
            acc = jnp.square(jnp.maximum(acc, 0.0))
        elif act == "sigmoid":
            acc = 0.5 * jnp.tanh(0.5 * acc) + 0.5
        o_ref[...] = acc.astype(o_ref.dtype)

    _with_normalised_tile(x_ref, g_ref, (u0_ref, u1_ref), nj, body)


def norm_matmul(x, g, w, layer, *, col0=0, n_cols, tm, tn, pieces=1, act=None, out_dtype=F32):
    M, K = x.shape
    pw = tn // pieces
    assert pw * pieces == tn and col0 % pw == 0 and n_cols % tn == 0
    w_specs = [pl.BlockSpec((None, K, pw), functools.partial(
        lambda i, j, t: (layer, 0, col0 // pw + j * pieces + t), t=t)) for t in range(pieces)]
    ni, nj = M // tm, n_cols // tn
    return pl.pallas_call(
        functools.partial(_norm_mm_kernel, act=act, nj=nj),
        grid=(ni, nj),
        in_specs=[pl.BlockSpec((tm, K), lambda i, j: (_x_rows_index(i, j, ni, nj), 0)),
                  pl.BlockSpec((None, 1, K), lambda i, j: (layer, 0, 0))] + w_specs,
        out_specs=pl.BlockSpec((tm, tn), lambda i, j: (i, j)),
        out_shape=jax.ShapeDtypeStruct((M, n_cols), out_dtype),
        scratch_shapes=[pltpu.VMEM((tm, K), BF16)] * 2,
        compiler_params=_cparams("arbitrary", "arbitrary"),
        name="norm_matmul_" + (act or "id"),
    )(x, g.reshape(-1, 1, K), *([w] * pieces))


def _qkv_proj_kernel(x_ref, g_ref, w_ref, o0_ref, o1_ref, o2_ref, u_ref, acc_ref):
    tm = x_ref.shape[0]

    @pl.when(pl.program_id(1) == 0)
    def _():
        _normalise_rows(x_ref, g_ref, u_ref, 0, tm)

    acc = jnp.dot(u_ref[...], w_ref[...], preferred_element_type=F32)
    acc = acc * jnp.where(pl.program_id(1) == 0, HEAD_DIM ** -0.5, 1.0)
    tiles_per_group = ATTN_OUT_DIM // LANES
    for gi, (o_ref, (_, d)) in enumerate(zip((o0_ref, o1_ref, o2_ref), ATTN_GROUPS)):
        if d == 1:
            o_ref[0, 0] = acc[:, gi * ATTN_OUT_DIM:(gi + 1) * ATTN_OUT_DIM].astype(BF16)
            continue
        for t in range(tiles_per_group):
            c = gi * tiles_per_group + t
            acc_ref[c] = acc[:, c * LANES:(c + 1) * LANES]
            for res in range(d):
                o_ref[0, res, :, t * LANES:(t + 1) * LANES] = (
                    acc_ref[c, pl.ds(res, tm // d, stride=d), :].astype(BF16))


def qkv_proj(x, g, w, layer, *, batch, tm):
    M, K = x.shape
    S = M // batch
    nb = S // tm
    outs_shape, outs_spec = [], []
    for _, d in ATTN_GROUPS:
        outs_shape.append(jax.ShapeDtypeStruct((batch, d, S // d, 3 * ATTN_OUT_DIM), BF16))
        outs_spec.append(pl.BlockSpec((1, d, tm // d, ATTN_OUT_DIM),
                                      lambda i, j: (i // nb, 0, i % nb, j)))
    return pl.pallas_call(
        _qkv_proj_kernel,
        grid=(M // tm, 3),
        in_specs=[pl.BlockSpec((tm, K), lambda i, j: (i, 0)),
                  pl.BlockSpec((None, 1, K), lambda i, j: (layer, 0, 0)),
                  pl.BlockSpec((None, K, ATTN_DIM), lambda i, j: (layer, 0, j))],
        out_specs=outs_spec,
        out_shape=outs_shape,
        scratch_shapes=[pltpu.VMEM((tm, K), BF16), pltpu.VMEM((ATTN_DIM // LANES, tm, LANES), F32)],
        compiler_params=_cparams("parallel", "arbitrary"),
        name="qkv_proj",
    )(x, g.reshape(-1, 1, K), w)


def _mm_res_kernel(x_ref, w_ref, r_ref, o_ref, acc_ref):
    k = pl.program_id(2)

    @pl.when(k == 0)
    def _():
        acc_ref[...] = jnp.zeros_like(acc_ref)

    acc_ref[...] += jnp.dot(x_ref[...], w_ref[...].astype(BF16), preferred_element_type=F32)

    @pl.when(k == pl.num_programs(2) - 1)
    def _():
        o_ref[...] = r_ref[...] + acc_ref[...]


def matmul_residual(x, w, layer, res, *, tm, tn, tk):
    M, K = x.shape
    N = w.shape[2]
    return pl.pallas_call(
        _mm_res_kernel,
        grid=(M // tm, N // tn, K // tk),
        in_specs=[pl.BlockSpec((tm, tk), lambda i, j, k: (i, k)),
                  pl.BlockSpec((None, tk, tn), lambda i, j, k: (layer, k, j)),
                  pl.BlockSpec((tm, tn), lambda i, j, k: (i, j))],
        out_specs=pl.BlockSpec((tm, tn), lambda i, j, k: (i, j)),
        out_shape=jax.ShapeDtypeStruct((M, N), F32),
        scratch_shapes=[pltpu.VMEM((tm, tn), F32)],
        compiler_params=_cparams("parallel", "parallel", "arbitrary"),
        name="matmul_residual",
    )(x, w, res)


def _rmsnorm_kernel(x_ref, g_ref, o_ref):
    x = x_ref[...]
    ms = jnp.mean(x * x, axis=-1, keepdims=True)
    o_ref[...] = x * lax.rsqrt(ms + RMS_EPS) * g_ref[...]


def rmsnorm_rows(x, g, *, tm):
    M, K = x.shape
    return pl.pallas_call(
        _rmsnorm_kernel,
        grid=(M // tm,),
        in_specs=[pl.BlockSpec((tm, K), lambda i: (i, 0)),
                  pl.BlockSpec((1, K), lambda i: (0, 0))],
        out_specs=pl.BlockSpec((tm, K), lambda i: (i, 0)),
        out_shape=jax.ShapeDtypeStruct((M, K), F32),
        compiler_params=_cparams("parallel"),
        name="final_rmsnorm",
    )(x, g.reshape(1, K))


def _token_shift(main, prev_blk, next_blk, mu, first, last):
    rows = main.shape[0]
    prev_row = jnp.where(first, 0.0, prev_blk[7:8, :])
    next_row = jnp.where(last, 0.0, next_blk[0:1, :])
    rid = lax.broadcasted_iota(jnp.int32, main.shape, 0)
    up = jnp.where(rid == 0, prev_row, pltpu.roll(main, 1, 0))
    dn = jnp.where(rid == rows - 1, next_row, pltpu.roll(main, rows - 1, 0))
    return main * (1.0 - mu) + (0.5 * mu) * (up + dn)


def _rwkv_prep_kernel(rkv_m, rkv_p, rkv_n, l_m, l_p, l_n,
                      mu_ref, w0_ref, a0_ref, lora_w_ref, g_up_ref, kk_ref, ka_ref, rk_ref,
                      pack_o, lwf_o, lwb_o):
    i = pl.program_id(1)

    def put(slot, val):
        pack_o[0, :, slot * RWKV_DIM:(slot + 1) * RWKV_DIM] = val.astype(pack_o.dtype)

    first = i == 0
    last = i == pl.num_programs(1) - 1
    D = RWKV_DIM
    r, k, v = (_token_shift(rkv_m[0, :, c * D:(c + 1) * D], rkv_p[0, :, c * D:(c + 1) * D],
                            rkv_n[0, :, c * D:(c + 1) * D], mu_ref[:, c * D:(c + 1) * D], first, last)
               for c in range(3))
    lo = _token_shift(l_m[0], l_p[0], l_n[0], mu_ref[:, 3 * D:], first, last)

    g = _dot(jax.nn.sigmoid(lo[:, :GATE_LORA]), g_up_ref[...])
    tail = lo[:, GATE_LORA:]
    col = lax.broadcasted_iota(jnp.int32, tail.shape, 1)
    tail = jnp.where(col < 2 * DECAY_LORA, jnp.tanh(tail), tail)
    up = _dot(tail, lora_w_ref[...])

    ones_bd = _head_block_ones(2 * LANES)

    def head_sum(x):
        return jnp.concatenate(
            [_head_sum(x[:, c:c + 2 * LANES], ones_bd) for c in range(0, RWKV_DIM, 2 * LANES)], axis=1)

    kk = k * kk_ref[...]
    kk = kk * jnp.minimum(lax.rsqrt(head_sum(kk * kk)), 1.0 / L2_EPS)

    for d in range(2):
        put(PACK_DIR * d + 0, r)
        put(PACK_DIR * d + 1, v)
        put(PACK_DIR * d + 2, kk)
    put(PACK_G, g)
    kd_sum = jnp.zeros_like(k)
    for d, lw_o in enumerate((lwf_o, lwb_o)):
        z = w0_ref[d:d + 1, :] + up[:, d * RWKV_DIM:(d + 1) * RWKV_DIM]
        nz = -z
        softplus = jnp.maximum(nz, 0.0) + jnp.log(1.0 + jnp.exp(-jnp.abs(nz)))
        lw_o[0] = -jnp.exp(-softplus - 0.5)
        a = jax.nn.sigmoid(a0_ref[d:d + 1, :] + up[:, (2 + d) * RWKV_DIM:(3 + d) * RWKV_DIM])
        kd = k * (1.0 + (a - 1.0) * ka_ref[...])
        put(PACK_DIR * d + 3, kk * a)
        put(PACK_DIR * d + 4, kd)
        kd_sum = kd_sum + kd
    put(PACK_BONUS, head_sum(r * kd_sum * rk_ref[...]) * v)


def rwkv_prep(rkv, lora, layer, mu, w0, a0, lora_w, g_up, k_k, k_a, r_k, *, ts):
    B, S, _ = rkv.shape
    D = RWKV_DIM
    nblk8 = S // 8

    def main_spec(width, cblk):
        return pl.BlockSpec((1, ts, width), lambda b, i: (b, i, cblk))

    def prev_spec(width, cblk):
        return pl.BlockSpec((1, 8, width), lambda b, i: (b, jnp.maximum(i * (ts // 8) - 1, 0), cblk))

    def next_spec(width, cblk):
        return pl.BlockSpec((1, 8, width), lambda b, i: (b, jnp.minimum((i + 1) * (ts // 8), nblk8 - 1), cblk))

    def layer_spec(rows, width):
        return pl.BlockSpec((None, rows, width), lambda b, i: (layer, 0, 0))

    in_specs = []
    args = []
    in_specs += [main_spec(3 * D, 0), prev_spec(3 * D, 0), next_spec(3 * D, 0)]
    args += [rkv, rkv, rkv]
    in_specs += [main_spec(LORA_COLS, 0), prev_spec(LORA_COLS, 0), next_spec(LORA_COLS, 0)]
    args += [lora, lora, lora]
    depth = mu.shape[0]
    in_specs += [layer_spec(1, 3 * D + LORA_COLS), layer_spec(2, D), layer_spec(2, D),
                 layer_spec(LORA_TAIL, 4 * D), layer_spec(GATE_LORA, D),
                 layer_spec(1, D), layer_spec(1, D), layer_spec(1, D)]
    args += [mu.reshape(depth, 1, -1), w0, a0, lora_w, g_up,
             k_k.reshape(depth, 1, D), k_a.reshape(depth, 1, D), r_k.reshape(depth, 1, D)]
    out_spec = pl.BlockSpec((1, ts, D), lambda b, i: (b, i, 0))
    return pl.pallas_call(
        _rwkv_prep_kernel,
        grid=(B, S // ts),
        in_specs=in_specs,
        out_specs=[pl.BlockSpec((1, ts, PACK_SLOTS * D), lambda b, i: (b, i, 0)), out_spec, out_spec],
        out_shape=[jax.ShapeDtypeStruct((B, S, PACK_SLOTS * D), BF16),
                   jax.ShapeDtypeStruct((B, S, D), F32), jax.ShapeDtypeStruct((B, S, D), F32)],
        compiler_params=_cparams("parallel", "arbitrary"),
        name="rwkv_prep",
    )(*args)


_BMM = (((2,), (1,)), ((0,), (0,)))
_BMM_NT = (((2,), (2,)), ((0,), (0,)))
_BMM_TN = (((1,), (1,)), ((0,), (0,)))


def _bdot(a, b, dims=_BMM):
    return lax.dot_general(a.astype(BF16), b.astype(BF16), dims, preferred_element_type=F32)


def _bd_stack(x, m0):
    zero = jnp.zeros_like(x)
    return jnp.concatenate([jnp.where(m0, x, zero), jnp.where(m0, zero, x)], axis=-2)


def _scan_chunk(r, v, kk, lw, cum, ka, kd, s_bd, consts):
    is_rev, strict2, incl2, eye2, m0_1, m0_2, bd_mask, off_masks = consts
    C = r.shape[1]
    cum_prev = cum - lw
    e_cum = jnp.exp(cum)
    e_prev = jnp.exp(cum_prev)
    e_neg = jnp.exp(-cum)
    total = jnp.where(is_rev, cum[:, 0:1, :], cum[:, C - 1:C, :])
    e_rest = jnp.exp(total - cum)
    r0 = r * e_cum
    at0 = -kk * e_prev
    bt = ka * e_neg
    kt = kd * e_neg
    bh = ka * e_rest
    kh = kd * e_rest

    gram = _bdot(jnp.concatenate([at0, r0], axis=1),
                 jnp.concatenate([_bd_stack(bt, m0_1), _bd_stack(kt, m0_1)], axis=1), _BMM_NT)
    zero = jnp.zeros_like(gram[:, :C])
    a_top = jnp.where(strict2, gram[:, :C], zero)
    a_bot = jnp.where(incl2, gram[:, C:], zero)
    a_ab = a_top[:, :, :LANES]
    a_ak = a_top[:, :, LANES:]

    zero1 = jnp.zeros_like(a_ab)
    t = eye2 + jnp.where(off_masks[0], a_ab, zero1)
    for off_mask in off_masks[1:]:
        z = _bdot(t, _bd_stack(jnp.where(off_mask, a_ab, zero1), m0_1))
        t = t + _bdot(z, _bd_stack(t, m0_1))

    av = _bdot(a_ak, _bd_stack(v, m0_1))
    wu = _bdot(t, _bd_stack(jnp.concatenate([at0, av], axis=2), m0_2))
    w_a = wu[:, :, :LANES]
    u_v = wu[:, :, LANES:]

    wr = _bdot(jnp.concatenate([w_a, r0], axis=1), s_bd, _BMM_NT)
    u = wr[:, :C] + u_v
    y = wr[:, C:] + _bdot(a_bot, jnp.concatenate([_bd_stack(u, m0_1), _bd_stack(v, m0_1)], axis=1))
    upd = _bdot(jnp.concatenate([u, v], axis=1), jnp.concatenate([bh, kh], axis=1), _BMM_TN)
    s_new = s_bd * jnp.exp(total) + jnp.where(bd_mask, upd, jnp.zeros_like(upd))
    return y, s_new


def _scan_consts(C, n_fwd, n_bwd):
    G = n_fwd + n_bwd
    is_rev = lax.broadcasted_iota(jnp.int32, (G, 1, 1), 0) >= n_fwd
    sign = jnp.where(is_rev, -1, 1)
    row2 = lax.broadcasted_iota(jnp.int32, (G, C, 2 * LANES), 1)
    col2 = lax.broadcasted_iota(jnp.int32, (G, C, 2 * LANES), 2) % HEAD_DIM
    strict2 = (col2 - row2) * sign < 0
    incl2 = (col2 - row2) * sign <= 0
    lane1 = lax.broadcasted_iota(jnp.int32, (1, C, LANES), 2)
    lane2 = lax.broadcasted_iota(jnp.int32, (1, C, 2 * LANES), 2) % LANES
    m0_1 = lane1 < HEAD_DIM
    m0_2 = lane2 < HEAD_DIM
    rb = lax.broadcasted_iota(jnp.int32, (1, LANES, LANES), 1) // HEAD_DIM
    cb = lax.broadcasted_iota(jnp.int32, (1, LANES, LANES), 2) // HEAD_DIM
    bd_mask = rb == cb
    row1 = lax.broadcasted_iota(jnp.int32, (1, C, LANES), 1)
    col1 = lane1 % HEAD_DIM
    eye2 = (row1 == col1).astype(F32)
    off_masks = [row1 // 2 == col1 // 2]
    b = 2
    while b < C:
        off_masks.append((row1 // (2 * b) == col1 // (2 * b)) & (row1 // b != col1 // b))
        b *= 2
    return (is_rev, strict2, incl2, eye2, m0_1, m0_2, bd_mask, off_masks)


def _rwkv_scan_kernel(pack_f, pack_b, lwf, lwb, yf_o, yb_o, state_ref):
    c = pl.program_id(0)

    @pl.when(c == 0)
    def _():
        state_ref[...] = jnp.zeros_like(state_ref)

    nb, C, D = lwf.shape
    n_pairs = D // LANES
    n_dir = nb * n_pairs
    consts = _scan_consts(C, n_dir, n_dir)

    row = lax.broadcasted_iota(jnp.int32, (C, C), 0)
    col = lax.broadcasted_iota(jnp.int32, (C, C), 1)
    tri_f = (col <= row).astype(F32).astype(BF16)
    tri_b = (col >= row).astype(F32).astype(BF16)
    cum_f = [_dot_exact_rhs(tri_f, lwf[b]) for b in range(nb)]
    cum_b = [_dot_exact_rhs(tri_b, lwb[b]) for b in range(nb)]

    def chains(xs_f, xs_b):
        return jnp.stack([x[:, p * LANES:(p + 1) * LANES]
                          for xs in (xs_f, xs_b) for x in xs for p in range(n_pairs)], axis=0)

    def slot(ref, k):
        return [ref[b, :, k * D:(k + 1) * D].astype(F32) for b in range(nb)]

    r, v, kk, ka, kd = (chains(slot(pack_f, k), slot(pack_b, k)) for k in range(PACK_DIR))
    lw = chains([lwf[b] for b in range(nb)], [lwb[b] for b in range(nb)])
    y, s_new = _scan_chunk(r, v, kk, lw, chains(cum_f, cum_b), ka, kd, state_ref[...], consts)
    state_ref[...] = s_new
    for d, y_o in enumerate((yf_o, yb_o)):
        for b in range(nb):
            for p in range(n_pairs):
                y_o[b, :, p * LANES:(p + 1) * LANES] = y[(d * nb + b) * n_pairs + p].astype(y_o.dtype)


def rwkv_scan(pack, lwf, lwb):
    B, S, D = lwf.shape
    C = CHUNK
    nc = S // C
    fwd = pl.BlockSpec((B, C, D), lambda c: (0, c, 0))
    bwd = pl.BlockSpec((B, C, D), lambda c: (0, nc - 1 - c, 0))
    return pl.pallas_call(
        _rwkv_scan_kernel,
        grid=(nc,),
        in_specs=[pl.BlockSpec((B, C, PACK_DIR * D), lambda c: (0, c, 0)),
                  pl.BlockSpec((B, C, PACK_DIR * D), lambda c: (0, nc - 1 - c, 1)), fwd, bwd],
        out_specs=[fwd, bwd],
        out_shape=[jax.ShapeDtypeStruct((B, S, D), BF16)] * 2,
        scratch_shapes=[pltpu.VMEM((2 * B * (D // LANES), LANES, LANES), F32)],
        compiler_params=_cparams("arbitrary"),
        name="rwkv_scan",
    )(pack, pack, lwf, lwb)


def _attn_kernel(table_ref, bucket_ref, q_ref, kp_ref, km_ref, kn_ref, vp_ref, vm_ref, vn_ref,
                 o_ref, lse_ref, bias_ref, *, nblk):
    i = pl.program_id(2)
    n_sub = q_ref.shape[0] // ATTN_SUB
    TQ = ATTN_SUB
    TK = TQ + 2 * ATTN_HALF

    @pl.when((pl.program_id(0) == 0) & (pl.program_id(1) == 0) & (i == 0))
    def _():
        bucket = bucket_ref[...]
        row = lax.broadcasted_iota(jnp.int32, (TQ, TK), 0)
        colk = lax.broadcasted_iota(jnp.int32, (TQ, TK), 1)
        band = jnp.abs(colk - ATTN_HALF - row) <= ATTN_HALF
        has_prev = colk >= ATTN_HALF
        has_next = colk < TQ + ATTN_HALF
        for h in range(ATTN_SLOTS):
            acc = jnp.zeros((TQ, TK), F32)
            for b in range(N_BUCKETS):
                acc = jnp.where(bucket == b, table_ref[h * N_BUCKETS + b], acc)
            acc = jnp.where(band, acc, NEG_BIG)
            bias_ref[0, h] = acc
            bias_ref[1, h] = jnp.where(has_prev, acc, NEG_BIG)
            bias_ref[2, h] = jnp.where(has_next, acc, NEG_BIG)
            bias_ref[3, h] = jnp.where(has_prev & has_next, acc, NEG_BIG)

    m0 = lax.broadcasted_iota(jnp.int32, (TQ, LANES), 1) < HEAD_DIM
    lane = lax.broadcasted_iota(jnp.int32, (1, LANES), 1)
    head_keep = [(lane < HEAD_DIM).astype(F32).astype(BF16), (lane >= HEAD_DIM).astype(F32).astype(BF16)]
    for p in range(ATTN_OUT_DIM // LANES):
        sl = slice(p * LANES, (p + 1) * LANES)
        kfull = jnp.concatenate([kp_ref[:, sl], km_ref[:, sl], kn_ref[:, sl]], axis=0)
        vfull = jnp.concatenate([vp_ref[:, sl], vm_ref[:, sl], vn_ref[:, sl]], axis=0)
        for sb in range(n_sub):
            rows = slice(sb * TQ, (sb + 1) * TQ)
            first = (i == 0) if sb == 0 else False
            last = (i == nblk - 1) if sb == n_sub - 1 else False
            variant = jnp.where(first, 1, 0) + jnp.where(last, 2, 0)
            q2 = q_ref[rows, sl]
            kwin = kfull[sb * TQ:sb * TQ + TK]
            vwin = vfull[sb * TQ:sb * TQ + TK]
            outs, lses = [], []
            for hh in range(2):
                qm = q2 * head_keep[hh]
                s = lax.dot_general(qm, kwin, (((1,), (1,)), ((), ())), preferred_element_type=F32)
                s = s + bias_ref[variant, 2 * p + hh]
                m = jnp.max(s, axis=-1, keepdims=True)
                e = jnp.exp(s - m)
                den = jnp.sum(e, axis=-1, keepdims=True)
                pv = jnp.dot(e.astype(BF16), vwin, preferred_element_type=F32)
                outs.append(pv / den)
                lses.append(jnp.broadcast_to(m + jnp.log(den), (TQ, LANES)))
            o_ref[rows, sl] = jnp.where(m0, outs[0], outs[1]).astype(o_ref.dtype)
            lse_ref[rows, sl] = jnp.where(m0, lses[0], lses[1])


def _t5_bucket(rel):
    nb = N_BUCKETS // 2
    max_exact = nb // 2
    ret = jnp.where(rel > 0, nb, 0)
    n = jnp.abs(rel)
    nf = jnp.maximum(n, 1).astype(jnp.float32)
    large = max_exact + (jnp.log(nf / max_exact) / math.log(MAX_DISTANCE / max_exact)
                         * (nb - max_exact)).astype(jnp.int32)
    large = jnp.minimum(large, nb - 1)
    return ret + jnp.where(n < max_exact, n, large)


def attention_bias_inputs(table, gi, dilation):
    tq = ATTN_SUB
    tk = tq + 2 * ATTN_HALF
    rel = jnp.arange(tk)[None, :] - ATTN_HALF - jnp.arange(tq)[:, None]
    bucket = _t5_bucket(rel * dilation).astype(jnp.int32)
    tbl = table[:, gi * ATTN_SLOTS:(gi + 1) * ATTN_SLOTS].astype(F32).T.reshape(-1)
    return tbl, bucket


def dilated_attention(qkv, tbl, bucket, gi, *, tq):
    B, dilation, L, _ = qkv.shape
    nblk = L // tq
    hb = tq // ATTN_HALF
    nhalf = L // ATTN_HALF
    W = ATTN_OUT_DIM

    def main(off):
        return pl.BlockSpec((None, None, tq, W), lambda b, r, i: (b, r, i, off))

    def prev(off):
        return pl.BlockSpec((None, None, ATTN_HALF, W),
                            lambda b, r, i: (b, r, jnp.maximum(i * hb - 1, 0), off))

    def nxt(off):
        return pl.BlockSpec((None, None, ATTN_HALF, W),
                            lambda b, r, i: (b, r, jnp.minimum((i + 1) * hb, nhalf - 1), off))

    sub_shape = (ATTN_SUB, ATTN_SUB + 2 * ATTN_HALF)
    out_spec = pl.BlockSpec((None, None, tq, W), lambda b, r, i: (b, r, i, 0))
    return pl.pallas_call(
        functools.partial(_attn_kernel, nblk=nblk),
        grid=(B, dilation, nblk),
        in_specs=[pl.BlockSpec(memory_space=pltpu.SMEM),
                  pl.BlockSpec(sub_shape, lambda b, r, i: (0, 0)),
                  main(0), prev(1), main(1), nxt(1), prev(2), main(2), nxt(2)],
        out_specs=[out_spec, out_spec],
        out_shape=[jax.ShapeDtypeStruct((B, dilation, L, W), BF16),
                   jax.ShapeDtypeStruct((B, dilation, L, W), F32)],
        scratch_shapes=[pltpu.VMEM((4, ATTN_SLOTS) + sub_shape, F32)],
        compiler_params=_cparams("arbitrary", "arbitrary", "arbitrary"),
        name=f"dilated_attn_g{gi}",
    )(tbl, bucket, qkv, qkv, qkv, qkv, qkv, qkv, qkv)


def _branch_post_kernel(yf_ref, yb_ref, bonus_ref, g_ref, gnw_ref, gnb_ref,
                        o0, o1, o2, l0, l1, l2, orw_ref, oat_ref, *scratch):
    y = yf_ref[...].astype(F32) + yb_ref[...].astype(F32)
    ones_bd = _head_block_ones(2 * LANES)
    tm, width = y.shape

    def head_mean(x):
        return jnp.concatenate(
            [_head_sum(x[:, c:c + 2 * LANES], ones_bd) for c in range(0, width, 2 * LANES)],
            axis=1) * (1.0 / HEAD_DIM)

    mean = head_mean(y)
    yc = y - mean
    var = head_mean(yc * yc)
    yn = yc * lax.rsqrt(var + GN_EPS) * gnw_ref[...] + gnb_ref[...]
    orw_ref[...] = ((yn + bonus_ref[...].astype(F32)) * g_ref[...].astype(F32)).astype(BF16)

    def natural_order(ref, scr):
        d = ref.shape[0]
        for t in range(ATTN_OUT_DIM // LANES):
            for res in range(d):
                scr[t, pl.ds(res, tm // d, stride=d), :] = (
                    ref[res, :, t * LANES:(t + 1) * LANES].astype(F32))
        return jnp.concatenate([scr[t] for t in range(ATTN_OUT_DIM // LANES)], axis=1)

    s_o1, s_o2, s_l1, s_l2 = scratch
    oa, ob, oc = o0[0].astype(F32), natural_order(o1, s_o1), natural_order(o2, s_o2)
    la, lb, lc = l0[0], natural_order(l1, s_l1), natural_order(l2, s_l2)
    m = jnp.maximum(jnp.maximum(la, lb), lc)
    ea, eb, ec = jnp.exp(la - m), jnp.exp(lb - m), jnp.exp(lc - m)
    den = ea + eb + ec
    oat_ref[...] = ((ea * oa + eb * ob + ec * oc) / den).astype(BF16)


def branch_post(yf, yb, pack, gn_w, gn_b, layer, outs, lses, *, tm):
    M, D = yf.shape
    A = ATTN_OUT_DIM
    batch = outs[0].shape[0]
    nb = M // batch // tm
    big = pl.BlockSpec((tm, D), lambda b, i: (b * nb + i, 0))
    bonus = pl.BlockSpec((tm, D), lambda b, i: (b * nb + i, PACK_BONUS))
    g = pl.BlockSpec((tm, D), lambda b, i: (b * nb + i, PACK_G))
    rowp = pl.BlockSpec((None, 1, D), lambda b, i: (layer, 0, 0))
    attn_specs = [pl.BlockSpec((None, o.shape[1], tm // o.shape[1], A), lambda b, i: (b, 0, i, 0))
                  for o in outs]
    return pl.pallas_call(
        _branch_post_kernel,
        grid=(batch, nb),
        in_specs=[big, big, bonus, g, rowp, rowp] + attn_specs + attn_specs,
        out_specs=[big, pl.BlockSpec((tm, A), lambda b, i: (b * nb + i, 0))],
        out_shape=[jax.ShapeDtypeStruct((M, D), BF16), jax.ShapeDtypeStruct((M, A), BF16)],
        scratch_shapes=[pltpu.VMEM((A // LANES, tm, LANES), F32)] * 4,
        compiler_params=_cparams("parallel", "arbitrary"),
        name="branch_post",
    )(yf, yb, pack, pack, gn_w.reshape(-1, 1, D), gn_b.reshape(-1, 1, D), *outs, *lses)


def _merge_kernel(orw_ref, oat_ref, wr_ref, wa_ref, gr_ref, ga_ref, o_ref):
    a = jnp.dot(orw_ref[...], wr_ref[...].astype(BF16), preferred_element_type=F32)
    b = jnp.dot(oat_ref[...], wa_ref[...].astype(BF16), preferred_element_type=F32)
    o_ref[...] = (gr_ref[...].astype(F32) * a + ga_ref[...].astype(F32) * b).astype(o_ref.dtype)


def branch_merge(orw, oat, w_r, w_a, layer, gates, *, tm, tn):
    M = orw.shape[0]
    N = w_r.shape[2]
    nj = N // tn
    return pl.pallas_call(
        _merge_kernel,
        grid=(M // tm, nj),
        in_specs=[pl.BlockSpec((tm, orw.shape[1]), lambda i, j: (i, 0)),
                  pl.BlockSpec((tm, oat.shape[1]), lambda i, j: (i, 0)),
                  pl.BlockSpec((None, w_r.shape[1], tn), lambda i, j: (layer, 0, j)),
                  pl.BlockSpec((None, w_a.shape[1], tn), lambda i, j: (layer, 0, j)),
                  pl.BlockSpec((tm, tn), lambda i, j: (i, j)),
                  pl.BlockSpec((tm, tn), lambda i, j: (i, j + nj))],
        out_specs=pl.BlockSpec((tm, tn), lambda i, j: (i, j)),
        out_shape=jax.ShapeDtypeStruct((M, N), BF16),
        compiler_params=_cparams("parallel", "arbitrary"),
        name="branch_merge",
    )(orw, oat, w_r, w_a, gates, gates)


def _lora_weights(w_up, a_up):
    z = jnp.zeros_like(w_up[:, 0])
    rows = [jnp.concatenate([w_up[:, 0], z, z, z], axis=2),
            jnp.concatenate([z, w_up[:, 1], z, z], axis=2),
            jnp.concatenate([z, z, a_up[:, 0], z], axis=2),
            jnp.concatenate([z, z, z, a_up[:, 1]], axis=2)]
    return jnp.concatenate(rows, axis=1).astype(BF16)


def kernel(x, norm1_g, w_in, tshift_mu, w0, w_lora_up, a0, a_lora_up, g_lora_up, k_k, k_a, r_k,
           gn_w, gn_b, rel_bias, w_branch_rwkv, w_branch_attn, w_out, norm2_g, w_mlp_in, w_mlp_out,
           final_g):
    B, S, D = x.shape
    M = B * S
    depth = w_in.shape[0]
    c_rkv = 3 * RWKV_DIM
    c_slab = c_rkv + LORA_COLS
    c_attn = c_slab + 3 * ATTN_DIM
    w_qkv = w_in[:, :, c_slab:c_attn].astype(BF16)
    w_lora_up = _lora_weights(w_lora_up, a_lora_up)
    g_up = g_lora_up.astype(BF16)
    bias_inputs = [attention_bias_inputs(rel_bias, gi, dilation)
                   for gi, (_, dilation) in enumerate(ATTN_GROUPS)]
    h = x.reshape(M, D)
    for l in range(depth):
        g1 = norm1_g
        rkv = norm_matmul(h, g1, w_in, l, n_cols=c_rkv, tm=1024, tn=1024)
        lora = norm_matmul(h, g1, w_in, l, col0=c_rkv, n_cols=LORA_COLS, tm=1024, tn=LORA_COLS,
                           pieces=LORA_COLS // LANES)
        qkv_groups = qkv_proj(h, g1, w_qkv, l, batch=B, tm=1024)
        gates = norm_matmul(h, g1, w_in, l, col0=c_attn, n_cols=2 * D, tm=1024, tn=1024,
                            pieces=1024 // LANES, act="sigmoid", out_dtype=BF16)

        pack, lwf, lwb = rwkv_prep(
            rkv.reshape(B, S, c_rkv), lora.reshape(B, S, LORA_COLS), l, tshift_mu, w0, a0,
            w_lora_up, g_up, k_k, k_a, r_k, ts=256)
        yf, yb = rwkv_scan(pack, lwf, lwb)

        outs, lses = [], []
        for gi, (window, dilation) in enumerate(ATTN_GROUPS):
            assert window // (2 * dilation) == ATTN_HALF
            o, lse = dilated_attention(qkv_groups[gi], *bias_inputs[gi], gi,
                                       tq=min(S // dilation, ATTN_MAX_BLOCK))
            outs.append(o)
            lses.append(lse)

        orw, oat = branch_post(yf.reshape(M, RWKV_DIM), yb.reshape(M, RWKV_DIM),
                               pack.reshape(M, PACK_SLOTS * RWKV_DIM), gn_w, gn_b, l, outs, lses,
                               tm=512)
        merged = branch_merge(orw, oat, w_branch_rwkv, w_branch_attn, l, gates, tm=1024, tn=1024)
        h = matmul_residual(merged, w_out, l, h, tm=1024, tn=1024, tk=D)

        act = norm_matmul(h, norm2_g, w_mlp_in, l, n_cols=w_mlp_in.shape[2], tm=1024, tn=1024,
                          act="relu2", out_dtype=BF16)
        h = matmul_residual(act, w_mlp_out, l, h, tm=1024, tn=1024, tk=2048)
    out = rmsnorm_rows(h, final_g, tm=512)
    return out.reshape(B, S, D)
```

```python
import functools
import math

import jax
import jax.numpy as jnp
from jax import lax
from jax.experimental import pallas as pl
from jax.experimental.pallas import tpu as pltpu

F32 = jnp.float32
BF16 = jnp.bfloat16

HEAD_DIM = 64
LANES = 128
RWKV_DIM = 1024
DECAY_LORA = 96
AAA_LORA = 96
GATE_LORA = 256
LORA_COLS = GATE_LORA + 2 * DECAY_LORA + 2 * AAA_LORA
LORA_TAIL = LORA_COLS - GATE_LORA
ATTN_GROUPS = ((128, 1), (512, 4), (2048, 16))
ATTN_SLOTS = 8
ATTN_DIM = 1536
ATTN_OUT_DIM = ATTN_SLOTS * HEAD_DIM
ATTN_HALF = 64
ATTN_SUB = 128
ATTN_MAX_BLOCK = 1024
N_BUCKETS = 32
MAX_DISTANCE = 1024
RMS_EPS = 1e-6
GN_EPS = 64e-5
L2_EPS = 1e-12
NEG_BIG = -1e30
CHUNK = 64
PACK_DIR = 5
PACK_G = 2 * PACK_DIR
PACK_BONUS = PACK_G + 1
PACK_SLOTS = PACK_BONUS + 1
VMEM_LIMIT = 56 * 1024 * 1024


def _cparams(*sem):
    return pltpu.CompilerParams(dimension_semantics=sem, vmem_limit_bytes=VMEM_LIMIT)


def _dot(a, b):
    return jnp.dot(a.astype(BF16), b.astype(BF16), preferred_element_type=F32)


def _split3(x):
    p0 = x.astype(BF16)
    r1 = x - p0.astype(F32)
    p1 = r1.astype(BF16)
    p2 = (r1 - p1.astype(F32)).astype(BF16)
    return p0, p1, p2


def _dot_exact_rhs(a_bf16, x):
    return sum(jnp.dot(a_bf16, p, preferred_element_type=F32) for p in _split3(x))


def _head_sum(x, ones_bd):
    hi = x.astype(BF16)
    lo = (x - hi.astype(F32)).astype(BF16)
    return (jnp.dot(hi, ones_bd, preferred_element_type=F32)
            + jnp.dot(lo, ones_bd, preferred_element_type=F32))


def _head_block_ones(width):
    r = lax.broadcasted_iota(jnp.int32, (width, width), 0) // HEAD_DIM
    c = lax.broadcasted_iota(jnp.int32, (width, width), 1) // HEAD_DIM
    return (r == c).astype(BF16)


def _normalise_rows(x_ref, g_ref, u_ref, start, size):
    rows = pl.ds(start, size)
    x = x_ref[rows, :]
    ms = jnp.mean(x * x, axis=-1, keepdims=True)
    u_ref[rows, :] = (x * lax.rsqrt(ms + RMS_EPS) * g_ref[...]).astype(BF16)


def _x_rows_index(i, j, ni, nj):
    if nj == 1:
        return i
    return jnp.minimum(i + jnp.where(j >= 1, 1, 0), ni - 1)


BF16_ROWS = 16


def _with_normalised_tile(x_ref, g_ref, u_refs, nj, body):
    i, j = pl.program_id(0), pl.program_id(1)
    tm = x_ref.shape[0]
    if nj == 1:
        _normalise_rows(x_ref, g_ref, u_refs[0], 0, tm)
        body(u_refs[0])
        return

    @pl.when((i == 0) & (j == 0))
    def _():
        _normalise_rows(x_ref, g_ref, u_refs[0], 0, tm)

    size = -(-tm // (nj - 1))
    size = -(-size // BF16_ROWS) * BF16_ROWS
    start = pl.multiple_of(jnp.clip((j - 1) * size, 0, tm - size), BF16_ROWS)
    for parity in range(2):
        @pl.when(i % 2 == parity)
        def _():
            body(u_refs[parity])
            _normalise_rows(x_ref, g_ref, u_refs[1 - parity], start, size)


def _norm_mm_kernel(x_ref, g_ref, *refs, act, nj):
    *w_refs, o_ref, u0_ref, u1_ref = refs

    def body(u_ref):
        w = jnp.concatenate([w_ref[...].astype(BF16) for w_ref in w_refs], axis=1)
        acc = jnp.dot(u_ref[...], w, preferred_element_type=F32)
        if act == "relu2":
            acc = jnp.square(jnp.maximum(acc, 0.0))
        elif act == "sigmoid":
            acc = 0.5 * jnp.tanh(0.5 * acc) + 0.5
        o_ref[...] = acc.astype(o_ref.dtype)

    _with_normalised_tile(x_ref, g_ref, (u0_ref, u1_ref), nj, body)


def norm_matmul(x, g, w, layer, *, col0=0, n_cols, tm, tn, pieces=1, act=None, out_dtype=F32):
    M, K = x.shape
    pw = tn // pieces
    assert pw * pieces == tn and col0 % pw == 0 and n_cols % tn == 0
    w_specs = [pl.BlockSpec((None, K, pw), functools.partial(
        lambda i, j, t: (layer, 0, col0 // pw + j * pieces + t), t=t)) for t in range(pieces)]
    ni, nj = M // tm, n_cols // tn
    return pl.pallas_call(
        functools.partial(_norm_mm_kernel, act=act, nj=nj),
        grid=(ni, nj),
        in_specs=[pl.BlockSpec((tm, K), lambda i, j: (_x_rows_index(i, j, ni, nj), 0)),
                  pl.BlockSpec((None, 1, K), lambda i, j: (layer, 0, 0))] + w_specs,
        out_specs=pl.BlockSpec((tm, tn), lambda i, j: (i, j)),
        out_shape=jax.ShapeDtypeStruct((M, n_cols), out_dtype),
        scratch_shapes=[pltpu.VMEM((tm, K), BF16)] * 2,
        compiler_params=_cparams("arbitrary", "arbitrary"),
        name="norm_matmul_" + (act or "id"),
    )(x, g.reshape(-1, 1, K), *([w] * pieces))


def _qkv_proj_kernel(x_ref, g_ref, w_ref, o0_ref, o1_ref, o2_ref, u_ref, acc_ref):
    tm = x_ref.shape[0]

    @pl.when(pl.program_id(1) == 0)
    def _():
        _normalise_rows(x_ref, g_ref, u_ref, 0, tm)

    acc = jnp.dot(u_ref[...], w_ref[...], preferred_element_type=F32)
    acc = acc * jnp.where(pl.program_id(1) == 0, HEAD_DIM ** -0.5, 1.0)
    tiles_per_group = ATTN_OUT_DIM // LANES
    for gi, (o_ref, (_, d)) in enumerate(zip((o0_ref, o1_ref, o2_ref), ATTN_GROUPS)):
        if d == 1:
            o_ref[0, 0] = acc[:, gi * ATTN_OUT_DIM:(gi + 1) * ATTN_OUT_DIM].astype(BF16)
            continue
        for t in range(tiles_per_group):
            c = gi * tiles_per_group + t
            acc_ref[c] = acc[:, c * LANES:(c + 1) * LANES]
            for res in range(d):
                o_ref[0, res, :, t * LANES:(t + 1) * LANES] = (
                    acc_ref[c, pl.ds(res, tm // d, stride=d), :].astype(BF16))


def qkv_proj(x, g, w, layer, *, batch, tm):
    M, K = x.shape
    S = M // batch
    nb = S // tm
    outs_shape, outs_spec = [], []
    for _, d in ATTN_GROUPS:
        outs_shape.append(jax.ShapeDtypeStruct((batch, d, S // d, 3 * ATTN_OUT_DIM), BF16))
        outs_spec.append(pl.BlockSpec((1, d, tm // d, ATTN_OUT_DIM),
                                      lambda i, j: (i // nb, 0, i % nb, j)))
    return pl.pallas_call(
        _qkv_proj_kernel,
        grid=(M // tm, 3),
        in_specs=[pl.BlockSpec((tm, K), lambda i, j: (i, 0)),
                  pl.BlockSpec((None, 1, K), lambda i, j: (layer, 0, 0)),
                  pl.BlockSpec((None, K, ATTN_DIM), lambda i, j: (layer, 0, j))],
        out_specs=outs_spec,
        out_shape=outs_shape,
        scratch_shapes=[pltpu.VMEM((tm, K), BF16), pltpu.VMEM((ATTN_DIM // LANES, tm, LANES), F32)],
        compiler_params=_cparams("parallel", "arbitrary"),
        name="qkv_proj",
    )(x, g.reshape(-1, 1, K), w)


def _mm_res_kernel(x_ref, w_ref, r_ref, o_ref, acc_ref):
    k = pl.program_id(2)

    @pl.when(k == 0)
    def _():
        acc_ref[...] = jnp.zeros_like(acc_ref)

    acc_ref[...] += jnp.dot(x_ref[...], w_ref[...].astype(BF16), preferred_element_type=F32)

    @pl.when(k == pl.num_programs(2) - 1)
    def _():
        o_ref[...] = r_ref[...] + acc_ref[...]


def matmul_residual(x, w, layer, res, *, tm, tn, tk):
    M, K = x.shape
    N = w.shape[2]
    return pl.pallas_call(
        _mm_res_kernel,
        grid=(M // tm, N // tn, K // tk),
        in_specs=[pl.BlockSpec((tm, tk), lambda i, j, k: (i, k)),
                  pl.BlockSpec((None, tk, tn), lambda i, j, k: (layer, k, j)),
                  pl.BlockSpec((tm, tn), lambda i, j, k: (i, j))],
        out_specs=pl.BlockSpec((tm, tn), lambda i, j, k: (i, j)),
        out_shape=jax.ShapeDtypeStruct((M, N), F32),
        scratch_shapes=[pltpu.VMEM((tm, tn), F32)],
        compiler_params=_cparams("parallel", "parallel", "arbitrary"),
        name="matmul_residual",
    )(x, w, res)


def _rmsnorm_kernel(x_ref, g_ref, o_ref):
    x = x_ref[...]
    ms = jnp.mean(x * x, axis=-1, keepdims=True)
    o_ref[...] = x * lax.rsqrt(ms + RMS_EPS) * g_ref[...]


def rmsnorm_rows(x, g, *, tm):
    M, K = x.shape
    return pl.pallas_call(
        _rmsnorm_kernel,
        grid=(M // tm,),
        in_specs=[pl.BlockSpec((tm, K), lambda i: (i, 0)),
                  pl.BlockSpec((1, K), lambda i: (0, 0))],
        out_specs=pl.BlockSpec((tm, K), lambda i: (i, 0)),
        out_shape=jax.ShapeDtypeStruct((M, K), F32),
        compiler_params=_cparams("parallel"),
        name="final_rmsnorm",
    )(x, g.reshape(1, K))


def _token_shift(main, prev_blk, next_blk, mu, first, last):
    rows = main.shape[0]
    prev_row = jnp.where(first, 0.0, prev_blk[7:8, :])
    next_row = jnp.where(last, 0.0, next_blk[0:1, :])
    rid = lax.broadcasted_iota(jnp.int32, main.shape, 0)
    up = jnp.where(rid == 0, prev_row, pltpu.roll(main, 1, 0))
    dn = jnp.where(rid == rows - 1, next_row, pltpu.roll(main, rows - 1, 0))
    return main * (1.0 - mu) + (0.5 * mu) * (up + dn)


def _rwkv_prep_kernel(rkv_m, rkv_p, rkv_n, l_m, l_p, l_n,
                      mu_ref, w0_ref, a0_ref, lora_w_ref, g_up_ref, kk_ref, ka_ref, rk_ref,
                      pack_o, lwf_o, lwb_o):
    i = pl.program_id(1)

    def put(slot, val):
        pack_o[0, :, slot * RWKV_DIM:(slot + 1) * RWKV_DIM] = val.astype(pack_o.dtype)

    first = i == 0
    last = i == pl.num_programs(1) - 1
    D = RWKV_DIM
    r, k, v = (_token_shift(rkv_m[0, :, c * D:(c + 1) * D], rkv_p[0, :, c * D:(c + 1) * D],
                            rkv_n[0, :, c * D:(c + 1) * D], mu_ref[:, c * D:(c + 1) * D], first, last)
               for c in range(3))
    lo = _token_shift(l_m[0], l_p[0], l_n[0], mu_ref[:, 3 * D:], first, last)

    g = _dot(jax.nn.sigmoid(lo[:, :GATE_LORA]), g_up_ref[...])
    tail = lo[:, GATE_LORA:]
    col = lax.broadcasted_iota(jnp.int32, tail.shape, 1)
    tail = jnp.where(col < 2 * DECAY_LORA, jnp.tanh(tail), tail)
    up = _dot(tail, lora_w_ref[...])

    ones_bd = _head_block_ones(2 * LANES)

    def head_sum(x):
        return jnp.concatenate(
            [_head_sum(x[:, c:c + 2 * LANES], ones_bd) for c in range(0, RWKV_DIM, 2 * LANES)], axis=1)

    kk = k * kk_ref[...]
    kk = kk * jnp.minimum(lax.rsqrt(head_sum(kk * kk)), 1.0 / L2_EPS)

    for d in range(2):
        put(PACK_DIR * d + 0, r)
        put(PACK_DIR * d + 1, v)
        put(PACK_DIR * d + 2, kk)
    put(PACK_G, g)
    kd_sum = jnp.zeros_like(k)
    for d, lw_o in enumerate((lwf_o, lwb_o)):
        z = w0_ref[d:d + 1, :] + up[:, d * RWKV_DIM:(d + 1) * RWKV_DIM]
        nz = -z
        softplus = jnp.maximum(nz, 0.0) + jnp.log(1.0 + jnp.exp(-jnp.abs(nz)))
        lw_o[0] = -jnp.exp(-softplus - 0.5)
        a = jax.nn.sigmoid(a0_ref[d:d + 1, :] + up[:, (2 + d) * RWKV_DIM:(3 + d) * RWKV_DIM])
        kd = k * (1.0 + (a - 1.0) * ka_ref[...])
        put(PACK_DIR * d + 3, kk * a)
        put(PACK_DIR * d + 4, kd)
        kd_sum = kd_sum + kd
    put(PACK_BONUS, head_sum(r * kd_sum * rk_ref[...]) * v)


def rwkv_prep(rkv, lora, layer, mu, w0, a0, lora_w, g_up, k_k, k_a, r_k, *, ts):
    B, S, _ = rkv.shape
    D = RWKV_DIM
    nblk8 = S // 8

    def main_spec(width, cblk):
        return pl.BlockSpec((1, ts, width), lambda b, i: (b, i, cblk))

    def prev_spec(width, cblk):
        return pl.BlockSpec((1, 8, width), lambda b, i: (b, jnp.maximum(i * (ts // 8) - 1, 0), cblk))

    def next_spec(width, cblk):
        return pl.BlockSpec((1, 8, width), lambda b, i: (b, jnp.minimum((i + 1) * (ts // 8), nblk8 - 1), cblk))

    def layer_spec(rows, width):
        return pl.BlockSpec((None, rows, width), lambda b, i: (layer, 0, 0))

    in_specs = []
    args = []
    in_specs += [main_spec(3 * D, 0), prev_spec(3 * D, 0), next_spec(3 * D, 0)]
    args += [rkv, rkv, rkv]
    in_specs += [main_spec(LORA_COLS, 0), prev_spec(LORA_COLS, 0), next_spec(LORA_COLS, 0)]
    args += [lora, lora, lora]
    depth = mu.shape[0]
    in_specs += [layer_spec(1, 3 * D + LORA_COLS), layer_spec(2, D), layer_spec(2, D),
                 layer_spec(LORA_TAIL, 4 * D), layer_spec(GATE_LORA, D),
                 layer_spec(1, D), layer_spec(1, D), layer_spec(1, D)]
    args += [mu.reshape(depth, 1, -1), w0, a0, lora_w, g_up,
             k_k.reshape(depth, 1, D), k_a.reshape(depth, 1, D), r_k.reshape(depth, 1, D)]
    out_spec = pl.BlockSpec((1, ts, D), lambda b, i: (b, i, 0))
    return pl.pallas_call(
        _rwkv_prep_kernel,
        grid=(B, S // ts),
        in_specs=in_specs,
        out_specs=[pl.BlockSpec((1, ts, PACK_SLOTS * D), lambda b, i: (b, i, 0)), out_spec, out_spec],
        out_shape=[jax.ShapeDtypeStruct((B, S, PACK_SLOTS * D), BF16),
                   jax.ShapeDtypeStruct((B, S, D), F32), jax.ShapeDtypeStruct((B, S, D), F32)],
        compiler_params=_cparams("parallel", "arbitrary"),
        name="rwkv_prep",
    )(*args)


_BMM = (((2,), (1,)), ((0,), (0,)))
_BMM_NT = (((2,), (2,)), ((0,), (0,)))
_BMM_TN = (((1,), (1,)), ((0,), (0,)))


def _bdot(a, b, dims=_BMM):
    return lax.dot_general(a.astype(BF16), b.astype(BF16), dims, preferred_element_type=F32)


def _bd_stack(x, m0):
    zero = jnp.zeros_like(x)
    return jnp.concatenate([jnp.where(m0, x, zero), jnp.where(m0, zero, x)], axis=-2)


def _scan_chunk(r, v, kk, lw, cum, ka, kd, s_bd, consts):
    is_rev, strict2, incl2, eye2, m0_1, m0_2, bd_mask, off_masks = consts
    C = r.shape[1]
    cum_prev = cum - lw
    e_cum = jnp.exp(cum)
    e_prev = jnp.exp(cum_prev)
    e_neg = jnp.exp(-cum)
    total = jnp.where(is_rev, cum[:, 0:1, :], cum[:, C - 1:C, :])
    e_rest = jnp.exp(total - cum)
    r0 = r * e_cum
    at0 = -kk * e_prev
    bt = ka * e_neg
    kt = kd * e_neg
    bh = ka * e_rest
    kh = kd * e_rest

    gram = _bdot(jnp.concatenate([at0, r0], axis=1),
                 jnp.concatenate([_bd_stack(bt, m0_1), _bd_stack(kt, m0_1)], axis=1), _BMM_NT)
    zero = jnp.zeros_like(gram[:, :C])
    a_top = jnp.where(strict2, gram[:, :C], zero)
    a_bot = jnp.where(incl2, gram[:, C:], zero)
    a_ab = a_top[:, :, :LANES]
    a_ak = a_top[:, :, LANES:]

    zero1 = jnp.zeros_like(a_ab)
    t = eye2 + jnp.where(off_masks[0], a_ab, zero1)
    for off_mask in off_masks[1:]:
        z = _bdot(t, _bd_stack(jnp.where(off_mask, a_ab, zero1), m0_1))
        t = t + _bdot(z, _bd_stack(t, m0_1))

    av = _bdot(a_ak, _bd_stack(v, m0_1))
    wu = _bdot(t, _bd_stack(jnp.concatenate([at0, av], axis=2), m0_2))
    w_a = wu[:, :, :LANES]
    u_v = wu[:, :, LANES:]

    wr = _bdot(jnp.concatenate([w_a, r0], axis=1), s_bd, _BMM_NT)
    u = wr[:, :C] + u_v
    y = wr[:, C:] + _bdot(a_bot, jnp.concatenate([_bd_stack(u, m0_1), _bd_stack(v, m0_1)], axis=1))
    upd = _bdot(jnp.concatenate([u, v], axis=1), jnp.concatenate([bh, kh], axis=1), _BMM_TN)
    s_new = s_bd * jnp.exp(total) + jnp.where(bd_mask, upd, jnp.zeros_like(upd))
    return y, s_new


def _scan_consts(C, n_fwd, n_bwd):
    G = n_fwd + n_bwd
    is_rev = lax.broadcasted_iota(jnp.int32, (G, 1, 1), 0) >= n_fwd
    sign = jnp.where(is_rev, -1, 1)
    row2 = lax.broadcasted_iota(jnp.int32, (G, C, 2 * LANES), 1)
    col2 = lax.broadcasted_iota(jnp.int32, (G, C, 2 * LANES), 2) % HEAD_DIM
    strict2 = (col2 - row2) * sign < 0
    incl2 = (col2 - row2) * sign <= 0
    lane1 = lax.broadcasted_iota(jnp.int32, (1, C, LANES), 2)
    lane2 = lax.broadcasted_iota(jnp.int32, (1, C, 2 * LANES), 2) % LANES
    m0_1 = lane1 < HEAD_DIM
    m0_2 = lane2 < HEAD_DIM
    rb = lax.broadcasted_iota(jnp.int32, (1, LANES, LANES), 1) // HEAD_DIM
    cb = lax.broadcasted_iota(jnp.int32, (1, LANES, LANES), 2) // HEAD_DIM
    bd_mask = rb == cb
    row1 = lax.broadcasted_iota(jnp.int32, (1, C, LANES), 1)
    col1 = lane1 % HEAD_DIM
    eye2 = (row1 == col1).astype(F32)
    off_masks = [row1 // 2 == col1 // 2]
    b = 2
    while b < C:
        off_masks.append((row1 // (2 * b) == col1 // (2 * b)) & (row1 // b != col1 // b))
        b *= 2
    return (is_rev, strict2, incl2, eye2, m0_1, m0_2, bd_mask, off_masks)


def _rwkv_scan_kernel(pack_f, pack_b, lwf, lwb, yf_o, yb_o, state_ref):
    c = pl.program_id(0)

    @pl.when(c == 0)
    def _():
        state_ref[...] = jnp.zeros_like(state_ref)

    nb, C, D = lwf.shape
    n_pairs = D // LANES
    n_dir = nb * n_pairs
    consts = _scan_consts(C, n_dir, n_dir)

    row = lax.broadcasted_iota(jnp.int32, (C, C), 0)
    col = lax.broadcasted_iota(jnp.int32, (C, C), 1)
    tri_f = (col <= row).astype(F32).astype(BF16)
    tri_b = (col >= row).astype(F32).astype(BF16)
    cum_f = [_dot_exact_rhs(tri_f, lwf[b]) for b in range(nb)]
    cum_b = [_dot_exact_rhs(tri_b, lwb[b]) for b in range(nb)]

    def chains(xs_f, xs_b):
        return jnp.stack([x[:, p * LANES:(p + 1) * LANES]
                          for xs in (xs_f, xs_b) for x in xs for p in range(n_pairs)], axis=0)

    def slot(ref, k):
        return [ref[b, :, k * D:(k + 1) * D].astype(F32) for b in range(nb)]

    r, v, kk, ka, kd = (chains(slot(pack_f, k), slot(pack_b, k)) for k in range(PACK_DIR))
    lw = chains([lwf[b] for b in range(nb)], [lwb[b] for b in range(nb)])
    y, s_new = _scan_chunk(r, v, kk, lw, chains(cum_f, cum_b), ka, kd, state_ref[...], consts)
    state_ref[...] = s_new
    for d, y_o in enumerate((yf_o, yb_o)):
        for b in range(nb):
            for p in range(n_pairs):
                y_o[b, :, p * LANES:(p + 1) * LANES] = y[(d * nb + b) * n_pairs + p].astype(y_o.dtype)


def rwkv_scan(pack, lwf, lwb):
    B, S, D = lwf.shape
    C = CHUNK
    nc = S // C
    fwd = pl.BlockSpec((B, C, D), lambda c: (0, c, 0))
    bwd = pl.BlockSpec((B, C, D), lambda c: (0, nc - 1 - c, 0))
    return pl.pallas_call(
        _rwkv_scan_kernel,
        grid=(nc,),
        in_specs=[pl.BlockSpec((B, C, PACK_DIR * D), lambda c: (0, c, 0)),
                  pl.BlockSpec((B, C, PACK_DIR * D), lambda c: (0, nc - 1 - c, 1)), fwd, bwd],
        out_specs=[fwd, bwd],
        out_shape=[jax.ShapeDtypeStruct((B, S, D), BF16)] * 2,
        scratch_shapes=[pltpu.VMEM((2 * B * (D // LANES), LANES, LANES), F32)],
        compiler_params=_cparams("arbitrary"),
        name="rwkv_scan",
    )(pack, pack, lwf, lwb)


def _attn_kernel(table_ref, bucket_ref, q_ref, kp_ref, km_ref, kn_ref, vp_ref, vm_ref, vn_ref,
                 o_ref, lse_ref, bias_ref, *, nblk):
    i = pl.program_id(2)
    n_sub = q_ref.shape[1] // ATTN_SUB
    TQ = ATTN_SUB
    TK = TQ + 2 * ATTN_HALF

    @pl.when((pl.program_id(0) == 0) & (pl.program_id(1) == 0) & (i == 0))
    def _():
        bucket = bucket_ref[...]
        row = lax.broadcasted_iota(jnp.int32, (TQ, TK), 0)
        colk = lax.broadcasted_iota(jnp.int32, (TQ, TK), 1)
        band = jnp.abs(colk - ATTN_HALF - row) <= ATTN_HALF
        has_prev = colk >= ATTN_HALF
        has_next = colk < TQ + ATTN_HALF
        for h in range(ATTN_SLOTS):
            acc = jnp.zeros((TQ, TK), F32)
            for b in range(N_BUCKETS):
                acc = jnp.where(bucket == b, table_ref[h * N_BUCKETS + b], acc)
            acc = jnp.where(band, acc, NEG_BIG)
            bias_ref[0, h] = acc
            bias_ref[1, h] = jnp.where(has_prev, acc, NEG_BIG)
            bias_ref[2, h] = jnp.where(has_next, acc, NEG_BIG)
            bias_ref[3, h] = jnp.where(has_prev & has_next, acc, NEG_BIG)

    m0 = lax.broadcasted_iota(jnp.int32, (TQ, LANES), 1) < HEAD_DIM
    lane = lax.broadcasted_iota(jnp.int32, (1, LANES), 1)
    head_keep = [(lane < HEAD_DIM).astype(F32).astype(BF16), (lane >= HEAD_DIM).astype(F32).astype(BF16)]
    for rr, p in ((rr, p) for rr in range(q_ref.shape[0]) for p in range(ATTN_OUT_DIM // LANES)):
        sl = slice(p * LANES, (p + 1) * LANES)
        kfull = jnp.concatenate([kp_ref[rr, :, sl], km_ref[rr, :, sl], kn_ref[rr, :, sl]], axis=0)
        vfull = jnp.concatenate([vp_ref[rr, :, sl], vm_ref[rr, :, sl], vn_ref[rr, :, sl]], axis=0)
        for sb in range(n_sub):
            rows = slice(sb * TQ, (sb + 1) * TQ)
            first = (i == 0) if sb == 0 else False
            last = (i == nblk - 1) if sb == n_sub - 1 else False
            variant = jnp.where(first, 1, 0) + jnp.where(last, 2, 0)
            q2 = q_ref[rr, rows, sl]
            kwin = kfull[sb * TQ:sb * TQ + TK]
            vwin = vfull[sb * TQ:sb * TQ + TK]
            outs, lses = [], []
            for hh in range(2):
                qm = q2 * head_keep[hh]
                s = lax.dot_general(qm, kwin, (((1,), (1,)), ((), ())), preferred_element_type=F32)
                s = s + bias_ref[variant, 2 * p + hh]
                m = jnp.max(s, axis=-1, keepdims=True)
                e = jnp.exp(s - m)
                den = jnp.sum(e, axis=-1, keepdims=True)
                pv = jnp.dot(e.astype(BF16), vwin, preferred_element_type=F32)
                outs.append(pv / den)
                lses.append(jnp.broadcast_to(m + jnp.log(den), (TQ, LANES)))
            o_ref[rr, rows, sl] = jnp.where(m0, outs[0], outs[1]).astype(o_ref.dtype)
            lse_ref[rr, rows, sl] = jnp.where(m0, lses[0], lses[1])


def _t5_bucket(rel):
    nb = N_BUCKETS // 2
    max_exact = nb // 2
    ret = jnp.where(rel > 0, nb, 0)
    n = jnp.abs(rel)
    nf = jnp.maximum(n, 1).astype(jnp.float32)
    large = max_exact + (jnp.log(nf / max_exact) / math.log(MAX_DISTANCE / max_exact)
                         * (nb - max_exact)).astype(jnp.int32)
    large = jnp.minimum(large, nb - 1)
    return ret + jnp.where(n < max_exact, n, large)


def attention_bias_inputs(table, gi, dilation):
    tq = ATTN_SUB
    tk = tq + 2 * ATTN_HALF
    rel = jnp.arange(tk)[None, :] - ATTN_HALF - jnp.arange(tq)[:, None]
    bucket = _t5_bucket(rel * dilation).astype(jnp.int32)
    tbl = table[:, gi * ATTN_SLOTS:(gi + 1) * ATTN_SLOTS].astype(F32).T.reshape(-1)
    return tbl, bucket


def dilated_attention(qkv, tbl, bucket, gi, *, tq):
    B, dilation, L, _ = qkv.shape
    nblk = L // tq
    hb = tq // ATTN_HALF
    nhalf = L // ATTN_HALF
    W = ATTN_OUT_DIM
    nres = max(1, min(dilation, ATTN_MAX_BLOCK // tq))

    def main(off):
        return pl.BlockSpec((None, nres, tq, W), lambda b, r, i: (b, r, i, off))

    def prev(off):
        return pl.BlockSpec((None, nres, ATTN_HALF, W),
                            lambda b, r, i: (b, r, jnp.maximum(i * hb - 1, 0), off))

    def nxt(off):
        return pl.BlockSpec((None, nres, ATTN_HALF, W),
                            lambda b, r, i: (b, r, jnp.minimum((i + 1) * hb, nhalf - 1), off))

    sub_shape = (ATTN_SUB, ATTN_SUB + 2 * ATTN_HALF)
    out_spec = pl.BlockSpec((None, nres, tq, W), lambda b, r, i: (b, r, i, 0))
    return pl.pallas_call(
        functools.partial(_attn_kernel, nblk=nblk),
        grid=(B, dilation // nres, nblk),
        in_specs=[pl.BlockSpec(memory_space=pltpu.SMEM),
                  pl.BlockSpec(sub_shape, lambda b, r, i: (0, 0)),
                  main(0), prev(1), main(1), nxt(1), prev(2), main(2), nxt(2)],
        out_specs=[out_spec, out_spec],
        out_shape=[jax.ShapeDtypeStruct((B, dilation, L, W), BF16),
                   jax.ShapeDtypeStruct((B, dilation, L, W), F32)],
        scratch_shapes=[pltpu.VMEM((4, ATTN_SLOTS) + sub_shape, F32)],
        compiler_params=_cparams("arbitrary", "arbitrary", "arbitrary"),
        name=f"dilated_attn_g{gi}",
    )(tbl, bucket, qkv, qkv, qkv, qkv, qkv, qkv, qkv)


def _branch_post_kernel(yf_ref, yb_ref, bonus_ref, g_ref, gnw_ref, gnb_ref,
                        o0, o1, o2, l0, l1, l2, orw_ref, oat_ref, *scratch):
    y = yf_ref[...].astype(F32) + yb_ref[...].astype(F32)
    ones_bd = _head_block_ones(2 * LANES)
    tm, width = y.shape

    def head_mean(x):
        return jnp.concatenate(
            [_head_sum(x[:, c:c + 2 * LANES], ones_bd) for c in range(0, width, 2 * LANES)],
            axis=1) * (1.0 / HEAD_DIM)

    mean = head_mean(y)
    yc = y - mean
    var = head_mean(yc * yc)
    yn = yc * lax.rsqrt(var + GN_EPS) * gnw_ref[...] + gnb_ref[...]
    orw_ref[...] = ((yn + bonus_ref[...].astype(F32)) * g_ref[...].astype(F32)).astype(BF16)

    def natural_order(ref, scr):
        d = ref.shape[0]
        for t in range(ATTN_OUT_DIM // LANES):
            for res in range(d):
                scr[t, pl.ds(res, tm // d, stride=d), :] = (
                    ref[res, :, t * LANES:(t + 1) * LANES].astype(F32))
        return jnp.concatenate([scr[t] for t in range(ATTN_OUT_DIM // LANES)], axis=1)

    s_o1, s_o2, s_l1, s_l2 = scratch
    oa, ob, oc = o0[0].astype(F32), natural_order(o1, s_o1), natural_order(o2, s_o2)
    la, lb, lc = l0[0], natural_order(l1, s_l1), natural_order(l2, s_l2)
    m = jnp.maximum(jnp.maximum(la, lb), lc)
    ea, eb, ec = jnp.exp(la - m), jnp.exp(lb - m), jnp.exp(lc - m)
    den = ea + eb + ec
    oat_ref[...] = ((ea * oa + eb * ob + ec * oc) / den).astype(BF16)


def branch_post(yf, yb, pack, gn_w, gn_b, layer, outs, lses, *, tm):
    M, D = yf.shape
    A = ATTN_OUT_DIM
    batch = outs[0].shape[0]
    nb = M // batch // tm
    big = pl.BlockSpec((tm, D), lambda b, i: (b * nb + i, 0))
    bonus = pl.BlockSpec((tm, D), lambda b, i: (b * nb + i, PACK_BONUS))
    g = pl.BlockSpec((tm, D), lambda b, i: (b * nb + i, PACK_G))
    rowp = pl.BlockSpec((None, 1, D), lambda b, i: (layer, 0, 0))
    attn_specs = [pl.BlockSpec((None, o.shape[1], tm // o.shape[1], A), lambda b, i: (b, 0, i, 0))
                  for o in outs]
    return pl.pallas_call(
        _branch_post_kernel,
        grid=(batch, nb),
        in_specs=[big, big, bonus, g, rowp, rowp] + attn_specs + attn_specs,
        out_specs=[big, pl.BlockSpec((tm, A), lambda b, i: (b * nb + i, 0))],
        out_shape=[jax.ShapeDtypeStruct((M, D), BF16), jax.ShapeDtypeStruct((M, A), BF16)],
        scratch_shapes=[pltpu.VMEM((A // LANES, tm, LANES), F32)] * 4,
        compiler_params=_cparams("parallel", "arbitrary"),
        name="branch_post",
    )(yf, yb, pack, pack, gn_w.reshape(-1, 1, D), gn_b.reshape(-1, 1, D), *outs, *lses)


def _merge_kernel(orw_ref, oat_ref, wr_ref, wa_ref, gr_ref, ga_ref, o_ref):
    a = jnp.dot(orw_ref[...], wr_ref[...].astype(BF16), preferred_element_type=F32)
    b = jnp.dot(oat_ref[...], wa_ref[...].astype(BF16), preferred_element_type=F32)
    o_ref[...] = (gr_ref[...].astype(F32) * a + ga_ref[...].astype(F32) * b).astype(o_ref.dtype)


def branch_merge(orw, oat, w_r, w_a, layer, gates, *, tm, tn):
    M = orw.shape[0]
    N = w_r.shape[2]
    nj = N // tn
    return pl.pallas_call(
        _merge_kernel,
        grid=(M // tm, nj),
        in_specs=[pl.BlockSpec((tm, orw.shape[1]), lambda i, j: (i, 0)),
                  pl.BlockSpec((tm, oat.shape[1]), lambda i, j: (i, 0)),
                  pl.BlockSpec((None, w_r.shape[1], tn), lambda i, j: (layer, 0, j)),
                  pl.BlockSpec((None, w_a.shape[1], tn), lambda i, j: (layer, 0, j)),
                  pl.BlockSpec((tm, tn), lambda i, j: (i, j)),
                  pl.BlockSpec((tm, tn), lambda i, j: (i, j + nj))],
        out_specs=pl.BlockSpec((tm, tn), lambda i, j: (i, j)),
        out_shape=jax.ShapeDtypeStruct((M, N), BF16),
        compiler_params=_cparams("parallel", "arbitrary"),
        name="branch_merge",
    )(orw, oat, w_r, w_a, gates, gates)


def _lora_weights(w_up, a_up):
    z = jnp.zeros_like(w_up[:, 0])
    rows = [jnp.concatenate([w_up[:, 0], z, z, z], axis=2),
            jnp.concatenate([z, w_up[:, 1], z, z], axis=2),
            jnp.concatenate([z, z, a_up[:, 0], z], axis=2),
            jnp.concatenate([z, z, z, a_up[:, 1]], axis=2)]
    return jnp.concatenate(rows, axis=1).astype(BF16)


def kernel(x, norm1_g, w_in, tshift_mu, w0, w_lora_up, a0, a_lora_up, g_lora_up, k_k, k_a, r_k,
           gn_w, gn_b, rel_bias, w_branch_rwkv, w_branch_attn, w_out, norm2_g, w_mlp_in, w_mlp_out,
           final_g):
    B, S, D = x.shape
    M = B * S
    depth = w_in.shape[0]
    c_rkv = 3 * RWKV_DIM
    c_slab = c_rkv + LORA_COLS
    c_attn = c_slab + 3 * ATTN_DIM
    w_qkv = w_in[:, :, c_slab:c_attn].astype(BF16)
    w_lora_up = _lora_weights(w_lora_up, a_lora_up)
    g_up = g_lora_up.astype(BF16)
    bias_inputs = [attention_bias_inputs(rel_bias, gi, dilation)
                   for gi, (_, dilation) in enumerate(ATTN_GROUPS)]
    h = x.reshape(M, D)
    for l in range(depth):
        g1 = norm1_g
        rkv = norm_matmul(h, g1, w_in, l, n_cols=c_rkv, tm=1024, tn=1024)
        lora = norm_matmul(h, g1, w_in, l, col0=c_rkv, n_cols=LORA_COLS, tm=1024, tn=LORA_COLS,
                           pieces=LORA_COLS // LANES)
        qkv_groups = qkv_proj(h, g1, w_qkv, l, batch=B, tm=1024)
        gates = norm_matmul(h, g1, w_in, l, col0=c_attn, n_cols=2 * D, tm=1024, tn=1024,
                            pieces=1024 // LANES, act="sigmoid", out_dtype=BF16)

        pack, lwf, lwb = rwkv_prep(
            rkv.reshape(B, S, c_rkv), lora.reshape(B, S, LORA_COLS), l, tshift_mu, w0, a0,
            w_lora_up, g_up, k_k, k_a, r_k, ts=256)
        yf, yb = rwkv_scan(pack, lwf, lwb)

        outs, lses = [], []
        for gi, (window, dilation) in enumerate(ATTN_GROUPS):
            assert window // (2 * dilation) == ATTN_HALF
            o, lse = dilated_attention(qkv_groups[gi], *bias_inputs[gi], gi,
                                       tq=min(S // dilation, ATTN_MAX_BLOCK))
            outs.append(o)
            lses.append(lse)

        orw, oat = branch_post(yf.reshape(M, RWKV_DIM), yb.reshape(M, RWKV_DIM),
                               pack.reshape(M, PACK_SLOTS * RWKV_DIM), gn_w, gn_b, l, outs, lses,
                               tm=512)
        merged = branch_merge(orw, oat, w_branch_rwkv, w_branch_attn, l, gates, tm=2048, tn=512)
        h = matmul_residual(merged, w_out, l, h, tm=1024, tn=1024, tk=D)

        act = norm_matmul(h, norm2_g, w_mlp_in, l, n_cols=w_mlp_in.shape[2], tm=1024, tn=1024,
                          act="relu2", out_dtype=BF16)
        h = matmul_residual(act, w_mlp_out, l, h, tm=1024, tn=1024, tk=2048)
    out = rmsnorm_rows(h, final_g, tm=512)
    return out.reshape(B, S, D)
```

```python
import functools
import math

import jax
import jax.numpy as jnp
from jax import lax
from jax.experimental import pallas as pl
from jax.experimental.pallas import tpu as pltpu

F32 = jnp.float32
BF16 = jnp.bfloat16

HEAD_DIM = 64
LANES = 128
RWKV_DIM = 1024
DECAY_LORA = 96
AAA_LORA = 96
GATE_LORA = 256
LORA_COLS = GATE_LORA + 2 * DECAY_LORA + 2 * AAA_LORA
LORA_TAIL = LORA_COLS - GATE_LORA
ATTN_GROUPS = ((128, 1), (512, 4), (2048, 16))
ATTN_SLOTS = 8
ATTN_DIM = 1536
ATTN_OUT_DIM = ATTN_SLOTS * HEAD_DIM
ATTN_HALF = 64
ATTN_SUB = 128
ATTN_MAX_BLOCK = 1024
N_BUCKETS = 32
MAX_DISTANCE = 1024
RMS_EPS = 1e-6
GN_EPS = 64e-5
L2_EPS = 1e-12
NEG_BIG = -1e30
CHUNK = 64
PACK_DIR = 5
PACK_G = 2 * PACK_DIR
PACK_BONUS = PACK_G + 1
PACK_SLOTS = PACK_BONUS + 1
VMEM_LIMIT = 56 * 1024 * 1024


def _cparams(*sem):
    return pltpu.CompilerParams(dimension_semantics=sem, vmem_limit_bytes=VMEM_LIMIT)


def _dot(a, b):
    return jnp.dot(a.astype(BF16), b.astype(BF16), preferred_element_type=F32)


def _split3(x):
    p0 = x.astype(BF16)
    r1 = x - p0.astype(F32)
    p1 = r1.astype(BF16)
    p2 = (r1 - p1.astype(F32)).astype(BF16)
    return p0, p1, p2


def _dot_exact_rhs(a_bf16, x):
    return sum(jnp.dot(a_bf16, p, preferred_element_type=F32) for p in _split3(x))


def _head_sum(x, ones_bd):
    hi = x.astype(BF16)
    lo = (x - hi.astype(F32)).astype(BF16)
    return (jnp.dot(hi, ones_bd, preferred_element_type=F32)
            + jnp.dot(lo, ones_bd, preferred_element_type=F32))


def _head_block_ones(width):
    r = lax.broadcasted_iota(jnp.int32, (width, width), 0) // HEAD_DIM
    c = lax.broadcasted_iota(jnp.int32, (width, width), 1) // HEAD_DIM
    return (r == c).astype(BF16)


def _normalise_rows(x_ref, g_ref, u_ref, start, size):
    rows = pl.ds(start, size)
    x = x_ref[rows, :]
    ms = jnp.mean(x * x, axis=-1, keepdims=True)
    u_ref[rows, :] = (x * lax.rsqrt(ms + RMS_EPS) * g_ref[...]).astype(BF16)


def _x_rows_index(i, j, ni, nj):
    if nj == 1:
        return i
    return jnp.minimum(i + jnp.where(j >= 1, 1, 0), ni - 1)


BF16_ROWS = 16


def _with_normalised_tile(x_ref, g_ref, u_refs, nj, body):
    i, j = pl.program_id(0), pl.program_id(1)
    tm = x_ref.shape[0]
    if nj == 1:
        _normalise_rows(x_ref, g_ref, u_refs[0], 0, tm)
        body(u_refs[0])
        return

    @pl.when((i == 0) & (j == 0))
    def _():
        _normalise_rows(x_ref, g_ref, u_refs[0], 0, tm)

    size = -(-tm // (nj - 1))
    size = -(-size // BF16_ROWS) * BF16_ROWS
    start = pl.multiple_of(jnp.clip((j - 1) * size, 0, tm - size), BF16_ROWS)
    for parity in range(2):
        @pl.when(i % 2 == parity)
        def _():
            body(u_refs[parity])
            _normalise_rows(x_ref, g_ref, u_refs[1 - parity], start, size)


def _norm_mm_kernel(x_ref, g_ref, *refs, act, nj):
    *w_refs, o_ref, u0_ref, u1_ref = refs

    def body(u_ref):
        w = jnp.concatenate([w_ref[...].astype(BF16) for w_ref in w_refs], axis=1)
        acc = jnp.dot(u_ref[...], w, preferred_element_type=F32)
        if act == "relu2":
            acc = jnp.square(jnp.maximum(acc, 0.0))
        elif act == "sigmoid":
            acc = 0.5 * jnp.tanh(0.5 * acc) + 0.5
        o_ref[...] = acc.astype(o_ref.dtype)

    _with_normalised_tile(x_ref, g_ref, (u0_ref, u1_ref), nj, body)


def norm_matmul(x, g, w, layer, *, col0=0, n_cols, tm, tn, pieces=1, act=None, out_dtype=F32):
    M, K = x.shape
    pw = tn // pieces
    assert pw * pieces == tn and col0 % pw == 0 and n_cols % tn == 0
    w_specs = [pl.BlockSpec((None, K, pw), functools.partial(
        lambda i, j, t: (layer, 0, col0 // pw + j * pieces + t), t=t)) for t in range(pieces)]
    ni, nj = M // tm, n_cols // tn
    return pl.pallas_call(
        functools.partial(_norm_mm_kernel, act=act, nj=nj),
        grid=(ni, nj),
        in_specs=[pl.BlockSpec((tm, K), lambda i, j: (_x_rows_index(i, j, ni, nj), 0)),
                  pl.BlockSpec((None, 1, K), lambda i, j: (layer, 0, 0))] + w_specs,
        out_specs=pl.BlockSpec((tm, tn), lambda i, j: (i, j)),
        out_shape=jax.ShapeDtypeStruct((M, n_cols), out_dtype),
        scratch_shapes=[pltpu.VMEM((tm, K), BF16)] * 2,
        compiler_params=_cparams("arbitrary", "arbitrary"),
        name="norm_matmul_" + (act or "id"),
    )(x, g.reshape(-1, 1, K), *([w] * pieces))


def _qkv_proj_kernel(x_ref, g_ref, w_ref, o0_ref, o1_ref, o2_ref, u_ref, acc_ref):
    tm = x_ref.shape[0]

    @pl.when(pl.program_id(1) == 0)
    def _():
        _normalise_rows(x_ref, g_ref, u_ref, 0, tm)

    acc = jnp.dot(u_ref[...], w_ref[...], preferred_element_type=F32)
    acc = acc * jnp.where(pl.program_id(1) == 0, HEAD_DIM ** -0.5, 1.0)
    tiles_per_group = ATTN_OUT_DIM // LANES
    for gi, (o_ref, (_, d)) in enumerate(zip((o0_ref, o1_ref, o2_ref), ATTN_GROUPS)):
        if d == 1:
            o_ref[0, 0] = acc[:, gi * ATTN_OUT_DIM:(gi + 1) * ATTN_OUT_DIM].astype(BF16)
            continue
        for t in range(tiles_per_group):
            c = gi * tiles_per_group + t
            acc_ref[c] = acc[:, c * LANES:(c + 1) * LANES]
            for res in range(d):
                o_ref[0, res, :, t * LANES:(t + 1) * LANES] = (
                    acc_ref[c, pl.ds(res, tm // d, stride=d), :].astype(BF16))


def qkv_proj(x, g, w, layer, *, batch, tm):
    M, K = x.shape
    S = M // batch
    nb = S // tm
    outs_shape, outs_spec = [], []
    for _, d in ATTN_GROUPS:
        outs_shape.append(jax.ShapeDtypeStruct((batch, d, S // d, 3 * ATTN_OUT_DIM), BF16))
        outs_spec.append(pl.BlockSpec((1, d, tm // d, ATTN_OUT_DIM),
                                      lambda i, j: (i // nb, 0, i % nb, j)))
    return pl.pallas_call(
        _qkv_proj_kernel,
        grid=(M // tm, 3),
        in_specs=[pl.BlockSpec((tm, K), lambda i, j: (i, 0)),
                  pl.BlockSpec((None, 1, K), lambda i, j: (layer, 0, 0)),
                  pl.BlockSpec((None, K, ATTN_DIM), lambda i, j: (layer, 0, j))],
        out_specs=outs_spec,
        out_shape=outs_shape,
        scratch_shapes=[pltpu.VMEM((tm, K), BF16), pltpu.VMEM((ATTN_DIM // LANES, tm, LANES), F32)],
        compiler_params=_cparams("parallel", "arbitrary"),
        name="qkv_proj",
    )(x, g.reshape(-1, 1, K), w)


def _mm_res_kernel(x_ref, w_ref, r_ref, o_ref, acc_ref):
    k = pl.program_id(2)

    @pl.when(k == 0)
    def _():
        acc_ref[...] = jnp.zeros_like(acc_ref)

    acc_ref[...] += jnp.dot(x_ref[...], w_ref[...].astype(BF16), preferred_element_type=F32)

    @pl.when(k == pl.num_programs(2) - 1)
    def _():
        o_ref[...] = r_ref[...] + acc_ref[...]


def matmul_residual(x, w, layer, res, *, tm, tn, tk):
    M, K = x.shape
    N = w.shape[2]
    return pl.pallas_call(
        _mm_res_kernel,
        grid=(M // tm, N // tn, K // tk),
        in_specs=[pl.BlockSpec((tm, tk), lambda i, j, k: (i, k)),
                  pl.BlockSpec((None, tk, tn), lambda i, j, k: (layer, k, j)),
                  pl.BlockSpec((tm, tn), lambda i, j, k: (i, j))],
        out_specs=pl.BlockSpec((tm, tn), lambda i, j, k: (i, j)),
        out_shape=jax.ShapeDtypeStruct((M, N), F32),
        scratch_shapes=[pltpu.VMEM((tm, tn), F32)],
        compiler_params=_cparams("parallel", "parallel", "arbitrary"),
        name="matmul_residual",
    )(x, w, res)


def _rmsnorm_kernel(x_ref, g_ref, o_ref):
    x = x_ref[...]
    ms = jnp.mean(x * x, axis=-1, keepdims=True)
    o_ref[...] = x * lax.rsqrt(ms + RMS_EPS) * g_ref[...]


def rmsnorm_rows(x, g, *, tm):
    M, K = x.shape
    return pl.pallas_call(
        _rmsnorm_kernel,
        grid=(M // tm,),
        in_specs=[pl.BlockSpec((tm, K), lambda i: (i, 0)),
                  pl.BlockSpec((1, K), lambda i: (0, 0))],
        out_specs=pl.BlockSpec((tm, K), lambda i: (i, 0)),
        out_shape=jax.ShapeDtypeStruct((M, K), F32),
        compiler_params=_cparams("parallel"),
        name="final_rmsnorm",
    )(x, g.reshape(1, K))


HALO_ROWS = 8


def _neighbour_matrix(rows):
    t = lax.broadcasted_iota(jnp.int32, (rows, rows), 0)
    s = lax.broadcasted_iota(jnp.int32, (rows, rows), 1)
    return ((s == t - 1) | (s == t + 1)).astype(F32).astype(BF16)


def _token_shift(main, prev_blk, next_blk, mu, first, last, neighbours):
    rows = main.shape[0]
    both = jnp.dot(neighbours, main.astype(BF16), preferred_element_type=F32)
    prev_row = jnp.where(first, 0.0, prev_blk[HALO_ROWS - 1:HALO_ROWS, :])
    next_row = jnp.where(last, 0.0, next_blk[0:1, :])
    rid = lax.broadcasted_iota(jnp.int32, (HALO_ROWS, main.shape[1]), 0)
    top = both[:HALO_ROWS] + jnp.where(rid == 0, prev_row, 0.0)
    bottom = both[rows - HALO_ROWS:] + jnp.where(rid == HALO_ROWS - 1, next_row, 0.0)
    both = jnp.concatenate([top, both[HALO_ROWS:rows - HALO_ROWS], bottom], axis=0)
    return main * (1.0 - mu) + (0.5 * mu) * both


def _rwkv_prep_kernel(rkv_m, rkv_p, rkv_n, l_m, l_p, l_n,
                      mu_ref, w0_ref, a0_ref, lora_w_ref, g_up_ref, kk_ref, ka_ref, rk_ref,
                      pack_o, lwf_o, lwb_o):
    i = pl.program_id(1)

    def put(slot, val):
        pack_o[0, :, slot * RWKV_DIM:(slot + 1) * RWKV_DIM] = val.astype(pack_o.dtype)

    first = i == 0
    last = i == pl.num_programs(1) - 1
    D = RWKV_DIM
    neighbours = _neighbour_matrix(rkv_m.shape[1])
    r, k, v = (_token_shift(rkv_m[0, :, c * D:(c + 1) * D], rkv_p[0, :, c * D:(c + 1) * D],
                            rkv_n[0, :, c * D:(c + 1) * D], mu_ref[:, c * D:(c + 1) * D], first, last,
                            neighbours)
               for c in range(3))
    lo = _token_shift(l_m[0], l_p[0], l_n[0], mu_ref[:, 3 * D:], first, last, neighbours)

    g = _dot(jax.nn.sigmoid(lo[:, :GATE_LORA]), g_up_ref[...])
    tail = lo[:, GATE_LORA:]
    col = lax.broadcasted_iota(jnp.int32, tail.shape, 1)
    tail = jnp.where(col < 2 * DECAY_LORA, jnp.tanh(tail), tail)
    up = _dot(tail, lora_w_ref[...])

    ones_bd = _head_block_ones(2 * LANES)

    def head_sum(x):
        return jnp.concatenate(
            [_head_sum(x[:, c:c + 2 * LANES], ones_bd) for c in range(0, RWKV_DIM, 2 * LANES)], axis=1)

    kk = k * kk_ref[...]
    kk = kk * jnp.minimum(lax.rsqrt(head_sum(kk * kk)), 1.0 / L2_EPS)

    for d in range(2):
        put(PACK_DIR * d + 0, r)
        put(PACK_DIR * d + 1, v)
        put(PACK_DIR * d + 2, kk)
    put(PACK_G, g)
    kd_sum = jnp.zeros_like(k)
    for d, lw_o in enumerate((lwf_o, lwb_o)):
        z = w0_ref[d:d + 1, :] + up[:, d * RWKV_DIM:(d + 1) * RWKV_DIM]
        nz = -z
        softplus = jnp.maximum(nz, 0.0) + jnp.log(1.0 + jnp.exp(-jnp.abs(nz)))
        lw_o[0] = -jnp.exp(-softplus - 0.5)
        a = jax.nn.sigmoid(a0_ref[d:d + 1, :] + up[:, (2 + d) * RWKV_DIM:(3 + d) * RWKV_DIM])
        kd = k * (1.0 + (a - 1.0) * ka_ref[...])
        put(PACK_DIR * d + 3, kk * a)
        put(PACK_DIR * d + 4, kd)
        kd_sum = kd_sum + kd
    put(PACK_BONUS, head_sum(r * kd_sum * rk_ref[...]) * v)


def rwkv_prep(rkv, lora, layer, mu, w0, a0, lora_w, g_up, k_k, k_a, r_k, *, ts):
    B, S, _ = rkv.shape
    D = RWKV_DIM
    nblk8 = S // 8

    def main_spec(width, cblk):
        return pl.BlockSpec((1, ts, width), lambda b, i: (b, i, cblk))

    def prev_spec(width, cblk):
        return pl.BlockSpec((1, 8, width), lambda b, i: (b, jnp.maximum(i * (ts // 8) - 1, 0), cblk))

    def next_spec(width, cblk):
        return pl.BlockSpec((1, 8, width), lambda b, i: (b, jnp.minimum((i + 1) * (ts // 8), nblk8 - 1), cblk))

    def layer_spec(rows, width):
        return pl.BlockSpec((None, rows, width), lambda b, i: (layer, 0, 0))

    in_specs = []
    args = []
    in_specs += [main_spec(3 * D, 0), prev_spec(3 * D, 0), next_spec(3 * D, 0)]
    args += [rkv, rkv, rkv]
    in_specs += [main_spec(LORA_COLS, 0), prev_spec(LORA_COLS, 0), next_spec(LORA_COLS, 0)]
    args += [lora, lora, lora]
    depth = mu.shape[0]
    in_specs += [layer_spec(1, 3 * D + LORA_COLS), layer_spec(2, D), layer_spec(2, D),
                 layer_spec(LORA_TAIL, 4 * D), layer_spec(GATE_LORA, D),
                 layer_spec(1, D), layer_spec(1, D), layer_spec(1, D)]
    args += [mu.reshape(depth, 1, -1), w0, a0, lora_w, g_up,
             k_k.reshape(depth, 1, D), k_a.reshape(depth, 1, D), r_k.reshape(depth, 1, D)]
    out_spec = pl.BlockSpec((1, ts, D), lambda b, i: (b, i, 0))
    return pl.pallas_call(
        _rwkv_prep_kernel,
        grid=(B, S // ts),
        in_specs=in_specs,
        out_specs=[pl.BlockSpec((1, ts, PACK_SLOTS * D), lambda b, i: (b, i, 0)), out_spec, out_spec],
        out_shape=[jax.ShapeDtypeStruct((B, S, PACK_SLOTS * D), BF16),
                   jax.ShapeDtypeStruct((B, S, D), F32), jax.ShapeDtypeStruct((B, S, D), F32)],
        compiler_params=_cparams("parallel", "arbitrary"),
        name="rwkv_prep",
    )(*args)


_BMM = (((2,), (1,)), ((0,), (0,)))
_BMM_NT = (((2,), (2,)), ((0,), (0,)))
_BMM_TN = (((1,), (1,)), ((0,), (0,)))


def _bdot(a, b, dims=_BMM):
    return lax.dot_general(a.astype(BF16), b.astype(BF16), dims, preferred_element_type=F32)


def _bd_stack(x, m0):
    zero = jnp.zeros_like(x)
    return jnp.concatenate([jnp.where(m0, x, zero), jnp.where(m0, zero, x)], axis=-2)


def _scan_chunk(r, v, kk, lw, cum, ka, kd, s_bd, consts):
    is_rev, strict2, incl2, eye2, m0_1, m0_2, bd_mask, off_masks = consts
    C = r.shape[1]
    cum_prev = cum - lw
    e_cum = jnp.exp(cum)
    e_prev = jnp.exp(cum_prev)
    e_neg = jnp.exp(-cum)
    total = jnp.where(is_rev, cum[:, 0:1, :], cum[:, C - 1:C, :])
    e_rest = jnp.exp(total - cum)
    r0 = r * e_cum
    at0 = -kk * e_prev
    bt = ka * e_neg
    kt = kd * e_neg
    bh = ka * e_rest
    kh = kd * e_rest

    gram = _bdot(jnp.concatenate([at0, r0], axis=1),
                 jnp.concatenate([_bd_stack(bt, m0_1), _bd_stack(kt, m0_1)], axis=1), _BMM_NT)
    zero = jnp.zeros_like(gram[:, :C])
    a_top = jnp.where(strict2, gram[:, :C], zero)
    a_bot = jnp.where(incl2, gram[:, C:], zero)
    a_ab = a_top[:, :, :LANES]
    a_ak = a_top[:, :, LANES:]

    zero1 = jnp.zeros_like(a_ab)
    t = eye2 + jnp.where(off_masks[0], a_ab, zero1)
    for off_mask in off_masks[1:]:
        z = _bdot(t, _bd_stack(jnp.where(off_mask, a_ab, zero1), m0_1))
        t = t + _bdot(z, _bd_stack(t, m0_1))

    av = _bdot(a_ak, _bd_stack(v, m0_1))
    wu = _bdot(t, _bd_stack(jnp.concatenate([at0, av], axis=2), m0_2))
    w_a = wu[:, :, :LANES]
    u_v = wu[:, :, LANES:]

    wr = _bdot(jnp.concatenate([w_a, r0], axis=1), s_bd, _BMM_NT)
    u = wr[:, :C] + u_v
    y = wr[:, C:] + _bdot(a_bot, jnp.concatenate([_bd_stack(u, m0_1), _bd_stack(v, m0_1)], axis=1))
    upd = _bdot(jnp.concatenate([u, v], axis=1), jnp.concatenate([bh, kh], axis=1), _BMM_TN)
    s_new = s_bd * jnp.exp(total) + jnp.where(bd_mask, upd, jnp.zeros_like(upd))
    return y, s_new


def _scan_consts(C, n_fwd, n_bwd):
    G = n_fwd + n_bwd
    is_rev = lax.broadcasted_iota(jnp.int32, (G, 1, 1), 0) >= n_fwd
    sign = jnp.where(is_rev, -1, 1)
    row2 = lax.broadcasted_iota(jnp.int32, (G, C, 2 * LANES), 1)
    col2 = lax.broadcasted_iota(jnp.int32, (G, C, 2 * LANES), 2) % HEAD_DIM
    strict2 = (col2 - row2) * sign < 0
    incl2 = (col2 - row2) * sign <= 0
    lane1 = lax.broadcasted_iota(jnp.int32, (1, C, LANES), 2)
    lane2 = lax.broadcasted_iota(jnp.int32, (1, C, 2 * LANES), 2) % LANES
    m0_1 = lane1 < HEAD_DIM
    m0_2 = lane2 < HEAD_DIM
    rb = lax.broadcasted_iota(jnp.int32, (1, LANES, LANES), 1) // HEAD_DIM
    cb = lax.broadcasted_iota(jnp.int32, (1, LANES, LANES), 2) // HEAD_DIM
    bd_mask = rb == cb
    row1 = lax.broadcasted_iota(jnp.int32, (1, C, LANES), 1)
    col1 = lane1 % HEAD_DIM
    eye2 = (row1 == col1).astype(F32)
    off_masks = [row1 // 2 == col1 // 2]
    b = 2
    while b < C:
        off_masks.append((row1 // (2 * b) == col1 // (2 * b)) & (row1 // b != col1 // b))
        b *= 2
    return (is_rev, strict2, incl2, eye2, m0_1, m0_2, bd_mask, off_masks)


def _rwkv_scan_kernel(pack_f, pack_b, lwf, lwb, yf_o, yb_o, state_ref):
    c = pl.program_id(0)

    @pl.when(c == 0)
    def _():
        state_ref[...] = jnp.zeros_like(state_ref)

    nb, C, D = lwf.shape
    n_pairs = D // LANES
    n_dir = nb * n_pairs
    consts = _scan_consts(C, n_dir, n_dir)

    row = lax.broadcasted_iota(jnp.int32, (C, C), 0)
    col = lax.broadcasted_iota(jnp.int32, (C, C), 1)
    tri_f = (col <= row).astype(F32).astype(BF16)
    tri_b = (col >= row).astype(F32).astype(BF16)
    cum_f = [_dot_exact_rhs(tri_f, lwf[b]) for b in range(nb)]
    cum_b = [_dot_exact_rhs(tri_b, lwb[b]) for b in range(nb)]

    def chains(xs_f, xs_b):
        return jnp.stack([x[:, p * LANES:(p + 1) * LANES]
                          for xs in (xs_f, xs_b) for x in xs for p in range(n_pairs)], axis=0)

    def slot(ref, k):
        return [ref[b, :, k * D:(k + 1) * D].astype(F32) for b in range(nb)]

    r, v, kk, ka, kd = (chains(slot(pack_f, k), slot(pack_b, k)) for k in range(PACK_DIR))
    lw = chains([lwf[b] for b in range(nb)], [lwb[b] for b in range(nb)])
    y, s_new = _scan_chunk(r, v, kk, lw, chains(cum_f, cum_b), ka, kd, state_ref[...], consts)
    state_ref[...] = s_new
    for d, y_o in enumerate((yf_o, yb_o)):
        for b in range(nb):
            for p in range(n_pairs):
                y_o[b, :, p * LANES:(p + 1) * LANES] = y[(d * nb + b) * n_pairs + p].astype(y_o.dtype)


def rwkv_scan(pack, lwf, lwb):
    B, S, D = lwf.shape
    C = CHUNK
    nc = S // C
    fwd = pl.BlockSpec((B, C, D), lambda c: (0, c, 0))
    bwd = pl.BlockSpec((B, C, D), lambda c: (0, nc - 1 - c, 0))
    return pl.pallas_call(
        _rwkv_scan_kernel,
        grid=(nc,),
        in_specs=[pl.BlockSpec((B, C, PACK_DIR * D), lambda c: (0, c, 0)),
                  pl.BlockSpec((B, C, PACK_DIR * D), lambda c: (0, nc - 1 - c, 1)), fwd, bwd],
        out_specs=[fwd, bwd],
        out_shape=[jax.ShapeDtypeStruct((B, S, D), BF16)] * 2,
        scratch_shapes=[pltpu.VMEM((2 * B * (D // LANES), LANES, LANES), F32)],
        compiler_params=_cparams("arbitrary"),
        name="rwkv_scan",
    )(pack, pack, lwf, lwb)


def _attn_kernel(table_ref, bucket_ref, q_ref, kp_ref, km_ref, kn_ref, vp_ref, vm_ref, vn_ref,
                 o_ref, lse_ref, bias_ref, *, nblk):
    i = pl.program_id(2)
    n_sub = q_ref.shape[1] // ATTN_SUB
    TQ = ATTN_SUB
    TK = TQ + 2 * ATTN_HALF

    @pl.when((pl.program_id(0) == 0) & (pl.program_id(1) == 0) & (i == 0))
    def _():
        bucket = bucket_ref[...]
        row = lax.broadcasted_iota(jnp.int32, (TQ, TK), 0)
        colk = lax.broadcasted_iota(jnp.int32, (TQ, TK), 1)
        band = jnp.abs(colk - ATTN_HALF - row) <= ATTN_HALF
        has_prev = colk >= ATTN_HALF
        has_next = colk < TQ + ATTN_HALF
        for h in range(ATTN_SLOTS):
            acc = jnp.zeros((TQ, TK), F32)
            for b in range(N_BUCKETS):
                acc = jnp.where(bucket == b, table_ref[h * N_BUCKETS + b], acc)
            acc = jnp.where(band, acc, NEG_BIG)
            bias_ref[0, h] = acc
            bias_ref[1, h] = jnp.where(has_prev, acc, NEG_BIG)
            bias_ref[2, h] = jnp.where(has_next, acc, NEG_BIG)
            bias_ref[3, h] = jnp.where(has_prev & has_next, acc, NEG_BIG)

    m0 = lax.broadcasted_iota(jnp.int32, (TQ, LANES), 1) < HEAD_DIM
    lane = lax.broadcasted_iota(jnp.int32, (1, LANES), 1)
    head_keep = [(lane < HEAD_DIM).astype(F32).astype(BF16), (lane >= HEAD_DIM).astype(F32).astype(BF16)]
    for rr, p in ((rr, p) for rr in range(q_ref.shape[0]) for p in range(ATTN_OUT_DIM // LANES)):
        sl = slice(p * LANES, (p + 1) * LANES)
        kfull = jnp.concatenate([kp_ref[rr, :, sl], km_ref[rr, :, sl], kn_ref[rr, :, sl]], axis=0)
        vfull = jnp.concatenate([vp_ref[rr, :, sl], vm_ref[rr, :, sl], vn_ref[rr, :, sl]], axis=0)
        for sb in range(n_sub):
            rows = slice(sb * TQ, (sb + 1) * TQ)
            first = (i == 0) if sb == 0 else False
            last = (i == nblk - 1) if sb == n_sub - 1 else False
            variant = jnp.where(first, 1, 0) + jnp.where(last, 2, 0)
            q2 = q_ref[rr, rows, sl]
            kwin = kfull[sb * TQ:sb * TQ + TK]
            vwin = vfull[sb * TQ:sb * TQ + TK]
            outs, lses = [], []
            for hh in range(2):
                qm = q2 * head_keep[hh]
                s = lax.dot_general(qm, kwin, (((1,), (1,)), ((), ())), preferred_element_type=F32)
                s = s + bias_ref[variant, 2 * p + hh]
                m = jnp.max(s, axis=-1, keepdims=True)
                e = jnp.exp(s - m)
                den = jnp.sum(e, axis=-1, keepdims=True)
                pv = jnp.dot(e.astype(BF16), vwin, preferred_element_type=F32)
                outs.append(pv / den)
                lses.append(jnp.broadcast_to(m + jnp.log(den), (TQ, LANES)))
            o_ref[rr, rows, sl] = jnp.where(m0, outs[0], outs[1]).astype(o_ref.dtype)
            lse_ref[rr, rows, sl] = jnp.where(m0, lses[0], lses[1])


def _t5_bucket(rel):
    nb = N_BUCKETS // 2
    max_exact = nb // 2
    ret = jnp.where(rel > 0, nb, 0)
    n = jnp.abs(rel)
    nf = jnp.maximum(n, 1).astype(jnp.float32)
    large = max_exact + (jnp.log(nf / max_exact) / math.log(MAX_DISTANCE / max_exact)
                         * (nb - max_exact)).astype(jnp.int32)
    large = jnp.minimum(large, nb - 1)
    return ret + jnp.where(n < max_exact, n, large)


def attention_bias_inputs(table, gi, dilation):
    tq = ATTN_SUB
    tk = tq + 2 * ATTN_HALF
    rel = jnp.arange(tk)[None, :] - ATTN_HALF - jnp.arange(tq)[:, None]
    bucket = _t5_bucket(rel * dilation).astype(jnp.int32)
    tbl = table[:, gi * ATTN_SLOTS:(gi + 1) * ATTN_SLOTS].astype(F32).T.reshape(-1)
    return tbl, bucket


def dilated_attention(qkv, tbl, bucket, gi, *, tq):
    B, dilation, L, _ = qkv.shape
    nblk = L // tq
    hb = tq // ATTN_HALF
    nhalf = L // ATTN_HALF
    W = ATTN_OUT_DIM
    nres = max(1, min(dilation, ATTN_MAX_BLOCK // tq))

    def main(off):
        return pl.BlockSpec((None, nres, tq, W), lambda b, r, i: (b, r, i, off))

    def prev(off):
        return pl.BlockSpec((None, nres, ATTN_HALF, W),
                            lambda b, r, i: (b, r, jnp.maximum(i * hb - 1, 0), off))

    def nxt(off):
        return pl.BlockSpec((None, nres, ATTN_HALF, W),
                            lambda b, r, i: (b, r, jnp.minimum((i + 1) * hb, nhalf - 1), off))

    sub_shape = (ATTN_SUB, ATTN_SUB + 2 * ATTN_HALF)
    out_spec = pl.BlockSpec((None, nres, tq, W), lambda b, r, i: (b, r, i, 0))
    return pl.pallas_call(
        functools.partial(_attn_kernel, nblk=nblk),
        grid=(B, dilation // nres, nblk),
        in_specs=[pl.BlockSpec(memory_space=pltpu.SMEM),
                  pl.BlockSpec(sub_shape, lambda b, r, i: (0, 0)),
                  main(0), prev(1), main(1), nxt(1), prev(2), main(2), nxt(2)],
        out_specs=[out_spec, out_spec],
        out_shape=[jax.ShapeDtypeStruct((B, dilation, L, W), BF16),
                   jax.ShapeDtypeStruct((B, dilation, L, W), F32)],
        scratch_shapes=[pltpu.VMEM((4, ATTN_SLOTS) + sub_shape, F32)],
        compiler_params=_cparams("arbitrary", "arbitrary", "arbitrary"),
        name=f"dilated_attn_g{gi}",
    )(tbl, bucket, qkv, qkv, qkv, qkv, qkv, qkv, qkv)


def _branch_post_kernel(yf_ref, yb_ref, bonus_ref, g_ref, gnw_ref, gnb_ref,
                        o0, o1, o2, l0, l1, l2, orw_ref, oat_ref, *scratch):
    y = yf_ref[...].astype(F32) + yb_ref[...].astype(F32)
    ones_bd = _head_block_ones(2 * LANES)
    tm, width = y.shape

    def head_mean(x):
        return jnp.concatenate(
            [_head_sum(x[:, c:c + 2 * LANES], ones_bd) for c in range(0, width, 2 * LANES)],
            axis=1) * (1.0 / HEAD_DIM)

    mean = head_mean(y)
    yc = y - mean
    var = head_mean(yc * yc)
    yn = yc * lax.rsqrt(var + GN_EPS) * gnw_ref[...] + gnb_ref[...]
    orw_ref[...] = ((yn + bonus_ref[...].astype(F32)) * g_ref[...].astype(F32)).astype(BF16)

    def natural_order(ref, scr):
        d = ref.shape[0]
        for t in range(ATTN_OUT_DIM // LANES):
            for res in range(d):
                scr[t, pl.ds(res, tm // d, stride=d), :] = (
                    ref[res, :, t * LANES:(t + 1) * LANES].astype(F32))
        return jnp.concatenate([scr[t] for t in range(ATTN_OUT_DIM // LANES)], axis=1)

    s_o1, s_o2, s_l1, s_l2 = scratch
    oa, ob, oc = o0[0].astype(F32), natural_order(o1, s_o1), natural_order(o2, s_o2)
    la, lb, lc = l0[0], natural_order(l1, s_l1), natural_order(l2, s_l2)
    m = jnp.maximum(jnp.maximum(la, lb), lc)
    ea, eb, ec = jnp.exp(la - m), jnp.exp(lb - m), jnp.exp(lc - m)
    den = ea + eb + ec
    oat_ref[...] = ((ea * oa + eb * ob + ec * oc) / den).astype(BF16)


def branch_post(yf, yb, pack, gn_w, gn_b, layer, outs, lses, *, tm):
    M, D = yf.shape
    A = ATTN_OUT_DIM
    batch = outs[0].shape[0]
    nb = M // batch // tm
    big = pl.BlockSpec((tm, D), lambda b, i: (b * nb + i, 0))
    bonus = pl.BlockSpec((tm, D), lambda b, i: (b * nb + i, PACK_BONUS))
    g = pl.BlockSpec((tm, D), lambda b, i: (b * nb + i, PACK_G))
    rowp = pl.BlockSpec((None, 1, D), lambda b, i: (layer, 0, 0))
    attn_specs = [pl.BlockSpec((None, o.shape[1], tm // o.shape[1], A), lambda b, i: (b, 0, i, 0))
                  for o in outs]
    return pl.pallas_call(
        _branch_post_kernel,
        grid=(batch, nb),
        in_specs=[big, big, bonus, g, rowp, rowp] + attn_specs + attn_specs,
        out_specs=[big, pl.BlockSpec((tm, A), lambda b, i: (b * nb + i, 0))],
        out_shape=[jax.ShapeDtypeStruct((M, D), BF16), jax.ShapeDtypeStruct((M, A), BF16)],
        scratch_shapes=[pltpu.VMEM((A // LANES, tm, LANES), F32)] * 4,
        compiler_params=_cparams("parallel", "arbitrary"),
        name="branch_post",
    )(yf, yb, pack, pack, gn_w.reshape(-1, 1, D), gn_b.reshape(-1, 1, D), *outs, *lses)


def _merge_kernel(orw_ref, oat_ref, wr_ref, wa_ref, gr_ref, ga_ref, o_ref):
    a = jnp.dot(orw_ref[...], wr_ref[...].astype(BF16), preferred_element_type=F32)
    b = jnp.dot(oat_ref[...], wa_ref[...].astype(BF16), preferred_element_type=F32)
    o_ref[...] = (gr_ref[...].astype(F32) * a + ga_ref[...].astype(F32) * b).astype(o_ref.dtype)


def branch_merge(orw, oat, w_r, w_a, layer, gates, *, tm, tn):
    M = orw.shape[0]
    N = w_r.shape[2]
    nj = N // tn
    return pl.pallas_call(
        _merge_kernel,
        grid=(M // tm, nj),
        in_specs=[pl.BlockSpec((tm, orw.shape[1]), lambda i, j: (i, 0)),
                  pl.BlockSpec((tm, oat.shape[1]), lambda i, j: (i, 0)),
                  pl.BlockSpec((None, w_r.shape[1], tn), lambda i, j: (layer, 0, j)),
                  pl.BlockSpec((None, w_a.shape[1], tn), lambda i, j: (layer, 0, j)),
                  pl.BlockSpec((tm, tn), lambda i, j: (i, j)),
                  pl.BlockSpec((tm, tn), lambda i, j: (i, j + nj))],
        out_specs=pl.BlockSpec((tm, tn), lambda i, j: (i, j)),
        out_shape=jax.ShapeDtypeStruct((M, N), BF16),
        compiler_params=_cparams("parallel", "arbitrary"),
        name="branch_merge",
    )(orw, oat, w_r, w_a, gates, gates)


def _lora_weights(w_up, a_up):
    z = jnp.zeros_like(w_up[:, 0])
    rows = [jnp.concatenate([w_up[:, 0], z, z, z], axis=2),
            jnp.concatenate([z, w_up[:, 1], z, z], axis=2),
            jnp.concatenate([z, z, a_up[:, 0], z], axis=2),
            jnp.concatenate([z, z, z, a_up[:, 1]], axis=2)]
    return jnp.concatenate(rows, axis=1).astype(BF16)


def kernel(x, norm1_g, w_in, tshift_mu, w0, w_lora_up, a0, a_lora_up, g_lora_up, k_k, k_a, r_k,
           gn_w, gn_b, rel_bias, w_branch_rwkv, w_branch_attn, w_out, norm2_g, w_mlp_in, w_mlp_out,
           final_g):
    B, S, D = x.shape
    M = B * S
    depth = w_in.shape[0]
    c_rkv = 3 * RWKV_DIM
    c_slab = c_rkv + LORA_COLS
    c_attn = c_slab + 3 * ATTN_DIM
    w_qkv = w_in[:, :, c_slab:c_attn].astype(BF16)
    w_lora_up = _lora_weights(w_lora_up, a_lora_up)
    g_up = g_lora_up.astype(BF16)
    bias_inputs = [attention_bias_inputs(rel_bias, gi, dilation)
                   for gi, (_, dilation) in enumerate(ATTN_GROUPS)]
    h = x.reshape(M, D)
    for l in range(depth):
        g1 = norm1_g
        rkv = norm_matmul(h, g1, w_in, l, n_cols=c_rkv, tm=1024, tn=1024)
        lora = norm_matmul(h, g1, w_in, l, col0=c_rkv, n_cols=LORA_COLS, tm=1024, tn=LORA_COLS,
                           pieces=LORA_COLS // LANES)
        qkv_groups = qkv_proj(h, g1, w_qkv, l, batch=B, tm=1024)
        gates = norm_matmul(h, g1, w_in, l, col0=c_attn, n_cols=2 * D, tm=1024, tn=1024,
                            pieces=1024 // LANES, act="sigmoid", out_dtype=BF16)

        pack, lwf, lwb = rwkv_prep(
            rkv.reshape(B, S, c_rkv), lora.reshape(B, S, LORA_COLS), l, tshift_mu, w0, a0,
            w_lora_up, g_up, k_k, k_a, r_k, ts=256)
        yf, yb = rwkv_scan(pack, lwf, lwb)

        outs, lses = [], []
        for gi, (window, dilation) in enumerate(ATTN_GROUPS):
            assert window // (2 * dilation) == ATTN_HALF
            o, lse = dilated_attention(qkv_groups[gi], *bias_inputs[gi], gi,
                                       tq=min(S // dilation, ATTN_MAX_BLOCK))
            outs.append(o)
            lses.append(lse)

        orw, oat = branch_post(yf.reshape(M, RWKV_DIM), yb.reshape(M, RWKV_DIM),
                               pack.reshape(M, PACK_SLOTS * RWKV_DIM), gn_w, gn_b, l, outs, lses,
                               tm=512)
        merged = branch_merge(orw, oat, w_branch_rwkv, w_branch_attn, l, gates, tm=2048, tn=512)
        h = matmul_residual(merged, w_out, l, h, tm=1024, tn=1024, tk=D)

        act = norm_matmul(h, norm2_g, w_mlp_in, l, n_cols=w_mlp_in.shape[2], tm=1024, tn=1024,
                          act="relu2", out_dtype=BF16)
        h = matmul_residual(act, w_mlp_out, l, h, tm=1024, tn=1024, tk=2048)
    out = rmsnorm_rows(h, final_g, tm=512)
    return out.reshape(B, S, D)
```
